```python
import jax, jax.numpy as jnp
from jax import lax
import numpy as np

D_MODEL = 1024
BATCH = 8
SEQ = 4096
DEPTH = 4

CTX_LEN = 256
GRID_W = 64

POOL_WIDTH = D_MODEL // 4
POOL_WINDOWS = (2, 4, 8, 16)
POOL_GROUP = POOL_WIDTH // len(POOL_WINDOWS)
SGU_WIDTH = D_MODEL // 4
SGU_HEADS = 4
SGU_HEAD_DIM = SGU_WIDTH // SGU_HEADS
SGU_CHUNK = 128
ATTN_WIDTH = D_MODEL - POOL_WIDTH - SGU_WIDTH
HEAD_DIM = 64
N_HEADS = ATTN_WIDTH // HEAD_DIM
N_KV_HEADS = 2
KV_GROUP = N_HEADS // N_KV_HEADS
KV_WIDTH = N_KV_HEADS * HEAD_DIM
Q_BLOCK = 128
ROPE_THETA = 10000.0
ROPE_AXIS_FREQS = HEAD_DIM // 4
MIX_WIDTH = POOL_WIDTH + SGU_WIDTH + ATTN_WIDTH

OFF_POOL = 0
OFF_U = OFF_POOL + POOL_WIDTH
OFF_V = OFF_U + SGU_WIDTH
OFF_Q = OFF_V + SGU_WIDTH
OFF_K = OFF_Q + ATTN_WIDTH
OFF_VAL = OFF_K + KV_WIDTH
IN_WIDTH = OFF_VAL + KV_WIDTH

N_EXPERTS = 16
EXPERT_FF = D_MODEL
EC_CAPACITY_FACTOR = 2

DN_ALPHA = (2 * DEPTH) ** 0.25
DN_BETA = (8 * DEPTH) ** -0.25
LN_EPS = 1e-6

kernel_name = "hybrid_pool_sgu_gqa_ec_moe_deepnorm"


def layer_norm(x, g=None, b=None):
    xf = x.astype(jnp.float32)
    mu = jnp.mean(xf, axis=-1, keepdims=True)
    var = jnp.mean(jnp.square(xf - mu), axis=-1, keepdims=True)
    y = (xf - mu) * lax.rsqrt(var + LN_EPS)
    if g is not None:
        y = y * g.astype(jnp.float32) + b.astype(jnp.float32)
    return y.astype(x.dtype)


def rms_norm(x, g):
    xf = x.astype(jnp.float32)
    y = xf * lax.rsqrt(jnp.mean(jnp.square(xf), axis=-1, keepdims=True) + LN_EPS)
    return (y * g.astype(jnp.float32)).astype(x.dtype)


def modulation(cond, w_mod, b_mod):
    m = jax.nn.silu(cond) @ w_mod + b_mod
    return jnp.split(m[:, None, :], 6, axis=-1)


def modulate(x, shift, scale):
    return layer_norm(x) * (1 + scale) + shift


def axial_rope(n):
    rows = n // GRID_W
    r = jnp.repeat(jnp.arange(rows, dtype=jnp.float32), GRID_W)
    col = jnp.tile(jnp.arange(GRID_W, dtype=jnp.float32), rows)
    inv = ROPE_THETA ** (-jnp.arange(ROPE_AXIS_FREQS, dtype=jnp.float32) / ROPE_AXIS_FREQS)
    ang = jnp.concatenate([r[:, None] * inv, col[:, None] * inv], axis=-1)
    return jnp.cos(ang), jnp.sin(ang)


def apply_rope(x, cos, sin):
    half = HEAD_DIM // 2
    xf = x.astype(jnp.float32)
    x1, x2 = xf[..., :half], xf[..., half:]
    cs, sn = cos[None, :, None, :], sin[None, :, None, :]
    return jnp.concatenate([x1 * cs - x2 * sn, x2 * cs + x1 * sn], axis=-1).astype(x.dtype)


def heads(z, off, n):
    return z[..., off:off + n * HEAD_DIM].reshape(z.shape[0], z.shape[1], n, HEAD_DIM)


def multiscale_pool(p, w, scale):
    L = p.shape[1]
    t = jnp.arange(L)
    outs = []
    for g, win in enumerate(POOL_WINDOWS):
        pg = p[..., g * POOL_GROUP:(g + 1) * POOL_GROUP].astype(jnp.float32)
        cs = jnp.concatenate([jnp.zeros_like(pg[:, :1]), jnp.cumsum(pg, axis=1)], axis=1)
        lo = jnp.clip(t - win // 2, 0, L)
        hi = jnp.clip(t + win // 2, 0, L)
        mean = (cs[:, hi] - cs[:, lo]) / (hi - lo).astype(jnp.float32)[None, :, None]
        outs.append((mean - pg).astype(p.dtype) @ w[g])
    return jnp.concatenate(outs, axis=-1) * scale


def spatial_gating(u, v, g, w_s, b_s):
    B, L, _ = u.shape
    shp = (B, L // SGU_CHUNK, SGU_CHUNK, SGU_HEADS, SGU_HEAD_DIM)
    uh = jax.nn.gelu(u).reshape(shp)
    vh = layer_norm(jax.nn.gelu(v).reshape(shp)) * g
    mixed = jnp.einsum('hpq,bcqhd->bcphd', w_s, vh) + b_s.T[:, :, None]
    return (uh * mixed).reshape(B, L, SGU_WIDTH)


def block_attention(q, k, v):
    B, Lq = q.shape[0], q.shape[1]
    nb = Lq // Q_BLOCK
    qb = q.reshape(B, nb, Q_BLOCK, N_KV_HEADS, KV_GROUP, HEAD_DIM).transpose(1, 0, 2, 3, 4, 5)
    scale = HEAD_DIM ** -0.5

    def one_block(qi):
        s = jnp.einsum('bqkgd,bskd->bkgqs', qi, k).astype(jnp.float32) * scale
        p = jax.nn.softmax(s, axis=-1).astype(v.dtype)
        return jnp.einsum('bkgqs,bskd->bqkgd', p, v)

    o = lax.map(one_block, qb)
    return o.transpose(1, 0, 2, 3, 4, 5).reshape(B, Lq, N_HEADS * HEAD_DIM)


def mixer_output(z, attn, pool_w, pool_scale, sgu_g, sgu_w, sgu_b, w_out):
    pool = multiscale_pool(z[..., OFF_POOL:OFF_U], pool_w, pool_scale)
    sgu = spatial_gating(z[..., OFF_U:OFF_V], z[..., OFF_V:OFF_Q], sgu_g, sgu_w, sgu_b)
    return jnp.concatenate([pool, sgu, attn], axis=-1) @ w_out


def expert_choice_moe(h, w_router, w1, w3, w2):
    B, L, D = h.shape
    cap = EC_CAPACITY_FACTOR * L // N_EXPERTS
    aff = jax.nn.softmax((h @ w_router).astype(jnp.float32), axis=-1)
    gate, idx = lax.top_k(jnp.swapaxes(aff, 1, 2), cap)
    xg = jax.vmap(lambda hb, ib: hb[ib])(h, idx)
    hid = jax.nn.silu(jnp.einsum('becd,edf->becf', xg, w1)) * jnp.einsum('becd,edf->becf', xg, w3)
    y = jnp.einsum('becf,efd->becd', hid, w2) * gate[..., None].astype(h.dtype)
    return jax.vmap(lambda ib, yb: jnp.zeros((L, D), yb.dtype).at[ib.reshape(-1)].add(yb.reshape(-1, D)))(idx, y)


def setup_inputs(seed: int = 0) -> dict:
    key = jax.random.key(seed)
    ks = jax.random.split(key, 24)
    f32 = jnp.float32

    def nrm(k, shape, std):
        return jax.random.normal(k, shape, f32) * std

    w_in = nrm(ks[6], (DEPTH, D_MODEL, IN_WIDTH), D_MODEL ** -0.5)
    w_in = w_in.at[:, :, OFF_VAL:].multiply(DN_BETA)
    return {
        "x": nrm(ks[0], (BATCH, SEQ, D_MODEL), 1.0),
        "c": nrm(ks[1], (BATCH, D_MODEL), 1.0),
        "ctx": nrm(ks[2], (BATCH, CTX_LEN, D_MODEL), 1.0),
        "c_ctx": nrm(ks[3], (D_MODEL,), 1.0),
        "w_mod": nrm(ks[4], (DEPTH, D_MODEL, 6 * D_MODEL), 0.5 * D_MODEL ** -0.5),
        "b_mod": nrm(ks[5], (DEPTH, 6 * D_MODEL), 0.02),
        "w_in": w_in,
        "pool_w": nrm(ks[7], (DEPTH, len(POOL_WINDOWS), POOL_GROUP, POOL_GROUP), POOL_GROUP ** -0.5),
        "pool_scale": 1.0 + nrm(ks[8], (DEPTH, POOL_WIDTH), 0.02),
        "sgu_g": 1.0 + nrm(ks[9], (DEPTH, SGU_HEADS, SGU_HEAD_DIM), 0.02),
        "sgu_w": nrm(ks[10], (DEPTH, SGU_HEADS, SGU_CHUNK, SGU_CHUNK), 0.5 * SGU_CHUNK ** -0.5),
        "sgu_b": 1.0 + nrm(ks[11], (DEPTH, SGU_HEADS, SGU_CHUNK), 0.02),
        "q_g": 1.0 + nrm(ks[12], (DEPTH, HEAD_DIM), 0.02),
        "k_g": 1.0 + nrm(ks[13], (DEPTH, HEAD_DIM), 0.02),
        "w_out": nrm(ks[14], (DEPTH, MIX_WIDTH, D_MODEL), DN_BETA * MIX_WIDTH ** -0.5),
        "ln1_g": 1.0 + nrm(ks[15], (DEPTH, D_MODEL), 0.02),
        "ln1_b": nrm(ks[16], (DEPTH, D_MODEL), 0.02),
        "w_router": nrm(ks[17], (DEPTH, D_MODEL, N_EXPERTS), D_MODEL ** -0.5),
        "w1": nrm(ks[18], (DEPTH, N_EXPERTS, D_MODEL, EXPERT_FF), D_MODEL ** -0.5),
        "w3": nrm(ks[19], (DEPTH, N_EXPERTS, D_MODEL, EXPERT_FF), D_MODEL ** -0.5),
        "w2": nrm(ks[20], (DEPTH, N_EXPERTS, EXPERT_FF, D_MODEL), DN_BETA * EXPERT_FF ** -0.5),
        "ln2_g": 1.0 + nrm(ks[21], (DEPTH, D_MODEL), 0.02),
        "ln2_b": nrm(ks[22], (DEPTH, D_MODEL), 0.02),
    }


def reference(x, c, ctx, c_ctx, w_mod, b_mod, w_in, pool_w, pool_scale, sgu_g, sgu_w, sgu_b,
              q_g, k_g, w_out, ln1_g, ln1_b, w_router, w1, w3, w2, ln2_g, ln2_b):
    B, S, _ = x.shape
    cos, sin = axial_rope(S)
    xc = ctx
    for l in range(DEPTH):
        last = l == DEPTH - 1
        sh1, sc1, g1, sh2, sc2, g2 = modulation(c, w_mod[l], b_mod[l])
        csh1, csc1, cg1, csh2, csc2, cg2 = modulation(c_ctx[None, :], w_mod[l], b_mod[l])

        hl = modulate(x, sh1, sc1)
        hc = modulate(xc, csh1, csc1)
        zl = hl @ w_in[l]
        zc_kv = hc @ w_in[l][:, OFF_K:]
        kc = rms_norm(zc_kv[..., :KV_WIDTH].reshape(B, -1, N_KV_HEADS, HEAD_DIM), k_g[l])
        vc = zc_kv[..., KV_WIDTH:].reshape(B, -1, N_KV_HEADS, HEAD_DIM)

        ql = apply_rope(rms_norm(heads(zl, OFF_Q, N_HEADS), q_g[l]), cos, sin)
        kl = apply_rope(rms_norm(heads(zl, OFF_K, N_KV_HEADS), k_g[l]), cos, sin)
        vl = heads(zl, OFF_VAL, N_KV_HEADS)
        attn_l = block_attention(ql, jnp.concatenate([kl, kc], axis=1), jnp.concatenate([vl, vc], axis=1))
        yl = mixer_output(zl, attn_l, pool_w[l], pool_scale[l], sgu_g[l], sgu_w[l], sgu_b[l], w_out[l])

        if not last:
            zc = hc @ w_in[l][:, :OFF_K]
            qc = rms_norm(heads(zc, OFF_Q, N_HEADS), q_g[l])
            attn_c = block_attention(qc, kc, vc)
            yc = mixer_output(zc, attn_c, pool_w[l], pool_scale[l], sgu_g[l], sgu_w[l], sgu_b[l], w_out[l])
            xc = layer_norm(DN_ALPHA * xc + cg1 * yc, ln1_g[l], ln1_b[l])
            mc = expert_choice_moe(modulate(xc, csh2, csc2), w_router[l], w1[l], w3[l], w2[l])
            xc = layer_norm(DN_ALPHA * xc + cg2 * mc, ln2_g[l], ln2_b[l])

        x = layer_norm(DN_ALPHA * x + g1 * yl, ln1_g[l], ln1_b[l])
        ml = expert_choice_moe(modulate(x, sh2, sc2), w_router[l], w1[l], w3[l], w2[l])
        x = layer_norm(DN_ALPHA * x + g2 * ml, ln2_g[l], ln2_b[l])
    return x
```

```python
import functools
import math

import jax
import jax.numpy as jnp
from jax import lax
from jax.experimental import pallas as pl
from jax.experimental.pallas import tpu as pltpu

F32 = jnp.float32
BF16 = jnp.bfloat16

GRID_W = 64
POOL_WIDTH = 256
POOL_WINDOWS = (2, 4, 8, 16)
POOL_GROUP = 64
POOL_HALO = 8
SGU_WIDTH = 256
SGU_HEADS = 4
SGU_CHUNK = 128
HEAD_DIM = 64
N_HEADS = 8
N_KV_HEADS = 2
KV_GROUP = N_HEADS // N_KV_HEADS
ATTN_WIDTH = N_HEADS * HEAD_DIM
KV_WIDTH = N_KV_HEADS * HEAD_DIM
PU_WIDTH = POOL_WIDTH + 2 * SGU_WIDTH
QK_WIDTH = ATTN_WIDTH + KV_WIDTH
ROPE_THETA = 10000.0
ROPE_AXIS_FREQS = HEAD_DIM // 4
N_EXPERTS = 16
EC_CAPACITY_FACTOR = 2
LN_EPS = 1e-6

LANES = 128
SUBLANES = 8
BF16_ROWS = 16
VMEM_LIMIT_BYTES = 56 * 1024 * 1024

ROW_TILE = 256
KV_TILE = 256
TOKEN_BLOCK = 128
COND_ROWS = 16


def _params(*sem):
    return pltpu.CompilerParams(dimension_semantics=sem, vmem_limit_bytes=VMEM_LIMIT_BYTES)


def _split(x):
    hi = x.astype(BF16)
    lo = (x - hi.astype(F32)).astype(BF16)
    return hi, lo


def _dot(a, b):
    return jnp.dot(a, b, preferred_element_type=F32)


def _dot3(a, b_hi, b_lo):
    a_hi, a_lo = _split(a)
    return _dot(a_hi, b_hi) + _dot(a_hi, b_lo) + _dot(a_lo, b_hi)


def _dot_exact_rhs(a, b):
    a_hi, a_lo = _split(a)
    return _dot(a_hi, b) + _dot(a_lo, b)


def _layer_norm(x):
    mu = jnp.mean(x, axis=-1, keepdims=True)
    xc = x - mu
    var = jnp.mean(xc * xc, axis=-1, keepdims=True)
    return xc * lax.rsqrt(var + LN_EPS)


def _sigmoid(x):
    return 1.0 / (1.0 + jnp.exp(-x))


def _gelu_tanh(x):
    c = math.sqrt(2.0 / math.pi)
    return 0.5 * x * (1.0 + jnp.tanh(c * (x + 0.044715 * (x * x * x))))


def _mod_kernel(cond_ref, w_ref, b_ref, o_ref):
    a = cond_ref[...]
    a = a * _sigmoid(a)
    w_hi, w_lo = _split(w_ref[0])
    o_ref[0] = _dot3(a, w_hi, w_lo) + b_ref[0]


def _modulation(cond, w_mod, b_mod):
    depth, d, n = w_mod.shape
    tn = n // 4
    return pl.pallas_call(
        _mod_kernel,
        grid=(depth, n // tn),
        in_specs=[
            pl.BlockSpec((COND_ROWS, d), lambda l, j: (0, 0)),
            pl.BlockSpec((1, d, tn), lambda l, j: (l, 0, j)),
            pl.BlockSpec((1, 1, tn), lambda l, j: (l, 0, j)),
        ],
        out_specs=pl.BlockSpec((1, COND_ROWS, tn), lambda l, j: (l, 0, j)),
        out_shape=jax.ShapeDtypeStruct((depth, COND_ROWS, n), F32),
        compiler_params=_params("arbitrary", "arbitrary"),
        name="modulation",
    )(cond, w_mod, b_mod.reshape(depth, 1, n))


def _in_kernel(x_ref, sh_ref, sc_ref, w_ref, gain_ref, cos_ref, sin_ref, gavg_ref,
               zpu_ref, qt_ref, k_ref, vt_ref):
    x = x_ref[0]
    h = _layer_norm(x) * (1.0 + sc_ref[0]) + sh_ref[0]
    z = _dot(h.astype(BF16), w_ref[...])
    zpu_ref[0] = z[:, :PU_WIDTH]

    gavg = gavg_ref[...]
    cos = cos_ref[...]
    sin = sin_ref[...]
    lane = lax.broadcasted_iota(jnp.int32, cos.shape, 1)
    first_half = (lane & (HEAD_DIM - 1)) < (HEAD_DIM // 2)
    pieces = []
    for c in range(QK_WIDTH // LANES):
        lo = PU_WIDTH + c * LANES
        y = z[:, lo:lo + LANES]
        ms = _dot_exact_rhs(y * y, gavg)
        y = y * lax.rsqrt(ms + LN_EPS) * gain_ref[:, c * LANES:(c + 1) * LANES]
        partner = jnp.where(first_half,
                            pltpu.roll(y, LANES - HEAD_DIM // 2, 1),
                            pltpu.roll(y, HEAD_DIM // 2, 1))
        pieces.append(y * cos + partner * sin)
    q = jnp.concatenate(pieces[:ATTN_WIDTH // LANES], axis=1)
    qt_ref[0] = q.T.astype(BF16)
    k_ref[0] = pieces[-1].astype(BF16)

    v = z[:, PU_WIDTH + QK_WIDTH:]
    vt = v.T
    row = lax.broadcasted_iota(jnp.int32, (HEAD_DIM, vt.shape[1]), 0)
    ones_row = (row == 0).astype(F32)
    for g in range(N_KV_HEADS):
        ext = jnp.concatenate([vt[g * HEAD_DIM:(g + 1) * HEAD_DIM], ones_row], axis=0)
        vt_ref[0, g, 0] = ext.astype(BF16)


def _in_projection(x, sh, sc, w_in, gain, cos, sin, gavg):
    b, l, d = x.shape
    n = w_in.shape[1]
    tm = ROW_TILE
    nt = l // tm
    vec = pl.BlockSpec((1, 1, d), lambda i, j: (i, 0, 0))
    return pl.pallas_call(
        _in_kernel,
        grid=(b, nt),
        in_specs=[
            pl.BlockSpec((1, tm, d), lambda i, j: (i, j, 0)),
            vec, vec,
            pl.BlockSpec((d, n), lambda i, j: (0, 0)),
            pl.BlockSpec((1, QK_WIDTH), lambda i, j: (0, 0)),
            pl.BlockSpec((tm, LANES), lambda i, j: (j, 0)),
            pl.BlockSpec((tm, LANES), lambda i, j: (j, 0)),
            pl.BlockSpec((LANES, LANES), lambda i, j: (0, 0)),
        ],
        out_specs=[
            pl.BlockSpec((1, tm, PU_WIDTH), lambda i, j: (i, j, 0)),
            pl.BlockSpec((1, ATTN_WIDTH, tm), lambda i, j: (i, 0, j)),
            pl.BlockSpec((1, tm, KV_WIDTH), lambda i, j: (i, j, 0)),
            pl.BlockSpec((1, N_KV_HEADS, 1, 2 * HEAD_DIM, tm), lambda i, j: (i, 0, j, 0, 0)),
        ],
        out_shape=[
            jax.ShapeDtypeStruct((b, l, PU_WIDTH), F32),
            jax.ShapeDtypeStruct((b, ATTN_WIDTH, l), BF16),
            jax.ShapeDtypeStruct((b, l, KV_WIDTH), BF16),
            jax.ShapeDtypeStruct((b, N_KV_HEADS, nt, 2 * HEAD_DIM, tm), BF16),
        ],
        compiler_params=_params("arbitrary", "arbitrary"),
        name="in_projection",
    )(x, sh, sc, w_in, gain, cos, sin, gavg)


def _attn_kernel(qt_ref, k_ref, vt_ref, o_ref, *, n_kv):
    tq = qt_ref.shape[2]
    zeros = jnp.zeros((HEAD_DIM, tq), BF16)
    blocks = []
    for hp in range(N_HEADS // 2):
        pair = []
        for hh in range(2):
            h = hp * 2 + hh
            g = h // KV_GROUP
            qh = qt_ref[0, h * HEAD_DIM:(h + 1) * HEAD_DIM, :]
            qpad = jnp.concatenate([qh, zeros] if g == 0 else [zeros, qh], axis=0)

            def body(c, carry, qpad=qpad, g=g):
                m, acc = carry
                start = pl.multiple_of(c * KV_TILE, KV_TILE)
                s = _dot(k_ref[0, pl.ds(start, KV_TILE), :], qpad)
                m_new = jnp.maximum(m, jnp.max(s, axis=0, keepdims=True))
                p = jnp.exp2(s - m_new)
                acc = acc * jnp.exp2(m - m_new) + _dot(vt_ref[0, g, c], p.astype(BF16))
                return m_new, acc

            m0 = jnp.full((1, tq), -1e30, F32)
            acc0 = jnp.zeros((2 * HEAD_DIM, tq), F32)
            _, acc = lax.fori_loop(0, n_kv, body, (m0, acc0))
            pair.append(acc[:HEAD_DIM] / acc[HEAD_DIM:HEAD_DIM + 1])
        blocks.append(jnp.concatenate(pair, axis=0).T)
    o_ref[0] = jnp.concatenate(blocks, axis=1).astype(BF16)


def _attention(qt, k, vt):
    b, _, lq = qt.shape
    lk = k.shape[1]
    n_kv = lk // KV_TILE
    tq = ROW_TILE
    return pl.pallas_call(
        functools.partial(_attn_kernel, n_kv=n_kv),
        grid=(b, lq // tq),
        in_specs=[
            pl.BlockSpec((1, ATTN_WIDTH, tq), lambda i, j: (i, 0, j)),
            pl.BlockSpec((1, lk, KV_WIDTH), lambda i, j: (i, 0, 0)),
            pl.BlockSpec((1, N_KV_HEADS, n_kv, 2 * HEAD_DIM, KV_TILE), lambda i, j: (i, 0, 0, 0, 0)),
        ],
        out_specs=pl.BlockSpec((1, tq, ATTN_WIDTH), lambda i, j: (i, j, 0)),
        out_shape=jax.ShapeDtypeStruct((b, lq, ATTN_WIDTH), BF16),
        compiler_params=_params("arbitrary", "arbitrary"),
        name="attention",
    )(qt, k, vt)


def _mix_kernel(zp_ref, zn_ref, z_ref, attn_ref, x_ref, g1_ref, sh2_ref, sc2_ref,
                poolw_ref, pscale_ref, gavg_ref, sgug_ref, sguw_ref, sgub_ref,
                wout_ref, lng_ref, lnb_ref, wr_hi_ref, wr_lo_ref,
                xo_ref, h2_ref, logit_ref, *, seq_len, alpha):
    j = pl.program_id(1)
    nt = pl.num_programs(1)
    z = z_ref[0]
    tm = z.shape[0]
    lane = lax.broadcasted_iota(jnp.int32, (tm, POOL_WIDTH), 1)
    group = lax.shift_right_logical(lane, POOL_GROUP.bit_length() - 1)

    p = z[:, :POOL_WIDTH]
    prev = zp_ref[0] * (j > 0).astype(F32)
    nxt = zn_ref[0] * (j < nt - 1).astype(F32)
    ext = jnp.concatenate([prev, p, nxt], axis=0)
    n_ext = ext.shape[0]
    s2 = ext + pltpu.roll(ext, 1, 0)
    s4 = pltpu.roll(s2, 1, 0) + pltpu.roll(s2, n_ext - 1, 0)
    s8 = pltpu.roll(s4, 2, 0) + pltpu.roll(s4, n_ext - 2, 0)
    s16 = pltpu.roll(s8, 4, 0) + pltpu.roll(s8, n_ext - 4, 0)
    sums = [s[POOL_HALO:POOL_HALO + tm] for s in (s2, s4, s8, s16)]
    wsum = jnp.where(group == 0, sums[0], jnp.where(group == 1, sums[1], jnp.where(group == 2, sums[2], sums[3])))
    half = jnp.where(group == 0, 1, jnp.where(group == 1, 2, jnp.where(group == 2, 4, 8)))
    t = j * tm + lax.broadcasted_iota(jnp.int32, (tm, POOL_WIDTH), 0)
    cnt = jnp.minimum(t + half, seq_len) - jnp.maximum(t - half, 0)
    pin = (wsum / cnt.astype(F32) - p).astype(BF16)
    pool = _dot(pin, poolw_ref[...]) * pscale_ref[...]

    gavg = gavg_ref[...]
    head = lax.shift_right_logical(lax.broadcasted_iota(jnp.int32, (SGU_CHUNK, SGU_WIDTH), 1),
                                   (SGU_WIDTH // SGU_HEADS).bit_length() - 1)
    sgu_parts = []
    for c in range(tm // SGU_CHUNK):
        rows = slice(c * SGU_CHUNK, (c + 1) * SGU_CHUNK)
        u = _gelu_tanh(z[rows, POOL_WIDTH:POOL_WIDTH + SGU_WIDTH])
        v = _gelu_tanh(z[rows, POOL_WIDTH + SGU_WIDTH:])
        mu = _dot_exact_rhs(v, gavg)
        vc = v - mu
        var = _dot_exact_rhs(vc * vc, gavg)
        vn = (vc * lax.rsqrt(var + LN_EPS) * sgug_ref[...]).astype(BF16)
        mixed = sgub_ref[...]
        for hd in range(SGU_HEADS):
            mixed = mixed + jnp.where(head == hd, _dot(sguw_ref[hd], vn), 0.0)
        sgu_parts.append(u * mixed)
    sgu = jnp.concatenate(sgu_parts, axis=0)

    ps = jnp.concatenate([pool, sgu], axis=1).astype(BF16)
    split = POOL_WIDTH + SGU_WIDTH
    y = _dot(ps, wout_ref[:split, :]) + _dot(attn_ref[0], wout_ref[split:, :])
    xn = _layer_norm(alpha * x_ref[0] + g1_ref[0] * y) * lng_ref[...] + lnb_ref[...]
    xo_ref[0] = xn

    h2 = _layer_norm(xn) * (1.0 + sc2_ref[0]) + sh2_ref[0]
    h2_ref[0] = h2.astype(BF16)
    logit_ref[0] = _dot3(h2, wr_hi_ref[...], wr_lo_ref[...])


def _mixer_output(zpu, attn, x, g1, sh2, sc2, lw, alpha):
    b, l, d = x.shape
    tm = ROW_TILE
    nt = l // tm
    hb = tm // POOL_HALO
    n_halo = l // POOL_HALO
    vec = pl.BlockSpec((1, 1, d), lambda i, j: (i, 0, 0))

    def full(a):
        nd = a.ndim
        return pl.BlockSpec(a.shape, lambda i, j: (0,) * nd)

    weights = [lw["pool_w"], lw["pool_scale"], lw["gavg256"], lw["sgu_g"], lw["sgu_w"], lw["sgu_b"],
               lw["w_out"], lw["ln1_g"], lw["ln1_b"], lw["wr_hi"], lw["wr_lo"]]
    return pl.pallas_call(
        functools.partial(_mix_kernel, seq_len=l, alpha=alpha),
        grid=(b, nt),
        in_specs=[
            pl.BlockSpec((1, POOL_HALO, POOL_WIDTH), lambda i, j: (i, jnp.maximum(j * hb - 1, 0), 0)),
            pl.BlockSpec((1, POOL_HALO, POOL_WIDTH), lambda i, j: (i, jnp.minimum((j + 1) * hb, n_halo - 1), 0)),
            pl.BlockSpec((1, tm, PU_WIDTH), lambda i, j: (i, j, 0)),
            pl.BlockSpec((1, tm, ATTN_WIDTH), lambda i, j: (i, j, 0)),
            pl.BlockSpec((1, tm, d), lambda i, j: (i, j, 0)),
            vec, vec, vec,
        ] + [full(w) for w in weights],
        out_specs=[
            pl.BlockSpec((1, tm, d), lambda i, j: (i, j, 0)),
            pl.BlockSpec((1, tm, d), lambda i, j: (i, j, 0)),
            pl.BlockSpec((1, tm, LANES), lambda i, j: (i, j, 0)),
        ],
        out_shape=[
            jax.ShapeDtypeStruct((b, l, d), F32),
            jax.ShapeDtypeStruct((b, l, d), BF16),
            jax.ShapeDtypeStruct((b, l, LANES), F32),
        ],
        compiler_params=_params("arbitrary", "arbitrary"),
        name="mixer_output",
    )(zpu, zpu, zpu, attn, x, g1, sh2, sc2, *weights)


def _route_kernel(logit_ref, tri_ref, rank_ref, gate_ref, start_ref, bits_ref, eq_ref, *, cap):
    l = logit_ref.shape[1]
    nb = l // TOKEN_BLOCK
    lane = lax.broadcasted_iota(jnp.int32, (l, LANES), 1)
    valid = lane < N_EXPERTS
    x = jnp.where(valid, logit_ref[0], -jnp.inf)
    e = jnp.exp(x - jnp.max(x, axis=1, keepdims=True))
    aff = e / jnp.sum(e, axis=1, keepdims=True)
    gate_ref[0] = aff
    bits_ref[...] = pltpu.bitcast(aff, jnp.int32)

    def bisect(i, thr):
        cand = thr | lax.shift_left(jnp.int32(1), 30 - i)
        cnt = jnp.sum((bits_ref[...] >= cand).astype(F32), axis=0, keepdims=True)
        return jnp.where(cnt >= float(cap), cand, thr)

    thr = lax.fori_loop(0, 31, bisect, jnp.zeros((1, LANES), jnp.int32))
    bits = bits_ref[...]
    n_gt = jnp.sum((bits > thr).astype(F32), axis=0, keepdims=True)
    need = float(cap) - n_gt
    eq_ref[...] = (bits == thr).astype(F32)
    tri = tri_ref[...]

    def block_rows(i):
        return pl.ds(pl.multiple_of(i * TOKEN_BLOCK, TOKEN_BLOCK), TOKEN_BLOCK)

    def pass_ties(i, carry):
        rows = block_rows(i)
        eq = eq_ref[rows, :]
        cum = _dot(tri, eq.astype(BF16)) + carry
        gt = bits_ref[rows, :] > thr
        sel = jnp.where(gt | ((eq > 0.0) & (cum <= need)), 1.0, 0.0)
        eq_ref[rows, :] = sel
        return carry + jnp.sum(eq, axis=0, keepdims=True)

    lax.fori_loop(0, nb, pass_ties, jnp.zeros((1, LANES), F32))

    def pass_rank(i, carry):
        rows = block_rows(i)
        sel = eq_ref[rows, :]
        cum = _dot(tri, sel.astype(BF16)) + carry
        rank_ref[0, rows, :] = jnp.where(sel > 0.0, cum - 1.0, -1.0)
        start_ref[0, pl.ds(i, 1), :] = carry.astype(jnp.int32)
        return carry + jnp.sum(sel, axis=0, keepdims=True)

    lax.fori_loop(0, nb, pass_rank, jnp.zeros((1, LANES), F32))


def _route(logits, tri, cap):
    b, l, _ = logits.shape
    nb = l // TOKEN_BLOCK
    return pl.pallas_call(
        functools.partial(_route_kernel, cap=cap),
        grid=(b,),
        in_specs=[
            pl.BlockSpec((1, l, LANES), lambda i: (i, 0, 0)),
            pl.BlockSpec((TOKEN_BLOCK, TOKEN_BLOCK), lambda i: (0, 0)),
        ],
        out_specs=[
            pl.BlockSpec((1, l, LANES), lambda i: (i, 0, 0)),
            pl.BlockSpec((1, l, LANES), lambda i: (i, 0, 0)),
            pl.BlockSpec((1, nb, LANES), lambda i: (i, 0, 0)),
        ],
        out_shape=[
            jax.ShapeDtypeStruct((b, l, LANES), F32),
            jax.ShapeDtypeStruct((b, l, LANES), F32),
            jax.ShapeDtypeStruct((b, nb, LANES), jnp.int32),
        ],
        scratch_shapes=[pltpu.VMEM((l, LANES), jnp.int32), pltpu.VMEM((l, LANES), F32)],
        compiler_params=_params("arbitrary"),
        name="routing",
    )(logits, tri)


def _expert_column(block, e):
    lane = lax.broadcasted_iota(jnp.int32, block.shape, 1)
    return jnp.sum(jnp.where(lane == e, block, 0.0), axis=1, keepdims=True)


def _ffn_kernel(starts_ref, h_ref, rank_ref, w1_ref, w3_ref, w2_ref, y_ref, xg_ref, *, cap, batch_major):
    b = pl.program_id(0 if batch_major else 1)
    e = pl.program_id(1 if batch_major else 0)
    l, d = h_ref.shape[1], h_ref.shape[2]
    nb = l // TOKEN_BLOCK
    rb = min(TOKEN_BLOCK, cap)
    base = (b * N_EXPERTS + e) * nb
    slot = lax.broadcasted_iota(jnp.int32, (TOKEN_BLOCK, rb), 1).astype(F32)

    for i in range(cap // rb):
        lo, hi = i * rb, (i + 1) * rb

        def gather(jb, acc, lo=lo, hi=hi):
            s0 = starts_ref[base + jb]
            s1 = jnp.where(jb + 1 < nb, starts_ref[base + jnp.minimum(jb + 1, nb - 1)], cap)

            def hit(acc):
                rows = pl.ds(pl.multiple_of(jb * TOKEN_BLOCK, TOKEN_BLOCK), TOKEN_BLOCK)
                rcol = _expert_column(rank_ref[0, rows, :], e)
                onehot = (rcol - float(lo) == slot).astype(BF16)
                return acc + lax.dot_general(onehot, h_ref[0, rows, :], (((0,), (0,)), ((), ())),
                                             preferred_element_type=F32)

            return lax.cond((s0 < hi) & (s1 > lo), hit, lambda a: a, acc)

        xg_ref[lo:hi, :] = lax.fori_loop(0, nb, gather, jnp.zeros((rb, d), F32)).astype(BF16)

    xg = xg_ref[...]
    a = _dot(xg, w1_ref[0])
    hid = (a * _sigmoid(a)) * _dot(xg, w3_ref[0])
    y_ref[0, 0] = _dot(hid.astype(BF16), w2_ref[0]).astype(BF16)


def _expert_ffn(starts, h2, rank, w1, w3, w2, cap, batch_major):
    b, l, d = h2.shape
    f = w1.shape[2]
    if batch_major:
        grid = (b, N_EXPERTS)
        be = lambda i, j: (i, j)
    else:
        grid = (N_EXPERTS, b)
        be = lambda i, j: (j, i)
    grid_spec = pltpu.PrefetchScalarGridSpec(
        num_scalar_prefetch=1,
        grid=grid,
        in_specs=[
            pl.BlockSpec((1, l, d), lambda i, j, s: (be(i, j)[0], 0, 0)),
            pl.BlockSpec((1, l, LANES), lambda i, j, s: (be(i, j)[0], 0, 0)),
            pl.BlockSpec((1, d, f), lambda i, j, s: (be(i, j)[1], 0, 0)),
            pl.BlockSpec((1, d, f), lambda i, j, s: (be(i, j)[1], 0, 0)),
            pl.BlockSpec((1, f, d), lambda i, j, s: (be(i, j)[1], 0, 0)),
        ],
        out_specs=pl.BlockSpec((1, 1, cap, d), lambda i, j, s: (be(i, j)[0], be(i, j)[1], 0, 0)),
        scratch_shapes=[pltpu.VMEM((cap, d), BF16)],
    )
    return pl.pallas_call(
        functools.partial(_ffn_kernel, cap=cap, batch_major=batch_major),
        grid_spec=grid_spec,
        out_shape=jax.ShapeDtypeStruct((b, N_EXPERTS, cap, d), BF16),
        compiler_params=_params("arbitrary", "arbitrary"),
        name="expert_ffn",
    )(starts, h2, rank, w1, w3, w2)


def _combine_kernel(starts_ref, y_ref, rank_ref, gate_ref, x_ref, g2_ref, lng_ref, lnb_ref, xo_ref,
                    *, cap, window, alpha):
    b = pl.program_id(0)
    jb = pl.program_id(1)
    nb = pl.num_programs(1)
    rank = rank_ref[0]
    gate = gate_ref[0]
    slot = lax.broadcasted_iota(jnp.int32, (TOKEN_BLOCK, window), 1).astype(F32)

    def expert(e, acc):
        s0 = starts_ref[(b * N_EXPERTS + e) * nb + jb]
        a = jnp.minimum(s0 & ~(BF16_ROWS - 1), cap - window)
        a = pl.multiple_of(a, BF16_ROWS)
        onehot = (_expert_column(rank, e) - a.astype(F32) == slot).astype(BF16)
        return acc + _expert_column(gate, e) * _dot(onehot, y_ref[0, e, pl.ds(a, window), :])

    ml = lax.fori_loop(0, N_EXPERTS, expert, jnp.zeros(x_ref.shape[1:], F32))
    xo_ref[0] = _layer_norm(alpha * x_ref[0] + g2_ref[0] * ml) * lng_ref[...] + lnb_ref[...]


def _combine(starts, y, rank, gate, x, g2, ln_g, ln_b, cap, alpha):
    b, l, d = x.shape
    nb = l // TOKEN_BLOCK
    window = min(cap, 2 * TOKEN_BLOCK)
    grid_spec = pltpu.PrefetchScalarGridSpec(
        num_scalar_prefetch=1,
        grid=(b, nb),
        in_specs=[
            pl.BlockSpec((1, N_EXPERTS, cap, d), lambda i, j, s: (i, 0, 0, 0)),
            pl.BlockSpec((1, TOKEN_BLOCK, LANES), lambda i, j, s: (i, j, 0)),
            pl.BlockSpec((1, TOKEN_BLOCK, LANES), lambda i, j, s: (i, j, 0)),
            pl.BlockSpec((1, TOKEN_BLOCK, d), lambda i, j, s: (i, j, 0)),
            pl.BlockSpec((1, 1, d), lambda i, j, s: (i, 0, 0)),
            pl.BlockSpec((1, d), lambda i, j, s: (0, 0)),
            pl.BlockSpec((1, d), lambda i, j, s: (0, 0)),
        ],
        out_specs=pl.BlockSpec((1, TOKEN_BLOCK, d), lambda i, j, s: (i, j, 0)),
    )
    return pl.pallas_call(
        functools.partial(_combine_kernel, cap=cap, window=window, alpha=alpha),
        grid_spec=grid_spec,
        out_shape=jax.ShapeDtypeStruct((b, l, d), F32),
        compiler_params=_params("arbitrary", "arbitrary"),
        name="moe_combine",
    )(starts, y, rank, gate, x, g2, ln_g, ln_b)


def _group_mean_matrix(n, group):
    idx = jnp.arange(n) // group
    return (idx[:, None] == idx[None, :]).astype(F32) / group


def _rope_tables(n):
    rows = n // GRID_W
    r = jnp.repeat(jnp.arange(rows, dtype=F32), GRID_W)
    col = jnp.tile(jnp.arange(GRID_W, dtype=F32), rows)
    inv = ROPE_THETA ** (-jnp.arange(ROPE_AXIS_FREQS, dtype=F32) / ROPE_AXIS_FREQS)
    ang = jnp.concatenate([r[:, None] * inv, col[:, None] * inv], axis=-1)
    cos, sin = jnp.cos(ang), jnp.sin(ang)
    cos_t = jnp.tile(cos, (1, LANES // (HEAD_DIM // 2)))
    sin_t = jnp.tile(jnp.concatenate([-sin, sin], axis=-1), (1, LANES // HEAD_DIM))
    return cos_t, sin_t


def _layer_weights(l, w_in, pool_w, pool_scale, sgu_g, sgu_w, sgu_b, q_g, k_g, w_out, ln1_g, ln1_b,
                   w_router, w1, w3, w2, ln2_g, ln2_b):
    d = w_in.shape[1]
    q_scale = HEAD_DIM ** -0.5 * math.log2(math.e)
    gain = jnp.concatenate([jnp.tile(q_g[l] * q_scale, N_HEADS), jnp.tile(k_g[l], N_KV_HEADS)])[None, :]
    pool_bd = jax.scipy.linalg.block_diag(*[pool_w[l, g] for g in range(len(POOL_WINDOWS))])
    wr = jnp.pad(w_router[l], ((0, 0), (0, LANES - N_EXPERTS)))
    wr_hi = wr.astype(BF16)
    return dict(
        w_in=w_in[l].astype(BF16),
        gain=gain,
        pool_w=pool_bd.astype(BF16),
        pool_scale=pool_scale[l][None, :],
        gavg256=_group_mean_matrix(SGU_WIDTH, SGU_WIDTH // SGU_HEADS).astype(BF16),
        sgu_g=sgu_g[l].reshape(1, SGU_WIDTH),
        sgu_w=sgu_w[l].astype(BF16),
        sgu_b=jnp.repeat(sgu_b[l].T, SGU_WIDTH // SGU_HEADS, axis=1),
        w_out=w_out[l].astype(BF16),
        ln1_g=ln1_g[l][None, :], ln1_b=ln1_b[l][None, :],
        wr_hi=wr_hi, wr_lo=(wr - wr_hi.astype(F32)).astype(BF16),
        w1=w1[l].astype(BF16), w3=w3[l].astype(BF16), w2=w2[l].astype(BF16),
        ln2_g=ln2_g[l][None, :], ln2_b=ln2_b[l][None, :],
    )


def _moe(x, h2, logits, g2, lw, tri, alpha, batch_major):
    b, l, d = x.shape
    cap = EC_CAPACITY_FACTOR * l // N_EXPERTS
    rank, gate, starts = _route(logits, tri, cap)
    starts = jnp.swapaxes(starts[:, :, :N_EXPERTS], 1, 2).reshape(-1)
    y = _expert_ffn(starts, h2, rank, lw["w1"], lw["w3"], lw["w2"], cap, batch_major)
    return _combine(starts, y, rank, gate, x, g2, lw["ln2_g"], lw["ln2_b"], cap, alpha)


def kernel(x, c, ctx, c_ctx, w_mod, b_mod, w_in, pool_w, pool_scale, sgu_g, sgu_w, sgu_b, q_g, k_g, w_out,
           ln1_g, ln1_b, w_router, w1, w3, w2, ln2_g, ln2_b):
    batch, seq, d = x.shape
    ctx_len = ctx.shape[1]
    depth = w_mod.shape[0]
    alpha = (2 * depth) ** 0.25

    cond = jnp.concatenate([c, c_ctx[None, :], jnp.zeros((COND_ROWS - batch - 1, d), F32)], axis=0)
    mod = _modulation(cond, w_mod, b_mod)

    cos, sin = _rope_tables(seq)
    cos_c = jnp.ones((ctx_len, LANES), F32)
    sin_c = jnp.zeros((ctx_len, LANES), F32)
    gavg128 = _group_mean_matrix(LANES, HEAD_DIM).astype(BF16)
    tri = jnp.tril(jnp.ones((TOKEN_BLOCK, TOKEN_BLOCK), F32)).astype(BF16)

    xc = ctx
    for l in range(depth):
        last = l == depth - 1
        lw = _layer_weights(l, w_in, pool_w, pool_scale, sgu_g, sgu_w, sgu_b, q_g, k_g, w_out, ln1_g, ln1_b,
                            w_router, w1, w3, w2, ln2_g, ln2_b)
        m = mod[l].reshape(COND_ROWS, 6, d)
        sh1, sc1, g1, sh2, sc2, g2 = [m[:batch, i][:, None, :] for i in range(6)]
        csh1, csc1, cg1, csh2, csc2, cg2 = [jnp.broadcast_to(m[batch, i][None, None, :], (batch, 1, d))
                                            for i in range(6)]

        zpu, qt, kl, vtl = _in_projection(x, sh1, sc1, lw["w_in"], lw["gain"], cos, sin, gavg128)
        zpu_c, qt_c, kc, vtc = _in_projection(xc, csh1, csc1, lw["w_in"], lw["gain"], cos_c, sin_c, gavg128)

        attn = _attention(qt, jnp.concatenate([kl, kc], axis=1), jnp.concatenate([vtl, vtc], axis=2))
        if not last:
            attn_c = _attention(qt_c, kc, vtc)
            xc, h2c, logits_c = _mixer_output(zpu_c, attn_c, xc, cg1, csh2, csc2, lw, alpha)
            xc = _moe(xc, h2c, logits_c, cg2, lw, tri, alpha, batch_major=False)

        x, h2, logits = _mixer_output(zpu, attn, x, g1, sh2, sc2, lw, alpha)
        x = _moe(x, h2, logits, g2, lw, tri, alpha, batch_major=True)
    return x
```

```python
import functools
import math

import jax
import jax.numpy as jnp
from jax import lax
from jax.experimental import pallas as pl
from jax.experimental.pallas import tpu as pltpu

F32 = jnp.float32
BF16 = jnp.bfloat16

GRID_W = 64
POOL_WIDTH = 256
POOL_WINDOWS = (2, 4, 8, 16)
POOL_GROUP = 64
POOL_HALO = 8
SGU_WIDTH = 256
SGU_HEADS = 4
SGU_CHUNK = 128
HEAD_DIM = 64
N_HEADS = 8
N_KV_HEADS = 2
KV_GROUP = N_HEADS // N_KV_HEADS
ATTN_WIDTH = N_HEADS * HEAD_DIM
KV_WIDTH = N_KV_HEADS * HEAD_DIM
PU_WIDTH = POOL_WIDTH + 2 * SGU_WIDTH
QK_WIDTH = ATTN_WIDTH + KV_WIDTH
ROPE_THETA = 10000.0
ROPE_AXIS_FREQS = HEAD_DIM // 4
N_EXPERTS = 16
EC_CAPACITY_FACTOR = 2
LN_EPS = 1e-6

LANES = 128
SUBLANES = 8
BF16_ROWS = 16
VMEM_LIMIT_BYTES = 56 * 1024 * 1024

ROW_TILE = 256
KV_TILE = 256
TOKEN_BLOCK = 128
GATHER_UNROLL = 4
COND_ROWS = 16


def _params(*sem):
    return pltpu.CompilerParams(dimension_semantics=sem, vmem_limit_bytes=VMEM_LIMIT_BYTES)


def _split(x):
    hi = x.astype(BF16)
    lo = (x - hi.astype(F32)).astype(BF16)
    return hi, lo


def _dot(a, b):
    return jnp.dot(a, b, preferred_element_type=F32)


def _dot3(a, b_hi, b_lo):
    a_hi, a_lo = _split(a)
    return _dot(a_hi, b_hi) + _dot(a_hi, b_lo) + _dot(a_lo, b_hi)


def _dot_exact_rhs(a, b):
    a_hi, a_lo = _split(a)
    return _dot(a_hi, b) + _dot(a_lo, b)


def _layer_norm(x):
    mu = jnp.mean(x, axis=-1, keepdims=True)
    xc = x - mu
    var = jnp.mean(xc * xc, axis=-1, keepdims=True)
    return xc * lax.rsqrt(var + LN_EPS)


def _sigmoid(x):
    return 1.0 / (1.0 + jnp.exp(-x))


def _gelu_tanh(x):
    c = math.sqrt(2.0 / math.pi)
    return 0.5 * x * (1.0 + jnp.tanh(c * (x + 0.044715 * (x * x * x))))


def _mod_kernel(cond_ref, w_ref, b_ref, o_ref):
    a = cond_ref[...]
    a = a * _sigmoid(a)
    w_hi, w_lo = _split(w_ref[0])
    o_ref[0] = _dot3(a, w_hi, w_lo) + b_ref[0]


def _modulation(cond, w_mod, b_mod):
    depth, d, n = w_mod.shape
    tn = n // 4
    return pl.pallas_call(
        _mod_kernel,
        grid=(depth, n // tn),
        in_specs=[
            pl.BlockSpec((COND_ROWS, d), lambda l, j: (0, 0)),
            pl.BlockSpec((1, d, tn), lambda l, j: (l, 0, j)),
            pl.BlockSpec((1, 1, tn), lambda l, j: (l, 0, j)),
        ],
        out_specs=pl.BlockSpec((1, COND_ROWS, tn), lambda l, j: (l, 0, j)),
        out_shape=jax.ShapeDtypeStruct((depth, COND_ROWS, n), F32),
        compiler_params=_params("arbitrary", "arbitrary"),
        name="modulation",
    )(cond, w_mod, b_mod.reshape(depth, 1, n))


def _in_kernel(x_ref, sh_ref, sc_ref, w_ref, gain_ref, cos_ref, sin_ref, gavg_ref,
               zpu_ref, qt_ref, k_ref, vt_ref):
    x = x_ref[0]
    h = _layer_norm(x) * (1.0 + sc_ref[0]) + sh_ref[0]
    z = _dot(h.astype(BF16), w_ref[...])
    zpu_ref[0] = z[:, :PU_WIDTH]

    gavg = gavg_ref[...]
    cos = cos_ref[...]
    sin = sin_ref[...]
    lane = lax.broadcasted_iota(jnp.int32, cos.shape, 1)
    first_half = (lane & (HEAD_DIM - 1)) < (HEAD_DIM // 2)
    pieces = []
    for c in range(QK_WIDTH // LANES):
        lo = PU_WIDTH + c * LANES
        y = z[:, lo:lo + LANES]
        ms = _dot_exact_rhs(y * y, gavg)
        y = y * lax.rsqrt(ms + LN_EPS) * gain_ref[:, c * LANES:(c + 1) * LANES]
        partner = jnp.where(first_half,
                            pltpu.roll(y, LANES - HEAD_DIM // 2, 1),
                            pltpu.roll(y, HEAD_DIM // 2, 1))
        pieces.append(y * cos + partner * sin)
    q = jnp.concatenate(pieces[:ATTN_WIDTH // LANES], axis=1)
    qt_ref[0] = q.T.astype(BF16)
    k_ref[0] = pieces[-1].astype(BF16)

    v = z[:, PU_WIDTH + QK_WIDTH:]
    vt = v.T
    row = lax.broadcasted_iota(jnp.int32, (HEAD_DIM, vt.shape[1]), 0)
    ones_row = (row == 0).astype(F32)
    for g in range(N_KV_HEADS):
        ext = jnp.concatenate([vt[g * HEAD_DIM:(g + 1) * HEAD_DIM], ones_row], axis=0)
        vt_ref[0, g, 0] = ext.astype(BF16)


def _in_projection(x, sh, sc, w_in, gain, cos, sin, gavg):
    b, l, d = x.shape
    n = w_in.shape[1]
    tm = ROW_TILE
    nt = l // tm
    vec = pl.BlockSpec((1, 1, d), lambda i, j: (i, 0, 0))
    return pl.pallas_call(
        _in_kernel,
        grid=(b, nt),
        in_specs=[
            pl.BlockSpec((1, tm, d), lambda i, j: (i, j, 0)),
            vec, vec,
            pl.BlockSpec((d, n), lambda i, j: (0, 0)),
            pl.BlockSpec((1, QK_WIDTH), lambda i, j: (0, 0)),
            pl.BlockSpec((tm, LANES), lambda i, j: (j, 0)),
            pl.BlockSpec((tm, LANES), lambda i, j: (j, 0)),
            pl.BlockSpec((LANES, LANES), lambda i, j: (0, 0)),
        ],
        out_specs=[
            pl.BlockSpec((1, tm, PU_WIDTH), lambda i, j: (i, j, 0)),
            pl.BlockSpec((1, ATTN_WIDTH, tm), lambda i, j: (i, 0, j)),
            pl.BlockSpec((1, tm, KV_WIDTH), lambda i, j: (i, j, 0)),
            pl.BlockSpec((1, N_KV_HEADS, 1, 2 * HEAD_DIM, tm), lambda i, j: (i, 0, j, 0, 0)),
        ],
        out_shape=[
            jax.ShapeDtypeStruct((b, l, PU_WIDTH), F32),
            jax.ShapeDtypeStruct((b, ATTN_WIDTH, l), BF16),
            jax.ShapeDtypeStruct((b, l, KV_WIDTH), BF16),
            jax.ShapeDtypeStruct((b, N_KV_HEADS, nt, 2 * HEAD_DIM, tm), BF16),
        ],
        compiler_params=_params("arbitrary", "arbitrary"),
        name="in_projection",
    )(x, sh, sc, w_in, gain, cos, sin, gavg)


def _attn_kernel(qt_ref, k_ref, vt_ref, o_ref, qpad_ref, m_ref, acc_ref, p_ref, scale_ref, *, n_kv):
    tq = qt_ref.shape[2]
    zeros = jnp.zeros((HEAD_DIM, tq), BF16)
    for h in range(N_HEADS):
        qh = qt_ref[0, h * HEAD_DIM:(h + 1) * HEAD_DIM, :]
        qpad_ref[h] = jnp.concatenate([qh, zeros] if h // KV_GROUP == 0 else [zeros, qh], axis=0)
    m_ref[...] = jnp.full(m_ref.shape, -1e30, F32)
    acc_ref[...] = jnp.zeros(acc_ref.shape, F32)

    def scores(c):
        start = pl.multiple_of(c * KV_TILE, KV_TILE)
        kc = k_ref[0, pl.ds(start, KV_TILE), :]
        return [_dot(kc, qpad_ref[h]) for h in range(N_HEADS)]

    def softmax(s):
        for h in range(N_HEADS):
            m = m_ref[h]
            m_new = jnp.maximum(m, jnp.max(s[h], axis=0, keepdims=True))
            p_ref[h] = jnp.exp2(s[h] - m_new).astype(BF16)
            scale_ref[h] = jnp.exp2(m - m_new)
            m_ref[h] = m_new

    def values(c):
        for h in range(N_HEADS):
            acc_ref[h] = acc_ref[h] * scale_ref[h] + _dot(vt_ref[0, h // KV_GROUP, c], p_ref[h])

    softmax(scores(0))

    def body(c, carry):
        s = scores(c)
        values(c - 1)
        softmax(s)
        return carry

    lax.fori_loop(1, n_kv, body, 0)
    values(n_kv - 1)
    blocks = []
    for hp in range(N_HEADS // 2):
        pair = []
        for h in (2 * hp, 2 * hp + 1):
            acc = acc_ref[h]
            pair.append(acc[:HEAD_DIM] / acc[HEAD_DIM:HEAD_DIM + 1])
        blocks.append(jnp.concatenate(pair, axis=0).T)
    o_ref[0] = jnp.concatenate(blocks, axis=1).astype(BF16)


def _attention(qt, k, vt):
    b, _, lq = qt.shape
    lk = k.shape[1]
    n_kv = lk // KV_TILE
    tq = ROW_TILE
    return pl.pallas_call(
        functools.partial(_attn_kernel, n_kv=n_kv),
        grid=(b, lq // tq),
        in_specs=[
            pl.BlockSpec((1, ATTN_WIDTH, tq), lambda i, j: (i, 0, j)),
            pl.BlockSpec((1, lk, KV_WIDTH), lambda i, j: (i, 0, 0)),
            pl.BlockSpec((1, N_KV_HEADS, n_kv, 2 * HEAD_DIM, KV_TILE), lambda i, j: (i, 0, 0, 0, 0)),
        ],
        out_specs=pl.BlockSpec((1, tq, ATTN_WIDTH), lambda i, j: (i, j, 0)),
        out_shape=jax.ShapeDtypeStruct((b, lq, ATTN_WIDTH), BF16),
        scratch_shapes=[
            pltpu.VMEM((N_HEADS, 2 * HEAD_DIM, tq), BF16),
            pltpu.VMEM((N_HEADS, 1, tq), F32),
            pltpu.VMEM((N_HEADS, 2 * HEAD_DIM, tq), F32),
            pltpu.VMEM((N_HEADS, KV_TILE, tq), BF16),
            pltpu.VMEM((N_HEADS, 1, tq), F32),
        ],
        compiler_params=_params("arbitrary", "arbitrary"),
        name="attention",
    )(qt, k, vt)


def _mix_kernel(zp_ref, zn_ref, z_ref, attn_ref, x_ref, g1_ref, sh2_ref, sc2_ref,
                poolw_ref, pscale_ref, gavg_ref, sgug_ref, sguw_ref, sgub_ref,
                wout_ref, lng_ref, lnb_ref, wr_hi_ref, wr_lo_ref,
                xo_ref, h2_ref, logit_ref, *, seq_len, alpha):
    j = pl.program_id(1)
    nt = pl.num_programs(1)
    z = z_ref[0]
    tm = z.shape[0]
    lane = lax.broadcasted_iota(jnp.int32, (tm, POOL_WIDTH), 1)
    group = lax.shift_right_logical(lane, POOL_GROUP.bit_length() - 1)

    p = z[:, :POOL_WIDTH]
    prev = zp_ref[0] * (j > 0).astype(F32)
    nxt = zn_ref[0] * (j < nt - 1).astype(F32)
    ext = jnp.concatenate([prev, p, nxt], axis=0)
    n_ext = ext.shape[0]
    s2 = ext + pltpu.roll(ext, 1, 0)
    s4 = pltpu.roll(s2, 1, 0) + pltpu.roll(s2, n_ext - 1, 0)
    s8 = pltpu.roll(s4, 2, 0) + pltpu.roll(s4, n_ext - 2, 0)
    s16 = pltpu.roll(s8, 4, 0) + pltpu.roll(s8, n_ext - 4, 0)
    sums = [s[POOL_HALO:POOL_HALO + tm] for s in (s2, s4, s8, s16)]
    wsum = jnp.where(group == 0, sums[0], jnp.where(group == 1, sums[1], jnp.where(group == 2, sums[2], sums[3])))
    half = jnp.where(group == 0, 1, jnp.where(group == 1, 2, jnp.where(group == 2, 4, 8)))
    t = j * tm + lax.broadcasted_iota(jnp.int32, (tm, POOL_WIDTH), 0)
    cnt = jnp.minimum(t + half, seq_len) - jnp.maximum(t - half, 0)
    pin = (wsum / cnt.astype(F32) - p).astype(BF16)
    pool = _dot(pin, poolw_ref[...]) * pscale_ref[...]

    gavg = gavg_ref[...]
    head = lax.shift_right_logical(lax.broadcasted_iota(jnp.int32, (SGU_CHUNK, SGU_WIDTH), 1),
                                   (SGU_WIDTH // SGU_HEADS).bit_length() - 1)
    sgu_parts = []
    for c in range(tm // SGU_CHUNK):
        rows = slice(c * SGU_CHUNK, (c + 1) * SGU_CHUNK)
        u = _gelu_tanh(z[rows, POOL_WIDTH:POOL_WIDTH + SGU_WIDTH])
        v = _gelu_tanh(z[rows, POOL_WIDTH + SGU_WIDTH:])
        mu = _dot_exact_rhs(v, gavg)
        vc = v - mu
        var = _dot_exact_rhs(vc * vc, gavg)
        vn = (vc * lax.rsqrt(var + LN_EPS) * sgug_ref[...]).astype(BF16)
        mixed = sgub_ref[...]
        for hd in range(SGU_HEADS):
            mixed = mixed + jnp.where(head == hd, _dot(sguw_ref[hd], vn), 0.0)
        sgu_parts.append(u * mixed)
    sgu = jnp.concatenate(sgu_parts, axis=0)

    ps = jnp.concatenate([pool, sgu], axis=1).astype(BF16)
    split = POOL_WIDTH + SGU_WIDTH
    y = _dot(ps, wout_ref[:split, :]) + _dot(attn_ref[0], wout_ref[split:, :])
    xn = _layer_norm(alpha * x_ref[0] + g1_ref[0] * y) * lng_ref[...] + lnb_ref[...]
    xo_ref[0] = xn

    h2 = _layer_norm(xn) * (1.0 + sc2_ref[0]) + sh2_ref[0]
    h2_ref[0] = h2.astype(BF16)
    logit_ref[0] = _dot3(h2, wr_hi_ref[...], wr_lo_ref[...])


def _mixer_output(zpu, attn, x, g1, sh2, sc2, lw, alpha):
    b, l, d = x.shape
    tm = ROW_TILE
    nt = l // tm
    hb = tm // POOL_HALO
    n_halo = l // POOL_HALO
    vec = pl.BlockSpec((1, 1, d), lambda i, j: (i, 0, 0))

    def full(a):
        nd = a.ndim
        return pl.BlockSpec(a.shape, lambda i, j: (0,) * nd)

    weights = [lw["pool_w"], lw["pool_scale"], lw["gavg256"], lw["sgu_g"], lw["sgu_w"], lw["sgu_b"],
               lw["w_out"], lw["ln1_g"], lw["ln1_b"], lw["wr_hi"], lw["wr_lo"]]
    return pl.pallas_call(
        functools.partial(_mix_kernel, seq_len=l, alpha=alpha),
        grid=(b, nt),
        in_specs=[
            pl.BlockSpec((1, POOL_HALO, POOL_WIDTH), lambda i, j: (i, jnp.maximum(j * hb - 1, 0), 0)),
            pl.BlockSpec((1, POOL_HALO, POOL_WIDTH), lambda i, j: (i, jnp.minimum((j + 1) * hb, n_halo - 1), 0)),
            pl.BlockSpec((1, tm, PU_WIDTH), lambda i, j: (i, j, 0)),
            pl.BlockSpec((1, tm, ATTN_WIDTH), lambda i, j: (i, j, 0)),
            pl.BlockSpec((1, tm, d), lambda i, j: (i, j, 0)),
            vec, vec, vec,
        ] + [full(w) for w in weights],
        out_specs=[
            pl.BlockSpec((1, tm, d), lambda i, j: (i, j, 0)),
            pl.BlockSpec((1, tm, d), lambda i, j: (i, j, 0)),
            pl.BlockSpec((1, tm, LANES), lambda i, j: (i, j, 0)),
        ],
        out_shape=[
            jax.ShapeDtypeStruct((b, l, d), F32),
            jax.ShapeDtypeStruct((b, l, d), BF16),
            jax.ShapeDtypeStruct((b, l, LANES), F32),
        ],
        compiler_params=_params("arbitrary", "arbitrary"),
        name="mixer_output",
    )(zpu, zpu, zpu, attn, x, g1, sh2, sc2, *weights)


def _route_kernel(logit_ref, tri_ref, rank_ref, rank_t_ref, gate_ref, start_ref, bits_ref, eq_ref, *, cap):
    l = logit_ref.shape[1]
    nb = l // TOKEN_BLOCK
    lane = lax.broadcasted_iota(jnp.int32, (l, LANES), 1)
    valid = lane < N_EXPERTS
    x = jnp.where(valid, logit_ref[0], -jnp.inf)
    e = jnp.exp(x - jnp.max(x, axis=1, keepdims=True))
    aff = e / jnp.sum(e, axis=1, keepdims=True)
    gate_ref[0] = aff
    bits_ref[...] = pltpu.bitcast(aff, jnp.int32)

    def bisect(i, thr):
        cand = thr | lax.shift_left(jnp.int32(1), 30 - i)
        cnt = jnp.sum((bits_ref[...] >= cand).astype(F32), axis=0, keepdims=True)
        return jnp.where(cnt >= float(cap), cand, thr)

    thr = lax.fori_loop(0, 31, bisect, jnp.zeros((1, LANES), jnp.int32))
    bits = bits_ref[...]
    n_gt = jnp.sum((bits > thr).astype(F32), axis=0, keepdims=True)
    need = float(cap) - n_gt
    eq_ref[...] = (bits == thr).astype(F32)
    tri = tri_ref[...]

    def block_rows(i):
        return pl.ds(pl.multiple_of(i * TOKEN_BLOCK, TOKEN_BLOCK), TOKEN_BLOCK)

    def pass_ties(i, carry):
        rows = block_rows(i)
        eq = eq_ref[rows, :]
        cum = _dot(tri, eq.astype(BF16)) + carry
        gt = bits_ref[rows, :] > thr
        sel = jnp.where(gt | ((eq > 0.0) & (cum <= need)), 1.0, 0.0)
        eq_ref[rows, :] = sel
        return carry + jnp.sum(eq, axis=0, keepdims=True)

    lax.fori_loop(0, nb, pass_ties, jnp.zeros((1, LANES), F32))

    def pass_rank(i, carry):
        rows = block_rows(i)
        sel = eq_ref[rows, :]
        cum = _dot(tri, sel.astype(BF16)) + carry
        rank = jnp.where(sel > 0.0, cum - 1.0, -1.0)
        rank_ref[0, rows, :] = rank
        rank_t_ref[0, i] = rank.T[:N_EXPERTS]
        start_ref[0, pl.ds(i, 1), :] = carry.astype(jnp.int32)
        return carry + jnp.sum(sel, axis=0, keepdims=True)

    lax.fori_loop(0, nb, pass_rank, jnp.zeros((1, LANES), F32))


def _route(logits, tri, cap):
    b, l, _ = logits.shape
    nb = l // TOKEN_BLOCK
    return pl.pallas_call(
        functools.partial(_route_kernel, cap=cap),
        grid=(b,),
        in_specs=[
            pl.BlockSpec((1, l, LANES), lambda i: (i, 0, 0)),
            pl.BlockSpec((TOKEN_BLOCK, TOKEN_BLOCK), lambda i: (0, 0)),
        ],
        out_specs=[
            pl.BlockSpec((1, l, LANES), lambda i: (i, 0, 0)),
            pl.BlockSpec((1, nb, N_EXPERTS, TOKEN_BLOCK), lambda i: (i, 0, 0, 0)),
            pl.BlockSpec((1, l, LANES), lambda i: (i, 0, 0)),
            pl.BlockSpec((1, nb, LANES), lambda i: (i, 0, 0)),
        ],
        out_shape=[
            jax.ShapeDtypeStruct((b, l, LANES), F32),
            jax.ShapeDtypeStruct((b, nb, N_EXPERTS, TOKEN_BLOCK), F32),
            jax.ShapeDtypeStruct((b, l, LANES), F32),
            jax.ShapeDtypeStruct((b, nb, LANES), jnp.int32),
        ],
        scratch_shapes=[pltpu.VMEM((l, LANES), jnp.int32), pltpu.VMEM((l, LANES), F32)],
        compiler_params=_params("arbitrary"),
        name="routing",
    )(logits, tri)


def _expert_column(block, e):
    lane = lax.broadcasted_iota(jnp.int32, block.shape, 1)
    return jnp.sum(jnp.where(lane == e, block, 0.0), axis=1, keepdims=True)


def _ffn_kernel(starts_ref, h_ref, rank_t_ref, w1_ref, w3_ref, w2_ref, y_ref, xg_ref, *, cap, window, batch_major):
    b = pl.program_id(0 if batch_major else 1)
    e = pl.program_id(1 if batch_major else 0)
    nb = rank_t_ref.shape[1]
    base = (b * N_EXPERTS + e) * nb
    slot = lax.broadcasted_iota(jnp.int32, (window, TOKEN_BLOCK), 0)
    xg_ref[...] = jnp.zeros(xg_ref.shape, F32)

    def gather(jb, carry):
        s0 = starts_ref[base + jb]
        a = pl.multiple_of(jnp.minimum(s0 & ~(BF16_ROWS - 1), cap - window), BF16_ROWS)
        rank = rank_t_ref[0, jb, pl.ds(e, 1), :]
        onehot = ((slot + a).astype(F32) == rank).astype(BF16)
        rows = pl.ds(pl.multiple_of(jb * TOKEN_BLOCK, TOKEN_BLOCK), TOKEN_BLOCK)
        xg_ref[pl.ds(a, window), :] += _dot(onehot, h_ref[0, rows, :])
        return carry

    lax.fori_loop(0, nb, gather, 0, unroll=min(nb, GATHER_UNROLL))

    xg = xg_ref[...].astype(BF16)
    a = _dot(xg, w1_ref[0])
    hid = (a * _sigmoid(a)) * _dot(xg, w3_ref[0])
    y_ref[0, 0] = _dot(hid.astype(BF16), w2_ref[0]).astype(BF16)


def _expert_ffn(starts, h2, rank_t, w1, w3, w2, cap, batch_major):
    b, l, d = h2.shape
    f = w1.shape[2]
    nb = l // TOKEN_BLOCK
    window = min(cap, TOKEN_BLOCK + BF16_ROWS)
    if batch_major:
        grid = (b, N_EXPERTS)
        be = lambda i, j: (i, j)
    else:
        grid = (N_EXPERTS, b)
        be = lambda i, j: (j, i)
    grid_spec = pltpu.PrefetchScalarGridSpec(
        num_scalar_prefetch=1,
        grid=grid,
        in_specs=[
            pl.BlockSpec((1, l, d), lambda i, j, s: (be(i, j)[0], 0, 0)),
            pl.BlockSpec((1, nb, N_EXPERTS, TOKEN_BLOCK), lambda i, j, s: (be(i, j)[0], 0, 0, 0)),
            pl.BlockSpec((1, d, f), lambda i, j, s: (be(i, j)[1], 0, 0)),
            pl.BlockSpec((1, d, f), lambda i, j, s: (be(i, j)[1], 0, 0)),
            pl.BlockSpec((1, f, d), lambda i, j, s: (be(i, j)[1], 0, 0)),
        ],
        out_specs=pl.BlockSpec((1, 1, cap, d), lambda i, j, s: (be(i, j)[0], be(i, j)[1], 0, 0)),
        scratch_shapes=[pltpu.VMEM((cap, d), F32)],
    )
    return pl.pallas_call(
        functools.partial(_ffn_kernel, cap=cap, window=window, batch_major=batch_major),
        grid_spec=grid_spec,
        out_shape=jax.ShapeDtypeStruct((b, N_EXPERTS, cap, d), BF16),
        compiler_params=_params("arbitrary", "arbitrary"),
        name="expert_ffn",
    )(starts, h2, rank_t, w1, w3, w2)


def _combine_kernel(starts_ref, y_ref, rank_ref, gate_ref, x_ref, g2_ref, lng_ref, lnb_ref, xo_ref,
                    *, cap, window, alpha):
    b = pl.program_id(0)
    jb = pl.program_id(1)
    nb = pl.num_programs(1)
    rank = rank_ref[0]
    gate = gate_ref[0]
    slot = lax.broadcasted_iota(jnp.int32, (TOKEN_BLOCK, window), 1).astype(F32)

    ml = jnp.zeros(x_ref.shape[1:], F32)
    for e in range(N_EXPERTS):
        s0 = starts_ref[(b * N_EXPERTS + e) * nb + jb]
        a = pl.multiple_of(jnp.minimum(s0 & ~(BF16_ROWS - 1), cap - window), BF16_ROWS)
        onehot = (_expert_column(rank, e) - a.astype(F32) == slot).astype(BF16)
        ml = ml + _expert_column(gate, e) * _dot(onehot, y_ref[0, e, pl.ds(a, window), :])
    xo_ref[0] = _layer_norm(alpha * x_ref[0] + g2_ref[0] * ml) * lng_ref[...] + lnb_ref[...]


def _combine(starts, y, rank, gate, x, g2, ln_g, ln_b, cap, alpha):
    b, l, d = x.shape
    nb = l // TOKEN_BLOCK
    window = min(cap, TOKEN_BLOCK + BF16_ROWS)
    grid_spec = pltpu.PrefetchScalarGridSpec(
        num_scalar_prefetch=1,
        grid=(b, nb),
        in_specs=[
            pl.BlockSpec((1, N_EXPERTS, cap, d), lambda i, j, s: (i, 0, 0, 0)),
            pl.BlockSpec((1, TOKEN_BLOCK, LANES), lambda i, j, s: (i, j, 0)),
            pl.BlockSpec((1, TOKEN_BLOCK, LANES), lambda i, j, s: (i, j, 0)),
            pl.BlockSpec((1, TOKEN_BLOCK, d), lambda i, j, s: (i, j, 0)),
            pl.BlockSpec((1, 1, d), lambda i, j, s: (i, 0, 0)),
            pl.BlockSpec((1, d), lambda i, j, s: (0, 0)),
            pl.BlockSpec((1, d), lambda i, j, s: (0, 0)),
        ],
        out_specs=pl.BlockSpec((1, TOKEN_BLOCK, d), lambda i, j, s: (i, j, 0)),
    )
    return pl.pallas_call(
        functools.partial(_combine_kernel, cap=cap, window=window, alpha=alpha),
        grid_spec=grid_spec,
        out_shape=jax.ShapeDtypeStruct((b, l, d), F32),
        compiler_params=_params("arbitrary", "arbitrary"),
        name="moe_combine",
    )(starts, y, rank, gate, x, g2, ln_g, ln_b)


def _group_mean_matrix(n, group):
    idx = jnp.arange(n) // group
    return (idx[:, None] == idx[None, :]).astype(F32) / group


def _rope_tables(n):
    rows = n // GRID_W
    r = jnp.repeat(jnp.arange(rows, dtype=F32), GRID_W)
    col = jnp.tile(jnp.arange(GRID_W, dtype=F32), rows)
    inv = ROPE_THETA ** (-jnp.arange(ROPE_AXIS_FREQS, dtype=F32) / ROPE_AXIS_FREQS)
    ang = jnp.concatenate([r[:, None] * inv, col[:, None] * inv], axis=-1)
    cos, sin = jnp.cos(ang), jnp.sin(ang)
    cos_t = jnp.tile(cos, (1, LANES // (HEAD_DIM // 2)))
    sin_t = jnp.tile(jnp.concatenate([-sin, sin], axis=-1), (1, LANES // HEAD_DIM))
    return cos_t, sin_t


def _layer_weights(l, w_in, pool_w, pool_scale, sgu_g, sgu_w, sgu_b, q_g, k_g, w_out, ln1_g, ln1_b,
                   w_router, w1, w3, w2, ln2_g, ln2_b):
    d = w_in.shape[1]
    q_scale = HEAD_DIM ** -0.5 * math.log2(math.e)
    gain = jnp.concatenate([jnp.tile(q_g[l] * q_scale, N_HEADS), jnp.tile(k_g[l], N_KV_HEADS)])[None, :]
    pool_bd = jax.scipy.linalg.block_diag(*[pool_w[l, g] for g in range(len(POOL_WINDOWS))])
    wr = jnp.pad(w_router[l], ((0, 0), (0, LANES - N_EXPERTS)))
    wr_hi = wr.astype(BF16)
    return dict(
        w_in=w_in[l].astype(BF16),
        gain=gain,
        pool_w=pool_bd.astype(BF16),
        pool_scale=pool_scale[l][None, :],
        gavg256=_group_mean_matrix(SGU_WIDTH, SGU_WIDTH // SGU_HEADS).astype(BF16),
        sgu_g=sgu_g[l].reshape(1, SGU_WIDTH),
        sgu_w=sgu_w[l].astype(BF16),
        sgu_b=jnp.repeat(sgu_b[l].T, SGU_WIDTH // SGU_HEADS, axis=1),
        w_out=w_out[l].astype(BF16),
        ln1_g=ln1_g[l][None, :], ln1_b=ln1_b[l][None, :],
        wr_hi=wr_hi, wr_lo=(wr - wr_hi.astype(F32)).astype(BF16),
        w1=w1[l].astype(BF16), w3=w3[l].astype(BF16), w2=w2[l].astype(BF16),
        ln2_g=ln2_g[l][None, :], ln2_b=ln2_b[l][None, :],
    )


def _moe(x, h2, logits, g2, lw, tri, alpha, batch_major):
    b, l, d = x.shape
    cap = EC_CAPACITY_FACTOR * l // N_EXPERTS
    rank, rank_t, gate, starts = _route(logits, tri, cap)
    starts = jnp.swapaxes(starts[:, :, :N_EXPERTS], 1, 2).reshape(-1)
    y = _expert_ffn(starts, h2, rank_t, lw["w1"], lw["w3"], lw["w2"], cap, batch_major)
    return _combine(starts, y, rank, gate, x, g2, lw["ln2_g"], lw["ln2_b"], cap, alpha)


def kernel(x, c, ctx, c_ctx, w_mod, b_mod, w_in, pool_w, pool_scale, sgu_g, sgu_w, sgu_b, q_g, k_g, w_out,
           ln1_g, ln1_b, w_router, w1, w3, w2, ln2_g, ln2_b):
    batch, seq, d = x.shape
    ctx_len = ctx.shape[1]
    depth = w_mod.shape[0]
    alpha = (2 * depth) ** 0.25

    cond = jnp.concatenate([c, c_ctx[None, :], jnp.zeros((COND_ROWS - batch - 1, d), F32)], axis=0)
    mod = _modulation(cond, w_mod, b_mod)

    cos, sin = _rope_tables(seq)
    cos_c = jnp.ones((ctx_len, LANES), F32)
    sin_c = jnp.zeros((ctx_len, LANES), F32)
    gavg128 = _group_mean_matrix(LANES, HEAD_DIM).astype(BF16)
    tri = jnp.tril(jnp.ones((TOKEN_BLOCK, TOKEN_BLOCK), F32)).astype(BF16)

    xc = ctx
    for l in range(depth):
        last = l == depth - 1
        lw = _layer_weights(l, w_in, pool_w, pool_scale, sgu_g, sgu_w, sgu_b, q_g, k_g, w_out, ln1_g, ln1_b,
                            w_router, w1, w3, w2, ln2_g, ln2_b)
        m = mod[l].reshape(COND_ROWS, 6, d)
        sh1, sc1, g1, sh2, sc2, g2 = [m[:batch, i][:, None, :] for i in range(6)]
        csh1, csc1, cg1, csh2, csc2, cg2 = [jnp.broadcast_to(m[batch, i][None, None, :], (batch, 1, d))
                                            for i in range(6)]

        zpu, qt, kl, vtl = _in_projection(x, sh1, sc1, lw["w_in"], lw["gain"], cos, sin, gavg128)
        zpu_c, qt_c, kc, vtc = _in_projection(xc, csh1, csc1, lw["w_in"], lw["gain"], cos_c, sin_c, gavg128)

        attn = _attention(qt, jnp.concatenate([kl, kc], axis=1), jnp.concatenate([vtl, vtc], axis=2))
        if not last:
            attn_c = _attention(qt_c, kc, vtc)
            xc, h2c, logits_c = _mixer_output(zpu_c, attn_c, xc, cg1, csh2, csc2, lw, alpha)
            xc = _moe(xc, h2c, logits_c, cg2, lw, tri, alpha, batch_major=False)

        x, h2, logits = _mixer_output(zpu, attn, x, g1, sh2, sc2, lw, alpha)
        x = _moe(x, h2, logits, g2, lw, tri, alpha, batch_major=True)
    return x
```

```python
import functools
import math

import jax
import jax.numpy as jnp
from jax import lax
from jax.experimental import pallas as pl
from jax.experimental.pallas import tpu as pltpu

F32 = jnp.float32
BF16 = jnp.bfloat16

GRID_W = 64
POOL_WIDTH = 256
POOL_WINDOWS = (2, 4, 8, 16)
POOL_GROUP = 64
POOL_HALO = 8
SGU_WIDTH = 256
SGU_HEADS = 4
SGU_CHUNK = 128
HEAD_DIM = 64
N_HEADS = 8
N_KV_HEADS = 2
KV_GROUP = N_HEADS // N_KV_HEADS
ATTN_WIDTH = N_HEADS * HEAD_DIM
KV_WIDTH = N_KV_HEADS * HEAD_DIM
PU_WIDTH = POOL_WIDTH + 2 * SGU_WIDTH
QK_WIDTH = ATTN_WIDTH + KV_WIDTH
ROPE_THETA = 10000.0
ROPE_AXIS_FREQS = HEAD_DIM // 4
N_EXPERTS = 16
EC_CAPACITY_FACTOR = 2
LN_EPS = 1e-6

LANES = 128
SUBLANES = 8
BF16_ROWS = 16
VMEM_LIMIT_BYTES = 56 * 1024 * 1024

ROW_TILE = 256
KV_TILE = 256
TOKEN_BLOCK = 128
GATHER_UNROLL = 4
SMALL_WINDOW = 48
COND_ROWS = 16


def _params(*sem):
    return pltpu.CompilerParams(dimension_semantics=sem, vmem_limit_bytes=VMEM_LIMIT_BYTES)


def _split(x):
    hi = x.astype(BF16)
    lo = (x - hi.astype(F32)).astype(BF16)
    return hi, lo


def _dot(a, b):
    return jnp.dot(a, b, preferred_element_type=F32)


def _dot3(a, b_hi, b_lo):
    a_hi, a_lo = _split(a)
    return _dot(a_hi, b_hi) + _dot(a_hi, b_lo) + _dot(a_lo, b_hi)


def _dot_exact_rhs(a, b):
    a_hi, a_lo = _split(a)
    return _dot(a_hi, b) + _dot(a_lo, b)


def _layer_norm(x):
    mu = jnp.mean(x, axis=-1, keepdims=True)
    xc = x - mu
    var = jnp.mean(xc * xc, axis=-1, keepdims=True)
    return xc * lax.rsqrt(var + LN_EPS)


def _sigmoid(x):
    return 1.0 / (1.0 + jnp.exp(-x))


def _gelu_tanh(x):
    c = math.sqrt(2.0 / math.pi)
    return 0.5 * x * (1.0 + jnp.tanh(c * (x + 0.044715 * (x * x * x))))


def _mod_kernel(cond_ref, w_ref, b_ref, o_ref):
    a = cond_ref[...]
    a = a * _sigmoid(a)
    w_hi, w_lo = _split(w_ref[0])
    o_ref[0] = _dot3(a, w_hi, w_lo) + b_ref[0]


def _modulation(cond, w_mod, b_mod):
    depth, d, n = w_mod.shape
    tn = n // 4
    return pl.pallas_call(
        _mod_kernel,
        grid=(depth, n // tn),
        in_specs=[
            pl.BlockSpec((COND_ROWS, d), lambda l, j: (0, 0)),
            pl.BlockSpec((1, d, tn), lambda l, j: (l, 0, j)),
            pl.BlockSpec((1, 1, tn), lambda l, j: (l, 0, j)),
        ],
        out_specs=pl.BlockSpec((1, COND_ROWS, tn), lambda l, j: (l, 0, j)),
        out_shape=jax.ShapeDtypeStruct((depth, COND_ROWS, n), F32),
        compiler_params=_params("arbitrary", "arbitrary"),
        name="modulation",
    )(cond, w_mod, b_mod.reshape(depth, 1, n))


def _in_kernel(x_ref, sh_ref, sc_ref, w_ref, gain_ref, cos_ref, sin_ref, gavg_ref,
               zpu_ref, qt_ref, k_ref, vt_ref):
    x = x_ref[0]
    h = _layer_norm(x) * (1.0 + sc_ref[0]) + sh_ref[0]
    z = _dot(h.astype(BF16), w_ref[...])
    zpu_ref[0] = z[:, :PU_WIDTH]

    gavg = gavg_ref[...]
    cos = cos_ref[...]
    sin = sin_ref[...]
    lane = lax.broadcasted_iota(jnp.int32, cos.shape, 1)
    first_half = (lane & (HEAD_DIM - 1)) < (HEAD_DIM // 2)
    pieces = []
    for c in range(QK_WIDTH // LANES):
        lo = PU_WIDTH + c * LANES
        y = z[:, lo:lo + LANES]
        ms = _dot_exact_rhs(y * y, gavg)
        y = y * lax.rsqrt(ms + LN_EPS) * gain_ref[:, c * LANES:(c + 1) * LANES]
        partner = jnp.where(first_half,
                            pltpu.roll(y, LANES - HEAD_DIM // 2, 1),
                            pltpu.roll(y, HEAD_DIM // 2, 1))
        pieces.append(y * cos + partner * sin)
    q = jnp.concatenate(pieces[:ATTN_WIDTH // LANES], axis=1)
    qt_ref[0] = q.T.astype(BF16)
    k_ref[0] = pieces[-1].astype(BF16)

    v = z[:, PU_WIDTH + QK_WIDTH:]
    vt = v.T
    row = lax.broadcasted_iota(jnp.int32, (HEAD_DIM, vt.shape[1]), 0)
    ones_row = (row == 0).astype(F32)
    for g in range(N_KV_HEADS):
        ext = jnp.concatenate([vt[g * HEAD_DIM:(g + 1) * HEAD_DIM], ones_row], axis=0)
        vt_ref[0, g, 0] = ext.astype(BF16)


def _in_projection(x, sh, sc, w_in, gain, cos, sin, gavg):
    b, l, d = x.shape
    n = w_in.shape[1]
    tm = ROW_TILE
    nt = l // tm
    vec = pl.BlockSpec((1, 1, d), lambda i, j: (i, 0, 0))
    return pl.pallas_call(
        _in_kernel,
        grid=(b, nt),
        in_specs=[
            pl.BlockSpec((1, tm, d), lambda i, j: (i, j, 0)),
            vec, vec,
            pl.BlockSpec((d, n), lambda i, j: (0, 0)),
            pl.BlockSpec((1, QK_WIDTH), lambda i, j: (0, 0)),
            pl.BlockSpec((tm, LANES), lambda i, j: (j, 0)),
            pl.BlockSpec((tm, LANES), lambda i, j: (j, 0)),
            pl.BlockSpec((LANES, LANES), lambda i, j: (0, 0)),
        ],
        out_specs=[
            pl.BlockSpec((1, tm, PU_WIDTH), lambda i, j: (i, j, 0)),
            pl.BlockSpec((1, ATTN_WIDTH, tm), lambda i, j: (i, 0, j)),
            pl.BlockSpec((1, tm, KV_WIDTH), lambda i, j: (i, j, 0)),
            pl.BlockSpec((1, N_KV_HEADS, 1, 2 * HEAD_DIM, tm), lambda i, j: (i, 0, j, 0, 0)),
        ],
        out_shape=[
            jax.ShapeDtypeStruct((b, l, PU_WIDTH), F32),
            jax.ShapeDtypeStruct((b, ATTN_WIDTH, l), BF16),
            jax.ShapeDtypeStruct((b, l, KV_WIDTH), BF16),
            jax.ShapeDtypeStruct((b, N_KV_HEADS, nt, 2 * HEAD_DIM, tm), BF16),
        ],
        compiler_params=_params("arbitrary", "arbitrary"),
        name="in_projection",
    )(x, sh, sc, w_in, gain, cos, sin, gavg)


def _attn_kernel(qt_ref, k_ref, vt_ref, o_ref, qpad_ref, m_ref, acc_ref, p_ref, scale_ref, *, n_kv):
    tq = qt_ref.shape[2]
    zeros = jnp.zeros((HEAD_DIM, tq), BF16)
    for h in range(N_HEADS):
        qh = qt_ref[0, h * HEAD_DIM:(h + 1) * HEAD_DIM, :]
        qpad_ref[h] = jnp.concatenate([qh, zeros] if h // KV_GROUP == 0 else [zeros, qh], axis=0)
    m_ref[...] = jnp.full(m_ref.shape, -1e30, F32)
    acc_ref[...] = jnp.zeros(acc_ref.shape, F32)

    def scores(c):
        start = pl.multiple_of(c * KV_TILE, KV_TILE)
        kc = k_ref[0, pl.ds(start, KV_TILE), :]
        return [_dot(kc, qpad_ref[h]) for h in range(N_HEADS)]

    def softmax(s):
        for h in range(N_HEADS):
            m = m_ref[h]
            m_new = jnp.maximum(m, jnp.max(s[h], axis=0, keepdims=True))
            p_ref[h] = jnp.exp2(s[h] - m_new).astype(BF16)
            scale_ref[h] = jnp.exp2(m - m_new)
            m_ref[h] = m_new

    def values(c):
        for h in range(N_HEADS):
            acc_ref[h] = acc_ref[h] * scale_ref[h] + _dot(vt_ref[0, h // KV_GROUP, c], p_ref[h])

    softmax(scores(0))

    def body(c, carry):
        s = scores(c)
        values(c - 1)
        softmax(s)
        return carry

    lax.fori_loop(1, n_kv, body, 0)
    values(n_kv - 1)
    blocks = []
    for hp in range(N_HEADS // 2):
        pair = []
        for h in (2 * hp, 2 * hp + 1):
            acc = acc_ref[h]
            pair.append(acc[:HEAD_DIM] / acc[HEAD_DIM:HEAD_DIM + 1])
        blocks.append(jnp.concatenate(pair, axis=0).T)
    o_ref[0] = jnp.concatenate(blocks, axis=1).astype(BF16)


def _attention(qt, k, vt):
    b, _, lq = qt.shape
    lk = k.shape[1]
    n_kv = lk // KV_TILE
    tq = ROW_TILE
    return pl.pallas_call(
        functools.partial(_attn_kernel, n_kv=n_kv),
        grid=(b, lq // tq),
        in_specs=[
            pl.BlockSpec((1, ATTN_WIDTH, tq), lambda i, j: (i, 0, j)),
            pl.BlockSpec((1, lk, KV_WIDTH), lambda i, j: (i, 0, 0)),
            pl.BlockSpec((1, N_KV_HEADS, n_kv, 2 * HEAD_DIM, KV_TILE), lambda i, j: (i, 0, 0, 0, 0)),
        ],
        out_specs=pl.BlockSpec((1, tq, ATTN_WIDTH), lambda i, j: (i, j, 0)),
        out_shape=jax.ShapeDtypeStruct((b, lq, ATTN_WIDTH), BF16),
        scratch_shapes=[
            pltpu.VMEM((N_HEADS, 2 * HEAD_DIM, tq), BF16),
            pltpu.VMEM((N_HEADS, 1, tq), F32),
            pltpu.VMEM((N_HEADS, 2 * HEAD_DIM, tq), F32),
            pltpu.VMEM((N_HEADS, KV_TILE, tq), BF16),
            pltpu.VMEM((N_HEADS, 1, tq), F32),
        ],
        compiler_params=_params("arbitrary", "arbitrary"),
        name="attention",
    )(qt, k, vt)


def _mix_kernel(zp_ref, zn_ref, z_ref, attn_ref, x_ref, g1_ref, sh2_ref, sc2_ref,
                poolw_ref, pscale_ref, gavg_ref, sgug_ref, sguw_ref, sgub_ref,
                wout_ref, lng_ref, lnb_ref, wr_hi_ref, wr_lo_ref,
                xo_ref, h2_ref, logit_ref, *, seq_len, alpha):
    j = pl.program_id(1)
    nt = pl.num_programs(1)
    z = z_ref[0]
    tm = z.shape[0]
    lane = lax.broadcasted_iota(jnp.int32, (tm, POOL_WIDTH), 1)
    group = lax.shift_right_logical(lane, POOL_GROUP.bit_length() - 1)

    p = z[:, :POOL_WIDTH]
    prev = zp_ref[0] * (j > 0).astype(F32)
    nxt = zn_ref[0] * (j < nt - 1).astype(F32)
    ext = jnp.concatenate([prev, p, nxt], axis=0)
    n_ext = ext.shape[0]
    s2 = ext + pltpu.roll(ext, 1, 0)
    s4 = pltpu.roll(s2, 1, 0) + pltpu.roll(s2, n_ext - 1, 0)
    s8 = pltpu.roll(s4, 2, 0) + pltpu.roll(s4, n_ext - 2, 0)
    s16 = pltpu.roll(s8, 4, 0) + pltpu.roll(s8, n_ext - 4, 0)
    sums = [s[POOL_HALO:POOL_HALO + tm] for s in (s2, s4, s8, s16)]
    wsum = jnp.where(group == 0, sums[0], jnp.where(group == 1, sums[1], jnp.where(group == 2, sums[2], sums[3])))
    half = jnp.where(group == 0, 1, jnp.where(group == 1, 2, jnp.where(group == 2, 4, 8)))
    t = j * tm + lax.broadcasted_iota(jnp.int32, (tm, POOL_WIDTH), 0)
    cnt = jnp.minimum(t + half, seq_len) - jnp.maximum(t - half, 0)
    pin = (wsum / cnt.astype(F32) - p).astype(BF16)
    pool = _dot(pin, poolw_ref[...]) * pscale_ref[...]

    gavg = gavg_ref[...]
    head = lax.shift_right_logical(lax.broadcasted_iota(jnp.int32, (SGU_CHUNK, SGU_WIDTH), 1),
                                   (SGU_WIDTH // SGU_HEADS).bit_length() - 1)
    sgu_parts = []
    for c in range(tm // SGU_CHUNK):
        rows = slice(c * SGU_CHUNK, (c + 1) * SGU_CHUNK)
        u = _gelu_tanh(z[rows, POOL_WIDTH:POOL_WIDTH + SGU_WIDTH])
        v = _gelu_tanh(z[rows, POOL_WIDTH + SGU_WIDTH:])
        mu = _dot_exact_rhs(v, gavg)
        vc = v - mu
        var = _dot_exact_rhs(vc * vc, gavg)
        vn = (vc * lax.rsqrt(var + LN_EPS) * sgug_ref[...]).astype(BF16)
        mixed = sgub_ref[...]
        for hd in range(SGU_HEADS):
            mixed = mixed + jnp.where(head == hd, _dot(sguw_ref[hd], vn), 0.0)
        sgu_parts.append(u * mixed)
    sgu = jnp.concatenate(sgu_parts, axis=0)

    ps = jnp.concatenate([pool, sgu], axis=1).astype(BF16)
    split = POOL_WIDTH + SGU_WIDTH
    y = _dot(ps, wout_ref[:split, :]) + _dot(attn_ref[0], wout_ref[split:, :])
    xn = _layer_norm(alpha * x_ref[0] + g1_ref[0] * y) * lng_ref[...] + lnb_ref[...]
    xo_ref[0] = xn

    h2 = _layer_norm(xn) * (1.0 + sc2_ref[0]) + sh2_ref[0]
    h2_ref[0] = h2.astype(BF16)
    logit_ref[0] = _dot3(h2, wr_hi_ref[...], wr_lo_ref[...])


def _mixer_output(zpu, attn, x, g1, sh2, sc2, lw, alpha):
    b, l, d = x.shape
    tm = ROW_TILE
    nt = l // tm
    hb = tm // POOL_HALO
    n_halo = l // POOL_HALO
    vec = pl.BlockSpec((1, 1, d), lambda i, j: (i, 0, 0))

    def full(a):
        nd = a.ndim
        return pl.BlockSpec(a.shape, lambda i, j: (0,) * nd)

    weights = [lw["pool_w"], lw["pool_scale"], lw["gavg256"], lw["sgu_g"], lw["sgu_w"], lw["sgu_b"],
               lw["w_out"], lw["ln1_g"], lw["ln1_b"], lw["wr_hi"], lw["wr_lo"]]
    return pl.pallas_call(
        functools.partial(_mix_kernel, seq_len=l, alpha=alpha),
        grid=(b, nt),
        in_specs=[
            pl.BlockSpec((1, POOL_HALO, POOL_WIDTH), lambda i, j: (i, jnp.maximum(j * hb - 1, 0), 0)),
            pl.BlockSpec((1, POOL_HALO, POOL_WIDTH), lambda i, j: (i, jnp.minimum((j + 1) * hb, n_halo - 1), 0)),
            pl.BlockSpec((1, tm, PU_WIDTH), lambda i, j: (i, j, 0)),
            pl.BlockSpec((1, tm, ATTN_WIDTH), lambda i, j: (i, j, 0)),
            pl.BlockSpec((1, tm, d), lambda i, j: (i, j, 0)),
            vec, vec, vec,
        ] + [full(w) for w in weights],
        out_specs=[
            pl.BlockSpec((1, tm, d), lambda i, j: (i, j, 0)),
            pl.BlockSpec((1, tm, d), lambda i, j: (i, j, 0)),
            pl.BlockSpec((1, tm, LANES), lambda i, j: (i, j, 0)),
        ],
        out_shape=[
            jax.ShapeDtypeStruct((b, l, d), F32),
            jax.ShapeDtypeStruct((b, l, d), BF16),
            jax.ShapeDtypeStruct((b, l, LANES), F32),
        ],
        compiler_params=_params("arbitrary", "arbitrary"),
        name="mixer_output",
    )(zpu, zpu, zpu, attn, x, g1, sh2, sc2, *weights)


def _route_kernel(logit_ref, tri_ref, rank_ref, rank_t_ref, gate_ref, gate_t_ref, start_ref, bits_ref, eq_ref, *, cap):
    l = logit_ref.shape[1]
    nb = l // TOKEN_BLOCK
    lane = lax.broadcasted_iota(jnp.int32, (l, LANES), 1)
    valid = lane < N_EXPERTS
    x = jnp.where(valid, logit_ref[0], -jnp.inf)
    e = jnp.exp(x - jnp.max(x, axis=1, keepdims=True))
    aff = e / jnp.sum(e, axis=1, keepdims=True)
    gate_ref[0] = aff
    bits_ref[...] = pltpu.bitcast(aff, jnp.int32)

    def bisect(i, thr):
        cand = thr | lax.shift_left(jnp.int32(1), 30 - i)
        cnt = jnp.sum((bits_ref[...] >= cand).astype(F32), axis=0, keepdims=True)
        return jnp.where(cnt >= float(cap), cand, thr)

    thr = lax.fori_loop(0, 31, bisect, jnp.zeros((1, LANES), jnp.int32))
    bits = bits_ref[...]
    n_gt = jnp.sum((bits > thr).astype(F32), axis=0, keepdims=True)
    need = float(cap) - n_gt
    eq_ref[...] = (bits == thr).astype(F32)
    tri = tri_ref[...]

    def block_rows(i):
        return pl.ds(pl.multiple_of(i * TOKEN_BLOCK, TOKEN_BLOCK), TOKEN_BLOCK)

    def pass_ties(i, carry):
        rows = block_rows(i)
        eq = eq_ref[rows, :]
        cum = _dot(tri, eq.astype(BF16)) + carry
        gt = bits_ref[rows, :] > thr
        sel = jnp.where(gt | ((eq > 0.0) & (cum <= need)), 1.0, 0.0)
        eq_ref[rows, :] = sel
        return carry + jnp.sum(eq, axis=0, keepdims=True)

    lax.fori_loop(0, nb, pass_ties, jnp.zeros((1, LANES), F32))

    def pass_rank(i, carry):
        rows = block_rows(i)
        sel = eq_ref[rows, :]
        cum = _dot(tri, sel.astype(BF16)) + carry
        rank = jnp.where(sel > 0.0, cum - 1.0, -1.0)
        rank_ref[0, rows, :] = rank
        rank_t_ref[0, i] = rank.T[:N_EXPERTS]
        gate_t_ref[0, i] = gate_ref[0, rows, :].T[:N_EXPERTS]
        start_ref[0, pl.ds(i, 1), :] = carry.astype(jnp.int32)
        return carry + jnp.sum(sel, axis=0, keepdims=True)

    lax.fori_loop(0, nb, pass_rank, jnp.zeros((1, LANES), F32))


def _route(logits, tri, cap):
    b, l, _ = logits.shape
    nb = l // TOKEN_BLOCK
    return pl.pallas_call(
        functools.partial(_route_kernel, cap=cap),
        grid=(b,),
        in_specs=[
            pl.BlockSpec((1, l, LANES), lambda i: (i, 0, 0)),
            pl.BlockSpec((TOKEN_BLOCK, TOKEN_BLOCK), lambda i: (0, 0)),
        ],
        out_specs=[
            pl.BlockSpec((1, l, LANES), lambda i: (i, 0, 0)),
            pl.BlockSpec((1, nb, N_EXPERTS, TOKEN_BLOCK), lambda i: (i, 0, 0, 0)),
            pl.BlockSpec((1, l, LANES), lambda i: (i, 0, 0)),
            pl.BlockSpec((1, nb, N_EXPERTS, TOKEN_BLOCK), lambda i: (i, 0, 0, 0)),
            pl.BlockSpec((1, nb, LANES), lambda i: (i, 0, 0)),
        ],
        out_shape=[
            jax.ShapeDtypeStruct((b, l, LANES), F32),
            jax.ShapeDtypeStruct((b, nb, N_EXPERTS, TOKEN_BLOCK), F32),
            jax.ShapeDtypeStruct((b, l, LANES), F32),
            jax.ShapeDtypeStruct((b, nb, N_EXPERTS, TOKEN_BLOCK), F32),
            jax.ShapeDtypeStruct((b, nb, LANES), jnp.int32),
        ],
        scratch_shapes=[pltpu.VMEM((l, LANES), jnp.int32), pltpu.VMEM((l, LANES), F32)],
        compiler_params=_params("arbitrary"),
        name="routing",
    )(logits, tri)


def _expert_column(block, e):
    lane = lax.broadcasted_iota(jnp.int32, block.shape, 1)
    return jnp.sum(jnp.where(lane == e, block, 0.0), axis=1, keepdims=True)


def _window_start(s0, cap, window):
    return pl.multiple_of(jnp.minimum(s0 & ~(BF16_ROWS - 1), cap - window), BF16_ROWS)


def _block_fits(starts_ref, idx, is_last, cap, window):
    s0 = starts_ref[idx]
    s1 = jnp.where(is_last, cap, starts_ref[jnp.where(is_last, idx, idx + 1)])
    return (s1 - _window_start(s0, cap, window)) <= window


def _ffn_kernel(starts_ref, h_ref, rank_t_ref, w1_ref, w3_ref, w2_ref, y_ref, xg_ref,
                *, cap, window, small_window, batch_major):
    b = pl.program_id(0 if batch_major else 1)
    e = pl.program_id(1 if batch_major else 0)
    nb = rank_t_ref.shape[1]
    base = (b * N_EXPERTS + e) * nb
    xg_ref[...] = jnp.zeros(xg_ref.shape, F32)

    def gather_all(win):
        slot = lax.broadcasted_iota(jnp.int32, (win, TOKEN_BLOCK), 0)

        def gather(jb, carry):
            a = _window_start(starts_ref[base + jb], cap, win)
            rank = rank_t_ref[0, jb, pl.ds(e, 1), :]
            onehot = ((slot + a).astype(F32) == rank).astype(BF16)
            rows = pl.ds(pl.multiple_of(jb * TOKEN_BLOCK, TOKEN_BLOCK), TOKEN_BLOCK)
            xg_ref[pl.ds(a, win), :] += _dot(onehot, h_ref[0, rows, :])
            return carry

        lax.fori_loop(0, nb, gather, 0, unroll=min(nb, GATHER_UNROLL))

    if small_window < window:
        fits = lax.fori_loop(
            0, nb, lambda jb, ok: ok & _block_fits(starts_ref, base + jb, jb == nb - 1, cap, small_window), True)
        pl.when(fits)(lambda: gather_all(small_window))
        pl.when(jnp.logical_not(fits))(lambda: gather_all(window))
    else:
        gather_all(window)

    xg = xg_ref[...].astype(BF16)
    a = _dot(xg, w1_ref[0])
    hid = (a * _sigmoid(a)) * _dot(xg, w3_ref[0])
    y_ref[0, 0] = _dot(hid.astype(BF16), w2_ref[0]).astype(BF16)


def _expert_ffn(starts, h2, rank_t, w1, w3, w2, cap, batch_major):
    b, l, d = h2.shape
    f = w1.shape[2]
    nb = l // TOKEN_BLOCK
    window = min(cap, TOKEN_BLOCK + BF16_ROWS)
    small_window = min(cap, SMALL_WINDOW)
    if batch_major:
        grid = (b, N_EXPERTS)
        be = lambda i, j: (i, j)
    else:
        grid = (N_EXPERTS, b)
        be = lambda i, j: (j, i)
    grid_spec = pltpu.PrefetchScalarGridSpec(
        num_scalar_prefetch=1,
        grid=grid,
        in_specs=[
            pl.BlockSpec((1, l, d), lambda i, j, s: (be(i, j)[0], 0, 0)),
            pl.BlockSpec((1, nb, N_EXPERTS, TOKEN_BLOCK), lambda i, j, s: (be(i, j)[0], 0, 0, 0)),
            pl.BlockSpec((1, d, f), lambda i, j, s: (be(i, j)[1], 0, 0)),
            pl.BlockSpec((1, d, f), lambda i, j, s: (be(i, j)[1], 0, 0)),
            pl.BlockSpec((1, f, d), lambda i, j, s: (be(i, j)[1], 0, 0)),
        ],
        out_specs=pl.BlockSpec((1, 1, cap, d), lambda i, j, s: (be(i, j)[0], be(i, j)[1], 0, 0)),
        scratch_shapes=[pltpu.VMEM((cap, d), F32)],
    )
    return pl.pallas_call(
        functools.partial(_ffn_kernel, cap=cap, window=window, small_window=small_window, batch_major=batch_major),
        grid_spec=grid_spec,
        out_shape=jax.ShapeDtypeStruct((b, N_EXPERTS, cap, d), BF16),
        compiler_params=_params("arbitrary", "arbitrary"),
        name="expert_ffn",
    )(starts, h2, rank_t, w1, w3, w2)


def _combine_kernel(starts_ref, y_ref, rank_ref, gate_ref, rank_t_ref, gate_t_ref, x_ref, g2_ref, lng_ref, lnb_ref,
                    xo_ref, ml_ref, ystack_ref, *, cap, window, small_window, alpha):
    b = pl.program_id(0)
    jb = pl.program_id(1)
    nb = pl.num_programs(1)
    table = [(b * N_EXPERTS + e) * nb + jb for e in range(N_EXPERTS)]

    def general():
        rank = rank_ref[0]
        gate = gate_ref[0]
        slot = lax.broadcasted_iota(jnp.int32, (TOKEN_BLOCK, window), 1).astype(F32)
        ml = jnp.zeros(ml_ref.shape, F32)
        for e in range(N_EXPERTS):
            a = _window_start(starts_ref[table[e]], cap, window)
            onehot = (_expert_column(rank, e) - a.astype(F32) == slot).astype(BF16)
            ml = ml + _expert_column(gate, e) * _dot(onehot, y_ref[0, e, pl.ds(a, window), :])
        ml_ref[...] = ml

    def stacked():
        slot = lax.broadcasted_iota(jnp.int32, (small_window, TOKEN_BLOCK), 0)
        hi, lo = [], []
        for e in range(N_EXPERTS):
            a = _window_start(starts_ref[table[e]], cap, small_window)
            ystack_ref[e * small_window:(e + 1) * small_window, :] = y_ref[0, e, pl.ds(a, small_window), :]
            hit = (slot + a).astype(F32) == rank_t_ref[0, 0, e:e + 1, :]
            g_hi, g_lo = _split(jnp.where(hit, gate_t_ref[0, 0, e:e + 1, :], 0.0))
            hi.append(g_hi)
            lo.append(g_lo)
        ystack = ystack_ref[...]
        contract_rows = (((0,), (0,)), ((), ()))
        ml_ref[...] = (
            lax.dot_general(jnp.concatenate(hi, axis=0), ystack, contract_rows, preferred_element_type=F32)
            + lax.dot_general(jnp.concatenate(lo, axis=0), ystack, contract_rows, preferred_element_type=F32))

    if small_window < window:
        fits = _block_fits(starts_ref, table[0], jb == nb - 1, cap, small_window)
        for e in range(1, N_EXPERTS):
            fits = fits & _block_fits(starts_ref, table[e], jb == nb - 1, cap, small_window)
        pl.when(fits)(stacked)
        pl.when(jnp.logical_not(fits))(general)
    else:
        general()
    xo_ref[0] = _layer_norm(alpha * x_ref[0] + g2_ref[0] * ml_ref[...]) * lng_ref[...] + lnb_ref[...]


def _combine(starts, y, rank, gate, rank_t, gate_t, x, g2, ln_g, ln_b, cap, alpha):
    b, l, d = x.shape
    nb = l // TOKEN_BLOCK
    window = min(cap, TOKEN_BLOCK + BF16_ROWS)
    small_window = min(cap, SMALL_WINDOW)
    grid_spec = pltpu.PrefetchScalarGridSpec(
        num_scalar_prefetch=1,
        grid=(b, nb),
        in_specs=[
            pl.BlockSpec((1, N_EXPERTS, cap, d), lambda i, j, s: (i, 0, 0, 0)),
            pl.BlockSpec((1, TOKEN_BLOCK, LANES), lambda i, j, s: (i, j, 0)),
            pl.BlockSpec((1, TOKEN_BLOCK, LANES), lambda i, j, s: (i, j, 0)),
            pl.BlockSpec((1, 1, N_EXPERTS, TOKEN_BLOCK), lambda i, j, s: (i, j, 0, 0)),
            pl.BlockSpec((1, 1, N_EXPERTS, TOKEN_BLOCK), lambda i, j, s: (i, j, 0, 0)),
            pl.BlockSpec((1, TOKEN_BLOCK, d), lambda i, j, s: (i, j, 0)),
            pl.BlockSpec((1, 1, d), lambda i, j, s: (i, 0, 0)),
            pl.BlockSpec((1, d), lambda i, j, s: (0, 0)),
            pl.BlockSpec((1, d), lambda i, j, s: (0, 0)),
        ],
        out_specs=pl.BlockSpec((1, TOKEN_BLOCK, d), lambda i, j, s: (i, j, 0)),
        scratch_shapes=[
            pltpu.VMEM((TOKEN_BLOCK, d), F32),
            pltpu.VMEM((N_EXPERTS * small_window, d), BF16),
        ],
    )
    return pl.pallas_call(
        functools.partial(_combine_kernel, cap=cap, window=window, small_window=small_window, alpha=alpha),
        grid_spec=grid_spec,
        out_shape=jax.ShapeDtypeStruct((b, l, d), F32),
        compiler_params=_params("arbitrary", "arbitrary"),
        name="moe_combine",
    )(starts, y, rank, gate, rank_t, gate_t, x, g2, ln_g, ln_b)


def _group_mean_matrix(n, group):
    idx = jnp.arange(n) // group
    return (idx[:, None] == idx[None, :]).astype(F32) / group


def _rope_tables(n):
    rows = n // GRID_W
    r = jnp.repeat(jnp.arange(rows, dtype=F32), GRID_W)
    col = jnp.tile(jnp.arange(GRID_W, dtype=F32), rows)
    inv = ROPE_THETA ** (-jnp.arange(ROPE_AXIS_FREQS, dtype=F32) / ROPE_AXIS_FREQS)
    ang = jnp.concatenate([r[:, None] * inv, col[:, None] * inv], axis=-1)
    cos, sin = jnp.cos(ang), jnp.sin(ang)
    cos_t = jnp.tile(cos, (1, LANES // (HEAD_DIM // 2)))
    sin_t = jnp.tile(jnp.concatenate([-sin, sin], axis=-1), (1, LANES // HEAD_DIM))
    return cos_t, sin_t


def _layer_weights(l, w_in, pool_w, pool_scale, sgu_g, sgu_w, sgu_b, q_g, k_g, w_out, ln1_g, ln1_b,
                   w_router, w1, w3, w2, ln2_g, ln2_b):
    d = w_in.shape[1]
    q_scale = HEAD_DIM ** -0.5 * math.log2(math.e)
    gain = jnp.concatenate([jnp.tile(q_g[l] * q_scale, N_HEADS), jnp.tile(k_g[l], N_KV_HEADS)])[None, :]
    pool_bd = jax.scipy.linalg.block_diag(*[pool_w[l, g] for g in range(len(POOL_WINDOWS))])
    wr = jnp.pad(w_router[l], ((0, 0), (0, LANES - N_EXPERTS)))
    wr_hi = wr.astype(BF16)
    return dict(
        w_in=w_in[l].astype(BF16),
        gain=gain,
        pool_w=pool_bd.astype(BF16),
        pool_scale=pool_scale[l][None, :],
        gavg256=_group_mean_matrix(SGU_WIDTH, SGU_WIDTH // SGU_HEADS).astype(BF16),
        sgu_g=sgu_g[l].reshape(1, SGU_WIDTH),
        sgu_w=sgu_w[l].astype(BF16),
        sgu_b=jnp.repeat(sgu_b[l].T, SGU_WIDTH // SGU_HEADS, axis=1),
        w_out=w_out[l].astype(BF16),
        ln1_g=ln1_g[l][None, :], ln1_b=ln1_b[l][None, :],
        wr_hi=wr_hi, wr_lo=(wr - wr_hi.astype(F32)).astype(BF16),
        w1=w1[l].astype(BF16), w3=w3[l].astype(BF16), w2=w2[l].astype(BF16),
        ln2_g=ln2_g[l][None, :], ln2_b=ln2_b[l][None, :],
    )


def _moe(x, h2, logits, g2, lw, tri, alpha, batch_major):
    b, l, d = x.shape
    cap = EC_CAPACITY_FACTOR * l // N_EXPERTS
    rank, rank_t, gate, gate_t, starts = _route(logits, tri, cap)
    starts = jnp.swapaxes(starts[:, :, :N_EXPERTS], 1, 2).reshape(-1)
    y = _expert_ffn(starts, h2, rank_t, lw["w1"], lw["w3"], lw["w2"], cap, batch_major)
    return _combine(starts, y, rank, gate, rank_t, gate_t, x, g2, lw["ln2_g"], lw["ln2_b"], cap, alpha)


def kernel(x, c, ctx, c_ctx, w_mod, b_mod, w_in, pool_w, pool_scale, sgu_g, sgu_w, sgu_b, q_g, k_g, w_out,
           ln1_g, ln1_b, w_router, w1, w3, w2, ln2_g, ln2_b):
    batch, seq, d = x.shape
    ctx_len = ctx.shape[1]
    depth = w_mod.shape[0]
    alpha = (2 * depth) ** 0.25

    cond = jnp.concatenate([c, c_ctx[None, :], jnp.zeros((COND_ROWS - batch - 1, d), F32)], axis=0)
    mod = _modulation(cond, w_mod, b_mod)

    cos, sin = _rope_tables(seq)
    cos_c = jnp.ones((ctx_len, LANES), F32)
    sin_c = jnp.zeros((ctx_len, LANES), F32)
    gavg128 = _group_mean_matrix(LANES, HEAD_DIM).astype(BF16)
    tri = jnp.tril(jnp.ones((TOKEN_BLOCK, TOKEN_BLOCK), F32)).astype(BF16)

    xc = ctx
    for l in range(depth):
        last = l == depth - 1
        lw = _layer_weights(l, w_in, pool_w, pool_scale, sgu_g, sgu_w, sgu_b, q_g, k_g, w_out, ln1_g, ln1_b,
                            w_router, w1, w3, w2, ln2_g, ln2_b)
        m = mod[l].reshape(COND_ROWS, 6, d)
        sh1, sc1, g1, sh2, sc2, g2 = [m[:batch, i][:, None, :] for i in range(6)]
        csh1, csc1, cg1, csh2, csc2, cg2 = [jnp.broadcast_to(m[batch, i][None, None, :], (batch, 1, d))
                                            for i in range(6)]

        zpu, qt, kl, vtl = _in_projection(x, sh1, sc1, lw["w_in"], lw["gain"], cos, sin, gavg128)
        zpu_c, qt_c, kc, vtc = _in_projection(xc, csh1, csc1, lw["w_in"], lw["gain"], cos_c, sin_c, gavg128)

        attn = _attention(qt, jnp.concatenate([kl, kc], axis=1), jnp.concatenate([vtl, vtc], axis=2))
        if not last:
            attn_c = _attention(qt_c, kc, vtc)
            xc, h2c, logits_c = _mixer_output(zpu_c, attn_c, xc, cg1, csh2, csc2, lw, alpha)
            xc = _moe(xc, h2c, logits_c, cg2, lw, tri, alpha, batch_major=False)

        x, h2, logits = _mixer_output(zpu, attn, x, g1, sh2, sc2, lw, alpha)
        x = _moe(x, h2, logits, g2, lw, tri, alpha, batch_major=True)
    return x
```

```python
import functools
import math

import jax
import jax.numpy as jnp
from jax import lax
from jax.experimental import pallas as pl
from jax.experimental.pallas import tpu as pltpu

F32 = jnp.float32
BF16 = jnp.bfloat16

GRID_W = 64
POOL_WIDTH = 256
POOL_WINDOWS = (2, 4, 8, 16)
POOL_GROUP = 64
POOL_HALO = 8
SGU_WIDTH = 256
SGU_HEADS = 4
SGU_CHUNK = 128
HEAD_DIM = 64
N_HEADS = 8
N_KV_HEADS = 2
KV_GROUP = N_HEADS // N_KV_HEADS
ATTN_WIDTH = N_HEADS * HEAD_DIM
KV_WIDTH = N_KV_HEADS * HEAD_DIM
PU_WIDTH = POOL_WIDTH + 2 * SGU_WIDTH
QK_WIDTH = ATTN_WIDTH + KV_WIDTH
ROPE_THETA = 10000.0
ROPE_AXIS_FREQS = HEAD_DIM // 4
N_EXPERTS = 16
EC_CAPACITY_FACTOR = 2
LN_EPS = 1e-6

LANES = 128
SUBLANES = 8
BF16_ROWS = 16
V_EXT_ROWS = HEAD_DIM + BF16_ROWS
VMEM_LIMIT_BYTES = 56 * 1024 * 1024

ROW_TILE = 512
KV_TILE = 256
ATTN_Q_TILE = 512
TOKEN_BLOCK = 128
GATHER_UNROLL = 4
SMALL_WINDOW = 48
COND_ROWS = 16


def _params(*sem):
    return pltpu.CompilerParams(dimension_semantics=sem, vmem_limit_bytes=VMEM_LIMIT_BYTES)


def _split(x):
    hi = x.astype(BF16)
    lo = (x - hi.astype(F32)).astype(BF16)
    return hi, lo


def _dot(a, b):
    return jnp.dot(a, b, preferred_element_type=F32)


def _dot3(a, b_hi, b_lo):
    a_hi, a_lo = _split(a)
    return _dot(a_hi, b_hi) + _dot(a_hi, b_lo) + _dot(a_lo, b_hi)


def _dot_exact_rhs(a, b):
    a_hi, a_lo = _split(a)
    return _dot(a_hi, b) + _dot(a_lo, b)


def _layer_norm(x):
    mu = jnp.mean(x, axis=-1, keepdims=True)
    xc = x - mu
    var = jnp.mean(xc * xc, axis=-1, keepdims=True)
    return xc * lax.rsqrt(var + LN_EPS)


def _sigmoid(x):
    return 1.0 / (1.0 + jnp.exp(-x))


def _gelu_tanh(x):
    c = math.sqrt(2.0 / math.pi)
    return 0.5 * x * (1.0 + jnp.tanh(c * (x + 0.044715 * (x * x * x))))


def _mod_kernel(cond_ref, w_ref, b_ref, o_ref):
    a = cond_ref[...]
    a = a * _sigmoid(a)
    w_hi, w_lo = _split(w_ref[0])
    o_ref[0] = _dot3(a, w_hi, w_lo) + b_ref[0]


def _modulation(cond, w_mod, b_mod):
    depth, d, n = w_mod.shape
    tn = n // 4
    return pl.pallas_call(
        _mod_kernel,
        grid=(depth, n // tn),
        in_specs=[
            pl.BlockSpec((COND_ROWS, d), lambda l, j: (0, 0)),
            pl.BlockSpec((1, d, tn), lambda l, j: (l, 0, j)),
            pl.BlockSpec((1, 1, tn), lambda l, j: (l, 0, j)),
        ],
        out_specs=pl.BlockSpec((1, COND_ROWS, tn), lambda l, j: (l, 0, j)),
        out_shape=jax.ShapeDtypeStruct((depth, COND_ROWS, n), F32),
        compiler_params=_params("arbitrary", "arbitrary"),
        name="modulation",
    )(cond, w_mod, b_mod.reshape(depth, 1, n))


def _in_kernel(x_ref, sh_ref, sc_ref, w_ref, gain_ref, cos_ref, sin_ref, gavg_ref,
               zpu_ref, qt_ref, k_ref, vt_ref):
    gavg = gavg_ref[...]
    lane = lax.broadcasted_iota(jnp.int32, (KV_TILE, LANES), 1)
    first_half = (lane & (HEAD_DIM - 1)) < (HEAD_DIM // 2)
    row = lax.broadcasted_iota(jnp.int32, (V_EXT_ROWS - HEAD_DIM, KV_TILE), 0)
    ones_row = (row == 0).astype(F32)

    for r in range(x_ref.shape[1] // KV_TILE):
        rows = slice(r * KV_TILE, (r + 1) * KV_TILE)
        h = _layer_norm(x_ref[0, rows, :]) * (1.0 + sc_ref[0]) + sh_ref[0]
        z = _dot(h.astype(BF16), w_ref[...])
        zpu_ref[0, rows, :] = z[:, :PU_WIDTH]

        cos = cos_ref[rows, :]
        sin = sin_ref[rows, :]
        pieces = []
        for c in range(QK_WIDTH // LANES):
            lo = PU_WIDTH + c * LANES
            y = z[:, lo:lo + LANES]
            ms = _dot_exact_rhs(y * y, gavg)
            y = y * lax.rsqrt(ms + LN_EPS) * gain_ref[:, c * LANES:(c + 1) * LANES]
            partner = jnp.where(first_half,
                                pltpu.roll(y, LANES - HEAD_DIM // 2, 1),
                                pltpu.roll(y, HEAD_DIM // 2, 1))
            pieces.append(y * cos + partner * sin)
        q = jnp.concatenate(pieces[:ATTN_WIDTH // LANES], axis=1)
        qt_ref[0, :, rows] = q.T.astype(BF16)
        k_ref[0, rows, :] = pieces[-1].astype(BF16)

        vt = z[:, PU_WIDTH + QK_WIDTH:].T
        for g in range(N_KV_HEADS):
            ext = jnp.concatenate([vt[g * HEAD_DIM:(g + 1) * HEAD_DIM], ones_row], axis=0)
            vt_ref[0, g, r] = ext.astype(BF16)


def _in_projection(x, sh, sc, w_in, gain, cos, sin, gavg):
    b, l, d = x.shape
    n = w_in.shape[1]
    tm = min(ROW_TILE, l)
    nt = l // tm
    sub = tm // KV_TILE
    vec = pl.BlockSpec((1, 1, d), lambda i, j: (i, 0, 0))
    return pl.pallas_call(
        _in_kernel,
        grid=(b, nt),
        in_specs=[
            pl.BlockSpec((1, tm, d), lambda i, j: (i, j, 0)),
            vec, vec,
            pl.BlockSpec((d, n), lambda i, j: (0, 0)),
            pl.BlockSpec((1, QK_WIDTH), lambda i, j: (0, 0)),
            pl.BlockSpec((tm, LANES), lambda i, j: (j, 0)),
            pl.BlockSpec((tm, LANES), lambda i, j: (j, 0)),
            pl.BlockSpec((LANES, LANES), lambda i, j: (0, 0)),
        ],
        out_specs=[
            pl.BlockSpec((1, tm, PU_WIDTH), lambda i, j: (i, j, 0)),
            pl.BlockSpec((1, ATTN_WIDTH, tm), lambda i, j: (i, 0, j)),
            pl.BlockSpec((1, tm, KV_WIDTH), lambda i, j: (i, j, 0)),
            pl.BlockSpec((1, N_KV_HEADS, sub, V_EXT_ROWS, KV_TILE), lambda i, j: (i, 0, j, 0, 0)),
        ],
        out_shape=[
            jax.ShapeDtypeStruct((b, l, PU_WIDTH), F32),
            jax.ShapeDtypeStruct((b, ATTN_WIDTH, l), BF16),
            jax.ShapeDtypeStruct((b, l, KV_WIDTH), BF16),
            jax.ShapeDtypeStruct((b, N_KV_HEADS, l // KV_TILE, V_EXT_ROWS, KV_TILE), BF16),
        ],
        compiler_params=_params("arbitrary", "arbitrary"),
        name="in_projection",
    )(x, sh, sc, w_in, gain, cos, sin, gavg)


def _attn_kernel(qt_ref, k_ref, vt_ref, o_ref, qpad_ref, m_ref, acc_ref, p_ref, scale_ref, *, n_kv):
    tq = qt_ref.shape[2]
    zeros = jnp.zeros((HEAD_DIM, tq), BF16)
    for h in range(N_HEADS):
        qh = qt_ref[0, h * HEAD_DIM:(h + 1) * HEAD_DIM, :]
        qpad_ref[h] = jnp.concatenate([qh, zeros] if h // KV_GROUP == 0 else [zeros, qh], axis=0)
    m_ref[...] = jnp.full(m_ref.shape, -1e30, F32)
    acc_ref[...] = jnp.zeros(acc_ref.shape, F32)

    def scores(c):
        start = pl.multiple_of(c * KV_TILE, KV_TILE)
        kc = k_ref[0, pl.ds(start, KV_TILE), :]
        return [_dot(kc, qpad_ref[h]) for h in range(N_HEADS)]

    def softmax(s):
        for h in range(N_HEADS):
            m = m_ref[h]
            m_new = jnp.maximum(m, jnp.max(s[h], axis=0, keepdims=True))
            p_ref[h] = jnp.exp2(s[h] - m_new).astype(BF16)
            scale_ref[h] = jnp.exp2(m - m_new)
            m_ref[h] = m_new

    def values(c):
        for h in range(N_HEADS):
            acc_ref[h] = acc_ref[h] * scale_ref[h] + _dot(vt_ref[0, h // KV_GROUP, c], p_ref[h])

    softmax(scores(0))

    def body(c, carry):
        s = scores(c)
        values(c - 1)
        softmax(s)
        return carry

    lax.fori_loop(1, n_kv, body, 0)
    values(n_kv - 1)
    blocks = []
    for hp in range(N_HEADS // 2):
        pair = []
        for h in (2 * hp, 2 * hp + 1):
            acc = acc_ref[h]
            pair.append(acc[:HEAD_DIM] / acc[HEAD_DIM:HEAD_DIM + 1])
        blocks.append(jnp.concatenate(pair, axis=0).T)
    o_ref[0] = jnp.concatenate(blocks, axis=1).astype(BF16)


def _attention(qt, k, vt):
    b, _, lq = qt.shape
    lk = k.shape[1]
    n_kv = lk // KV_TILE
    tq = min(ATTN_Q_TILE, lq)
    return pl.pallas_call(
        functools.partial(_attn_kernel, n_kv=n_kv),
        grid=(b, lq // tq),
        in_specs=[
            pl.BlockSpec((1, ATTN_WIDTH, tq), lambda i, j: (i, 0, j)),
            pl.BlockSpec((1, lk, KV_WIDTH), lambda i, j: (i, 0, 0)),
            pl.BlockSpec((1, N_KV_HEADS, n_kv, V_EXT_ROWS, KV_TILE), lambda i, j: (i, 0, 0, 0, 0)),
        ],
        out_specs=pl.BlockSpec((1, tq, ATTN_WIDTH), lambda i, j: (i, j, 0)),
        out_shape=jax.ShapeDtypeStruct((b, lq, ATTN_WIDTH), BF16),
        scratch_shapes=[
            pltpu.VMEM((N_HEADS, 2 * HEAD_DIM, tq), BF16),
            pltpu.VMEM((N_HEADS, 1, tq), F32),
            pltpu.VMEM((N_HEADS, V_EXT_ROWS, tq), F32),
            pltpu.VMEM((N_HEADS, KV_TILE, tq), BF16),
            pltpu.VMEM((N_HEADS, 1, tq), F32),
        ],
        compiler_params=_params("arbitrary", "arbitrary"),
        name="attention",
    )(qt, k, vt)


def _mix_kernel(zp_ref, zn_ref, z_ref, attn_ref, x_ref, g1_ref, sh2_ref, sc2_ref,
                poolw_ref, pscale_ref, gavg_ref, sgug_ref, sguw_ref, sgub_ref,
                wout_ref, lng_ref, lnb_ref, wr_hi_ref, wr_lo_ref,
                xo_ref, h2_ref, logit_ref, *, seq_len, alpha):
    j = pl.program_id(1)
    nt = pl.num_programs(1)
    z = z_ref[0]
    tm = z.shape[0]
    lane = lax.broadcasted_iota(jnp.int32, (tm, POOL_WIDTH), 1)
    group = lax.shift_right_logical(lane, POOL_GROUP.bit_length() - 1)

    p = z[:, :POOL_WIDTH]
    prev = zp_ref[0] * (j > 0).astype(F32)
    nxt = zn_ref[0] * (j < nt - 1).astype(F32)
    ext = jnp.concatenate([prev, p, nxt], axis=0)
    n_ext = ext.shape[0]
    s2 = ext + pltpu.roll(ext, 1, 0)
    s4 = pltpu.roll(s2, 1, 0) + pltpu.roll(s2, n_ext - 1, 0)
    s8 = pltpu.roll(s4, 2, 0) + pltpu.roll(s4, n_ext - 2, 0)
    s16 = pltpu.roll(s8, 4, 0) + pltpu.roll(s8, n_ext - 4, 0)
    sums = [s[POOL_HALO:POOL_HALO + tm] for s in (s2, s4, s8, s16)]
    wsum = jnp.where(group == 0, sums[0], jnp.where(group == 1, sums[1], jnp.where(group == 2, sums[2], sums[3])))
    half = jnp.where(group == 0, 1, jnp.where(group == 1, 2, jnp.where(group == 2, 4, 8)))
    t = j * tm + lax.broadcasted_iota(jnp.int32, (tm, POOL_WIDTH), 0)
    cnt = jnp.minimum(t + half, seq_len) - jnp.maximum(t - half, 0)
    pin = (wsum / cnt.astype(F32) - p).astype(BF16)

    gavg = gavg_ref[...]
    head = lax.shift_right_logical(lax.broadcasted_iota(jnp.int32, (SGU_CHUNK, SGU_WIDTH), 1),
                                   (SGU_WIDTH // SGU_HEADS).bit_length() - 1)
    split = POOL_WIDTH + SGU_WIDTH

    for r in range(tm // KV_TILE):
        sub = slice(r * KV_TILE, (r + 1) * KV_TILE)
        pool = _dot(pin[sub], poolw_ref[...]) * pscale_ref[...]

        sgu_parts = []
        for c in range(KV_TILE // SGU_CHUNK):
            rows = slice(r * KV_TILE + c * SGU_CHUNK, r * KV_TILE + (c + 1) * SGU_CHUNK)
            u = _gelu_tanh(z[rows, POOL_WIDTH:POOL_WIDTH + SGU_WIDTH])
            v = _gelu_tanh(z[rows, POOL_WIDTH + SGU_WIDTH:])
            mu = _dot_exact_rhs(v, gavg)
            vc = v - mu
            var = _dot_exact_rhs(vc * vc, gavg)
            vn = vc * lax.rsqrt(var + LN_EPS) * sgug_ref[...]
            stacked = jnp.concatenate(
                [jnp.where(head == hd, vn, 0.0).astype(BF16) for hd in range(SGU_HEADS)], axis=0)
            sgu_parts.append(u * (_dot(sguw_ref[...], stacked) + sgub_ref[...]))
        sgu = jnp.concatenate(sgu_parts, axis=0)

        ps = jnp.concatenate([pool, sgu], axis=1).astype(BF16)
        y = _dot(ps, wout_ref[:split, :]) + _dot(attn_ref[0, sub, :], wout_ref[split:, :])
        xn = _layer_norm(alpha * x_ref[0, sub, :] + g1_ref[0] * y) * lng_ref[...] + lnb_ref[...]
        xo_ref[0, sub, :] = xn

        h2 = _layer_norm(xn) * (1.0 + sc2_ref[0]) + sh2_ref[0]
        h2_ref[0, sub, :] = h2.astype(BF16)
        logit_ref[0, sub, :] = _dot3(h2, wr_hi_ref[...], wr_lo_ref[...])


def _mixer_output(zpu, attn, x, g1, sh2, sc2, lw, alpha):
    b, l, d = x.shape
    tm = min(ROW_TILE, l)
    nt = l // tm
    hb = tm // POOL_HALO
    n_halo = l // POOL_HALO
    vec = pl.BlockSpec((1, 1, d), lambda i, j: (i, 0, 0))

    def full(a):
        nd = a.ndim
        return pl.BlockSpec(a.shape, lambda i, j: (0,) * nd)

    weights = [lw["pool_w"], lw["pool_scale"], lw["gavg256"], lw["sgu_g"], lw["sgu_w"], lw["sgu_b"],
               lw["w_out"], lw["ln1_g"], lw["ln1_b"], lw["wr_hi"], lw["wr_lo"]]
    return pl.pallas_call(
        functools.partial(_mix_kernel, seq_len=l, alpha=alpha),
        grid=(b, nt),
        in_specs=[
            pl.BlockSpec((1, POOL_HALO, POOL_WIDTH), lambda i, j: (i, jnp.maximum(j * hb - 1, 0), 0)),
            pl.BlockSpec((1, POOL_HALO, POOL_WIDTH), lambda i, j: (i, jnp.minimum((j + 1) * hb, n_halo - 1), 0)),
            pl.BlockSpec((1, tm, PU_WIDTH), lambda i, j: (i, j, 0)),
            pl.BlockSpec((1, tm, ATTN_WIDTH), lambda i, j: (i, j, 0)),
            pl.BlockSpec((1, tm, d), lambda i, j: (i, j, 0)),
            vec, vec, vec,
        ] + [full(w) for w in weights],
        out_specs=[
            pl.BlockSpec((1, tm, d), lambda i, j: (i, j, 0)),
            pl.BlockSpec((1, tm, d), lambda i, j: (i, j, 0)),
            pl.BlockSpec((1, tm, LANES), lambda i, j: (i, j, 0)),
        ],
        out_shape=[
            jax.ShapeDtypeStruct((b, l, d), F32),
            jax.ShapeDtypeStruct((b, l, d), BF16),
            jax.ShapeDtypeStruct((b, l, LANES), F32),
        ],
        compiler_params=_params("arbitrary", "arbitrary"),
        name="mixer_output",
    )(zpu, zpu, zpu, attn, x, g1, sh2, sc2, *weights)


def _route_kernel(logit_ref, tri_ref, rank_ref, rank_t_ref, gate_ref, gate_t_ref, start_ref, bits_ref, eq_ref, *, cap):
    l = logit_ref.shape[1]
    nb = l // TOKEN_BLOCK
    lane = lax.broadcasted_iota(jnp.int32, (l, LANES), 1)
    valid = lane < N_EXPERTS
    x = jnp.where(valid, logit_ref[0], -jnp.inf)
    e = jnp.exp(x - jnp.max(x, axis=1, keepdims=True))
    aff = e / jnp.sum(e, axis=1, keepdims=True)
    gate_ref[0] = aff
    bits_ref[...] = pltpu.bitcast(aff, jnp.int32)

    def bisect(i, thr):
        cand = thr | lax.shift_left(jnp.int32(1), 30 - i)
        cnt = jnp.sum((bits_ref[...] >= cand).astype(F32), axis=0, keepdims=True)
        return jnp.where(cnt >= float(cap), cand, thr)

    thr = lax.fori_loop(0, 31, bisect, jnp.zeros((1, LANES), jnp.int32))
    bits = bits_ref[...]
    n_gt = jnp.sum((bits > thr).astype(F32), axis=0, keepdims=True)
    need = float(cap) - n_gt
    eq_ref[...] = (bits == thr).astype(F32)
    tri = tri_ref[...]

    def block_rows(i):
        return pl.ds(pl.multiple_of(i * TOKEN_BLOCK, TOKEN_BLOCK), TOKEN_BLOCK)

    def pass_ties(i, carry):
        rows = block_rows(i)
        eq = eq_ref[rows, :]
        cum = _dot(tri, eq.astype(BF16)) + carry
        gt = bits_ref[rows, :] > thr
        sel = jnp.where(gt | ((eq > 0.0) & (cum <= need)), 1.0, 0.0)
        eq_ref[rows, :] = sel
        return carry + jnp.sum(eq, axis=0, keepdims=True)

    lax.fori_loop(0, nb, pass_ties, jnp.zeros((1, LANES), F32))

    def pass_rank(i, carry):
        rows = block_rows(i)
        sel = eq_ref[rows, :]
        cum = _dot(tri, sel.astype(BF16)) + carry
        rank = jnp.where(sel > 0.0, cum - 1.0, -1.0)
        rank_ref[0, rows, :] = rank
        rank_t_ref[0, i] = rank.T[:N_EXPERTS]
        gate_t_ref[0, i] = gate_ref[0, rows, :].T[:N_EXPERTS]
        start_ref[0, pl.ds(i, 1), :] = carry.astype(jnp.int32)
        return carry + jnp.sum(sel, axis=0, keepdims=True)

    lax.fori_loop(0, nb, pass_rank, jnp.zeros((1, LANES), F32))


def _route(logits, tri, cap):
    b, l, _ = logits.shape
    nb = l // TOKEN_BLOCK
    return pl.pallas_call(
        functools.partial(_route_kernel, cap=cap),
        grid=(b,),
        in_specs=[
            pl.BlockSpec((1, l, LANES), lambda i: (i, 0, 0)),
            pl.BlockSpec((TOKEN_BLOCK, TOKEN_BLOCK), lambda i: (0, 0)),
        ],
        out_specs=[
            pl.BlockSpec((1, l, LANES), lambda i: (i, 0, 0)),
            pl.BlockSpec((1, nb, N_EXPERTS, TOKEN_BLOCK), lambda i: (i, 0, 0, 0)),
            pl.BlockSpec((1, l, LANES), lambda i: (i, 0, 0)),
            pl.BlockSpec((1, nb, N_EXPERTS, TOKEN_BLOCK), lambda i: (i, 0, 0, 0)),
            pl.BlockSpec((1, nb, LANES), lambda i: (i, 0, 0)),
        ],
        out_shape=[
            jax.ShapeDtypeStruct((b, l, LANES), F32),
            jax.ShapeDtypeStruct((b, nb, N_EXPERTS, TOKEN_BLOCK), F32),
            jax.ShapeDtypeStruct((b, l, LANES), F32),
            jax.ShapeDtypeStruct((b, nb, N_EXPERTS, TOKEN_BLOCK), F32),
            jax.ShapeDtypeStruct((b, nb, LANES), jnp.int32),
        ],
        scratch_shapes=[pltpu.VMEM((l, LANES), jnp.int32), pltpu.VMEM((l, LANES), F32)],
        compiler_params=_params("arbitrary"),
        name="routing",
    )(logits, tri)


def _expert_column(block, e):
    lane = lax.broadcasted_iota(jnp.int32, block.shape, 1)
    return jnp.sum(jnp.where(lane == e, block, 0.0), axis=1, keepdims=True)


def _window_start(s0, cap, window):
    return pl.multiple_of(jnp.minimum(s0 & ~(BF16_ROWS - 1), cap - window), BF16_ROWS)


def _block_fits(starts_ref, idx, is_last, cap, window):
    s0 = starts_ref[idx]
    s1 = jnp.where(is_last, cap, starts_ref[jnp.where(is_last, idx, idx + 1)])
    return (s1 - _window_start(s0, cap, window)) <= window


def _ffn_kernel(starts_ref, h_ref, rank_t_ref, w1_ref, w3_ref, w2_ref, y_ref, xg_ref,
                *, cap, window, small_window, batch_major):
    b = pl.program_id(0 if batch_major else 1)
    e = pl.program_id(1 if batch_major else 0)
    nb = rank_t_ref.shape[1]
    base = (b * N_EXPERTS + e) * nb
    xg_ref[...] = jnp.zeros(xg_ref.shape, F32)

    def gather_all(win):
        slot = lax.broadcasted_iota(jnp.int32, (win, TOKEN_BLOCK), 0)

        def gather(jb, carry):
            a = _window_start(starts_ref[base + jb], cap, win)
            rank = rank_t_ref[0, jb, pl.ds(e, 1), :]
            onehot = ((slot + a).astype(F32) == rank).astype(BF16)
            rows = pl.ds(pl.multiple_of(jb * TOKEN_BLOCK, TOKEN_BLOCK), TOKEN_BLOCK)
            xg_ref[pl.ds(a, win), :] += _dot(onehot, h_ref[0, rows, :])
            return carry

        lax.fori_loop(0, nb, gather, 0, unroll=min(nb, GATHER_UNROLL))

    if small_window < window:
        fits = lax.fori_loop(
            0, nb, lambda jb, ok: ok & _block_fits(starts_ref, base + jb, jb == nb - 1, cap, small_window), True)
        pl.when(fits)(lambda: gather_all(small_window))
        pl.when(jnp.logical_not(fits))(lambda: gather_all(window))
    else:
        gather_all(window)

    xg = xg_ref[...].astype(BF16)
    a = _dot(xg, w1_ref[0])
    hid = (a * _sigmoid(a)) * _dot(xg, w3_ref[0])
    y_ref[0, 0] = _dot(hid.astype(BF16), w2_ref[0]).astype(BF16)


def _expert_ffn(starts, h2, rank_t, w1, w3, w2, cap, batch_major):
    b, l, d = h2.shape
    f = w1.shape[2]
    nb = l // TOKEN_BLOCK
    window = min(cap, TOKEN_BLOCK + BF16_ROWS)
    small_window = min(cap, SMALL_WINDOW)
    if batch_major:
        grid = (b, N_EXPERTS)
        be = lambda i, j: (i, j)
    else:
        grid = (N_EXPERTS, b)
        be = lambda i, j: (j, i)
    grid_spec = pltpu.PrefetchScalarGridSpec(
        num_scalar_prefetch=1,
        grid=grid,
        in_specs=[
            pl.BlockSpec((1, l, d), lambda i, j, s: (be(i, j)[0], 0, 0)),
            pl.BlockSpec((1, nb, N_EXPERTS, TOKEN_BLOCK), lambda i, j, s: (be(i, j)[0], 0, 0, 0)),
            pl.BlockSpec((1, d, f), lambda i, j, s: (be(i, j)[1], 0, 0)),
            pl.BlockSpec((1, d, f), lambda i, j, s: (be(i, j)[1], 0, 0)),
            pl.BlockSpec((1, f, d), lambda i, j, s: (be(i, j)[1], 0, 0)),
        ],
        out_specs=pl.BlockSpec((1, 1, cap, d), lambda i, j, s: (be(i, j)[0], be(i, j)[1], 0, 0)),
        scratch_shapes=[pltpu.VMEM((cap, d), F32)],
    )
    return pl.pallas_call(
        functools.partial(_ffn_kernel, cap=cap, window=window, small_window=small_window, batch_major=batch_major),
        grid_spec=grid_spec,
        out_shape=jax.ShapeDtypeStruct((b, N_EXPERTS, cap, d), BF16),
        compiler_params=_params("arbitrary", "arbitrary"),
        name="expert_ffn",
    )(starts, h2, rank_t, w1, w3, w2)


def _combine_kernel(starts_ref, y_ref, rank_ref, gate_ref, rank_t_ref, gate_t_ref, x_ref, g2_ref, lng_ref, lnb_ref,
                    xo_ref, ml_ref, ystack_ref, *, cap, window, small_window, alpha):
    b = pl.program_id(0)
    jb = pl.program_id(1)
    nb = pl.num_programs(1)
    table = [(b * N_EXPERTS + e) * nb + jb for e in range(N_EXPERTS)]

    def general():
        rank = rank_ref[0]
        gate = gate_ref[0]
        slot = lax.broadcasted_iota(jnp.int32, (TOKEN_BLOCK, window), 1).astype(F32)
        ml = jnp.zeros(ml_ref.shape, F32)
        for e in range(N_EXPERTS):
            a = _window_start(starts_ref[table[e]], cap, window)
            onehot = (_expert_column(rank, e) - a.astype(F32) == slot).astype(BF16)
            ml = ml + _expert_column(gate, e) * _dot(onehot, y_ref[0, e, pl.ds(a, window), :])
        ml_ref[...] = ml

    def stacked():
        slot = lax.broadcasted_iota(jnp.int32, (small_window, TOKEN_BLOCK), 0)
        hi, lo = [], []
        for e in range(N_EXPERTS):
            a = _window_start(starts_ref[table[e]], cap, small_window)
            ystack_ref[e * small_window:(e + 1) * small_window, :] = y_ref[0, e, pl.ds(a, small_window), :]
            hit = (slot + a).astype(F32) == rank_t_ref[0, 0, e:e + 1, :]
            g_hi, g_lo = _split(jnp.where(hit, gate_t_ref[0, 0, e:e + 1, :], 0.0))
            hi.append(g_hi)
            lo.append(g_lo)
        ystack = ystack_ref[...]
        contract_rows = (((0,), (0,)), ((), ()))
        ml_ref[...] = (
            lax.dot_general(jnp.concatenate(hi, axis=0), ystack, contract_rows, preferred_element_type=F32)
            + lax.dot_general(jnp.concatenate(lo, axis=0), ystack, contract_rows, preferred_element_type=F32))

    if small_window < window:
        fits = _block_fits(starts_ref, table[0], jb == nb - 1, cap, small_window)
        for e in range(1, N_EXPERTS):
            fits = fits & _block_fits(starts_ref, table[e], jb == nb - 1, cap, small_window)
        pl.when(fits)(stacked)
        pl.when(jnp.logical_not(fits))(general)
    else:
        general()
    xo_ref[0] = _layer_norm(alpha * x_ref[0] + g2_ref[0] * ml_ref[...]) * lng_ref[...] + lnb_ref[...]


def _combine(starts, y, rank, gate, rank_t, gate_t, x, g2, ln_g, ln_b, cap, alpha):
    b, l, d = x.shape
    nb = l // TOKEN_BLOCK
    window = min(cap, TOKEN_BLOCK + BF16_ROWS)
    small_window = min(cap, SMALL_WINDOW)
    grid_spec = pltpu.PrefetchScalarGridSpec(
        num_scalar_prefetch=1,
        grid=(b, nb),
        in_specs=[
            pl.BlockSpec((1, N_EXPERTS, cap, d), lambda i, j, s: (i, 0, 0, 0)),
            pl.BlockSpec((1, TOKEN_BLOCK, LANES), lambda i, j, s: (i, j, 0)),
            pl.BlockSpec((1, TOKEN_BLOCK, LANES), lambda i, j, s: (i, j, 0)),
            pl.BlockSpec((1, 1, N_EXPERTS, TOKEN_BLOCK), lambda i, j, s: (i, j, 0, 0)),
            pl.BlockSpec((1, 1, N_EXPERTS, TOKEN_BLOCK), lambda i, j, s: (i, j, 0, 0)),
            pl.BlockSpec((1, TOKEN_BLOCK, d), lambda i, j, s: (i, j, 0)),
            pl.BlockSpec((1, 1, d), lambda i, j, s: (i, 0, 0)),
            pl.BlockSpec((1, d), lambda i, j, s: (0, 0)),
            pl.BlockSpec((1, d), lambda i, j, s: (0, 0)),
        ],
        out_specs=pl.BlockSpec((1, TOKEN_BLOCK, d), lambda i, j, s: (i, j, 0)),
        scratch_shapes=[
            pltpu.VMEM((TOKEN_BLOCK, d), F32),
            pltpu.VMEM((N_EXPERTS * small_window, d), BF16),
        ],
    )
    return pl.pallas_call(
        functools.partial(_combine_kernel, cap=cap, window=window, small_window=small_window, alpha=alpha),
        grid_spec=grid_spec,
        out_shape=jax.ShapeDtypeStruct((b, l, d), F32),
        compiler_params=_params("arbitrary", "arbitrary"),
        name="moe_combine",
    )(starts, y, rank, gate, rank_t, gate_t, x, g2, ln_g, ln_b)


def _group_mean_matrix(n, group):
    idx = jnp.arange(n) // group
    return (idx[:, None] == idx[None, :]).astype(F32) / group


def _rope_tables(n):
    rows = n // GRID_W
    r = jnp.repeat(jnp.arange(rows, dtype=F32), GRID_W)
    col = jnp.tile(jnp.arange(GRID_W, dtype=F32), rows)
    inv = ROPE_THETA ** (-jnp.arange(ROPE_AXIS_FREQS, dtype=F32) / ROPE_AXIS_FREQS)
    ang = jnp.concatenate([r[:, None] * inv, col[:, None] * inv], axis=-1)
    cos, sin = jnp.cos(ang), jnp.sin(ang)
    cos_t = jnp.tile(cos, (1, LANES // (HEAD_DIM // 2)))
    sin_t = jnp.tile(jnp.concatenate([-sin, sin], axis=-1), (1, LANES // HEAD_DIM))
    return cos_t, sin_t


def _layer_weights(l, w_in, pool_w, pool_scale, sgu_g, sgu_w, sgu_b, q_g, k_g, w_out, ln1_g, ln1_b,
                   w_router, w1, w3, w2, ln2_g, ln2_b):
    d = w_in.shape[1]
    q_scale = HEAD_DIM ** -0.5 * math.log2(math.e)
    gain = jnp.concatenate([jnp.tile(q_g[l] * q_scale, N_HEADS), jnp.tile(k_g[l], N_KV_HEADS)])[None, :]
    pool_bd = jax.scipy.linalg.block_diag(*[pool_w[l, g] for g in range(len(POOL_WINDOWS))])
    wr = jnp.pad(w_router[l], ((0, 0), (0, LANES - N_EXPERTS)))
    wr_hi = wr.astype(BF16)
    return dict(
        w_in=w_in[l].astype(BF16),
        gain=gain,
        pool_w=pool_bd.astype(BF16),
        pool_scale=pool_scale[l][None, :],
        gavg256=_group_mean_matrix(SGU_WIDTH, SGU_WIDTH // SGU_HEADS).astype(BF16),
        sgu_g=sgu_g[l].reshape(1, SGU_WIDTH),
        sgu_w=jnp.swapaxes(sgu_w[l], 0, 1).reshape(SGU_CHUNK, SGU_HEADS * SGU_CHUNK).astype(BF16),
        sgu_b=jnp.repeat(sgu_b[l].T, SGU_WIDTH // SGU_HEADS, axis=1),
        w_out=w_out[l].astype(BF16),
        ln1_g=ln1_g[l][None, :], ln1_b=ln1_b[l][None, :],
        wr_hi=wr_hi, wr_lo=(wr - wr_hi.astype(F32)).astype(BF16),
        w1=w1[l].astype(BF16), w3=w3[l].astype(BF16), w2=w2[l].astype(BF16),
        ln2_g=ln2_g[l][None, :], ln2_b=ln2_b[l][None, :],
    )


def _moe(x, h2, logits, g2, lw, tri, alpha, batch_major):
    b, l, d = x.shape
    cap = EC_CAPACITY_FACTOR * l // N_EXPERTS
    rank, rank_t, gate, gate_t, starts = _route(logits, tri, cap)
    starts = jnp.swapaxes(starts[:, :, :N_EXPERTS], 1, 2).reshape(-1)
    y = _expert_ffn(starts, h2, rank_t, lw["w1"], lw["w3"], lw["w2"], cap, batch_major)
    return _combine(starts, y, rank, gate, rank_t, gate_t, x, g2, lw["ln2_g"], lw["ln2_b"], cap, alpha)


def kernel(x, c, ctx, c_ctx, w_mod, b_mod, w_in, pool_w, pool_scale, sgu_g, sgu_w, sgu_b, q_g, k_g, w_out,
           ln1_g, ln1_b, w_router, w1, w3, w2, ln2_g, ln2_b):
    batch, seq, d = x.shape
    ctx_len = ctx.shape[1]
    depth = w_mod.shape[0]
    alpha = (2 * depth) ** 0.25

    cond = jnp.concatenate([c, c_ctx[None, :], jnp.zeros((COND_ROWS - batch - 1, d), F32)], axis=0)
    mod = _modulation(cond, w_mod, b_mod)

    cos, sin = _rope_tables(seq)
    cos_c = jnp.ones((ctx_len, LANES), F32)
    sin_c = jnp.zeros((ctx_len, LANES), F32)
    gavg128 = _group_mean_matrix(LANES, HEAD_DIM).astype(BF16)
    tri = jnp.tril(jnp.ones((TOKEN_BLOCK, TOKEN_BLOCK), F32)).astype(BF16)

    xc = ctx
    for l in range(depth):
        last = l == depth - 1
        lw = _layer_weights(l, w_in, pool_w, pool_scale, sgu_g, sgu_w, sgu_b, q_g, k_g, w_out, ln1_g, ln1_b,
                            w_router, w1, w3, w2, ln2_g, ln2_b)
        m = mod[l].reshape(COND_ROWS, 6, d)
        sh1, sc1, g1, sh2, sc2, g2 = [m[:batch, i][:, None, :] for i in range(6)]
        csh1, csc1, cg1, csh2, csc2, cg2 = [jnp.broadcast_to(m[batch, i][None, None, :], (batch, 1, d))
                                            for i in range(6)]

        zpu, qt, kl, vtl = _in_projection(x, sh1, sc1, lw["w_in"], lw["gain"], cos, sin, gavg128)
        zpu_c, qt_c, kc, vtc = _in_projection(xc, csh1, csc1, lw["w_in"], lw["gain"], cos_c, sin_c, gavg128)

        attn = _attention(qt, jnp.concatenate([kl, kc], axis=1), jnp.concatenate([vtl, vtc], axis=2))
        if not last:
            attn_c = _attention(qt_c, kc, vtc)
            xc, h2c, logits_c = _mixer_output(zpu_c, attn_c, xc, cg1, csh2, csc2, lw, alpha)
            xc = _moe(xc, h2c, logits_c, cg2, lw, tri, alpha, batch_major=False)

        x, h2, logits = _mixer_output(zpu, attn, x, g1, sh2, sc2, lw, alpha)
        x = _moe(x, h2, logits, g2, lw, tri, alpha, batch_major=True)
    return x
```

```python
import functools
import math

import jax
import jax.numpy as jnp
from jax import lax
from jax.experimental import pallas as pl
from jax.experimental.pallas import tpu as pltpu

F32 = jnp.float32
BF16 = jnp.bfloat16

GRID_W = 64
POOL_WIDTH = 256
POOL_WINDOWS = (2, 4, 8, 16)
POOL_GROUP = 64
POOL_HALO = 8
SGU_WIDTH = 256
SGU_HEADS = 4
SGU_CHUNK = 128
HEAD_DIM = 64
N_HEADS = 8
N_KV_HEADS = 2
KV_GROUP = N_HEADS // N_KV_HEADS
ATTN_WIDTH = N_HEADS * HEAD_DIM
KV_WIDTH = N_KV_HEADS * HEAD_DIM
PU_WIDTH = POOL_WIDTH + 2 * SGU_WIDTH
QK_WIDTH = ATTN_WIDTH + KV_WIDTH
ROPE_THETA = 10000.0
ROPE_AXIS_FREQS = HEAD_DIM // 4
N_EXPERTS = 16
EC_CAPACITY_FACTOR = 2
LN_EPS = 1e-6

LANES = 128
SUBLANES = 8
BF16_ROWS = 16
V_EXT_ROWS = HEAD_DIM + BF16_ROWS
VMEM_LIMIT_BYTES = 56 * 1024 * 1024

ROW_TILE = 512
KV_TILE = 256
ATTN_Q_TILE = 512
TOKEN_BLOCK = 128
GATHER_UNROLL = 4
FFN_ROWS = 512
SMALL_WINDOW = 48
COND_ROWS = 16


def _params(*sem):
    return pltpu.CompilerParams(dimension_semantics=sem, vmem_limit_bytes=VMEM_LIMIT_BYTES)


def _split(x):
    hi = x.astype(BF16)
    lo = (x - hi.astype(F32)).astype(BF16)
    return hi, lo


def _dot(a, b):
    return jnp.dot(a, b, preferred_element_type=F32)


def _dot3(a, b_hi, b_lo):
    a_hi, a_lo = _split(a)
    return _dot(a_hi, b_hi) + _dot(a_hi, b_lo) + _dot(a_lo, b_hi)


def _dot_exact_rhs(a, b):
    a_hi, a_lo = _split(a)
    return _dot(a_hi, b) + _dot(a_lo, b)


def _layer_norm(x):
    mu = jnp.mean(x, axis=-1, keepdims=True)
    xc = x - mu
    var = jnp.mean(xc * xc, axis=-1, keepdims=True)
    return xc * lax.rsqrt(var + LN_EPS)


def _sigmoid(x):
    return 1.0 / (1.0 + jnp.exp(-x))


def _gelu_tanh(x):
    c = math.sqrt(2.0 / math.pi)
    return 0.5 * x * (1.0 + jnp.tanh(c * (x + 0.044715 * (x * x * x))))


def _mod_kernel(cond_ref, w_ref, b_ref, o_ref):
    a = cond_ref[...]
    a = a * _sigmoid(a)
    w_hi, w_lo = _split(w_ref[0])
    o_ref[0] = _dot3(a, w_hi, w_lo) + b_ref[0]


def _modulation(cond, w_mod, b_mod):
    depth, d, n = w_mod.shape
    tn = n // 4
    return pl.pallas_call(
        _mod_kernel,
        grid=(depth, n // tn),
        in_specs=[
            pl.BlockSpec((COND_ROWS, d), lambda l, j: (0, 0)),
            pl.BlockSpec((1, d, tn), lambda l, j: (l, 0, j)),
            pl.BlockSpec((1, 1, tn), lambda l, j: (l, 0, j)),
        ],
        out_specs=pl.BlockSpec((1, COND_ROWS, tn), lambda l, j: (l, 0, j)),
        out_shape=jax.ShapeDtypeStruct((depth, COND_ROWS, n), F32),
        compiler_params=_params("arbitrary", "arbitrary"),
        name="modulation",
    )(cond, w_mod, b_mod.reshape(depth, 1, n))


def _in_kernel(x_ref, sh_ref, sc_ref, w_ref, gain_ref, cos_ref, sin_ref, gavg_ref,
               zpu_ref, qt_ref, k_ref, vt_ref):
    gavg = gavg_ref[...]
    lane = lax.broadcasted_iota(jnp.int32, (KV_TILE, LANES), 1)
    first_half = (lane & (HEAD_DIM - 1)) < (HEAD_DIM // 2)
    row = lax.broadcasted_iota(jnp.int32, (V_EXT_ROWS - HEAD_DIM, KV_TILE), 0)
    ones_row = (row == 0).astype(F32)

    for r in range(x_ref.shape[1] // KV_TILE):
        rows = slice(r * KV_TILE, (r + 1) * KV_TILE)
        h = _layer_norm(x_ref[0, rows, :]) * (1.0 + sc_ref[0]) + sh_ref[0]
        z = _dot(h.astype(BF16), w_ref[...])
        zpu_ref[0, rows, :] = z[:, :PU_WIDTH]

        cos = cos_ref[rows, :]
        sin = sin_ref[rows, :]
        pieces = []
        for c in range(QK_WIDTH // LANES):
            lo = PU_WIDTH + c * LANES
            y = z[:, lo:lo + LANES]
            ms = _dot_exact_rhs(y * y, gavg)
            y = y * lax.rsqrt(ms + LN_EPS) * gain_ref[:, c * LANES:(c + 1) * LANES]
            partner = jnp.where(first_half,
                                pltpu.roll(y, LANES - HEAD_DIM // 2, 1),
                                pltpu.roll(y, HEAD_DIM // 2, 1))
            pieces.append(y * cos + partner * sin)
        q = jnp.concatenate(pieces[:ATTN_WIDTH // LANES], axis=1)
        qt_ref[0, :, rows] = q.T.astype(BF16)
        k_ref[0, rows, :] = pieces[-1].astype(BF16)

        vt = z[:, PU_WIDTH + QK_WIDTH:].T
        for g in range(N_KV_HEADS):
            ext = jnp.concatenate([vt[g * HEAD_DIM:(g + 1) * HEAD_DIM], ones_row], axis=0)
            vt_ref[0, g, r] = ext.astype(BF16)


def _in_projection(x, sh, sc, w_in, gain, cos, sin, gavg):
    b, l, d = x.shape
    n = w_in.shape[1]
    tm = min(ROW_TILE, l)
    nt = l // tm
    sub = tm // KV_TILE
    vec = pl.BlockSpec((1, 1, d), lambda i, j: (i, 0, 0))
    return pl.pallas_call(
        _in_kernel,
        grid=(b, nt),
        in_specs=[
            pl.BlockSpec((1, tm, d), lambda i, j: (i, j, 0)),
            vec, vec,
            pl.BlockSpec((d, n), lambda i, j: (0, 0)),
            pl.BlockSpec((1, QK_WIDTH), lambda i, j: (0, 0)),
            pl.BlockSpec((tm, LANES), lambda i, j: (j, 0)),
            pl.BlockSpec((tm, LANES), lambda i, j: (j, 0)),
            pl.BlockSpec((LANES, LANES), lambda i, j: (0, 0)),
        ],
        out_specs=[
            pl.BlockSpec((1, tm, PU_WIDTH), lambda i, j: (i, j, 0)),
            pl.BlockSpec((1, ATTN_WIDTH, tm), lambda i, j: (i, 0, j)),
            pl.BlockSpec((1, tm, KV_WIDTH), lambda i, j: (i, j, 0)),
            pl.BlockSpec((1, N_KV_HEADS, sub, V_EXT_ROWS, KV_TILE), lambda i, j: (i, 0, j, 0, 0)),
        ],
        out_shape=[
            jax.ShapeDtypeStruct((b, l, PU_WIDTH), F32),
            jax.ShapeDtypeStruct((b, ATTN_WIDTH, l), BF16),
            jax.ShapeDtypeStruct((b, l, KV_WIDTH), BF16),
            jax.ShapeDtypeStruct((b, N_KV_HEADS, l // KV_TILE, V_EXT_ROWS, KV_TILE), BF16),
        ],
        compiler_params=_params("arbitrary", "arbitrary"),
        name="in_projection",
    )(x, sh, sc, w_in, gain, cos, sin, gavg)


def _attn_kernel(qt_ref, k_ref, vt_ref, o_ref, qpad_ref, m_ref, acc_ref, p_ref, scale_ref, *, n_kv):
    tq = qt_ref.shape[2]
    zeros = jnp.zeros((HEAD_DIM, tq), BF16)
    for h in range(N_HEADS):
        qh = qt_ref[0, h * HEAD_DIM:(h + 1) * HEAD_DIM, :]
        qpad_ref[h] = jnp.concatenate([qh, zeros] if h // KV_GROUP == 0 else [zeros, qh], axis=0)
    m_ref[...] = jnp.full(m_ref.shape, -1e30, F32)
    acc_ref[...] = jnp.zeros(acc_ref.shape, F32)

    def scores(c):
        start = pl.multiple_of(c * KV_TILE, KV_TILE)
        kc = k_ref[0, pl.ds(start, KV_TILE), :]
        return [_dot(kc, qpad_ref[h]) for h in range(N_HEADS)]

    def softmax(s):
        for h in range(N_HEADS):
            m = m_ref[h]
            m_new = jnp.maximum(m, jnp.max(s[h], axis=0, keepdims=True))
            p_ref[h] = jnp.exp2(s[h] - m_new).astype(BF16)
            scale_ref[h] = jnp.exp2(m - m_new)
            m_ref[h] = m_new

    def values(c):
        for h in range(N_HEADS):
            acc_ref[h] = acc_ref[h] * scale_ref[h] + _dot(vt_ref[0, h // KV_GROUP, c], p_ref[h])

    softmax(scores(0))

    def body(c, carry):
        s = scores(c)
        values(c - 1)
        softmax(s)
        return carry

    lax.fori_loop(1, n_kv, body, 0)
    values(n_kv - 1)
    blocks = []
    for hp in range(N_HEADS // 2):
        pair = []
        for h in (2 * hp, 2 * hp + 1):
            acc = acc_ref[h]
            pair.append(acc[:HEAD_DIM] / acc[HEAD_DIM:HEAD_DIM + 1])
        blocks.append(jnp.concatenate(pair, axis=0).T)
    o_ref[0] = jnp.concatenate(blocks, axis=1).astype(BF16)


def _attention(qt, k, vt):
    b, _, lq = qt.shape
    lk = k.shape[1]
    n_kv = lk // KV_TILE
    tq = min(ATTN_Q_TILE, lq)
    return pl.pallas_call(
        functools.partial(_attn_kernel, n_kv=n_kv),
        grid=(b, lq // tq),
        in_specs=[
            pl.BlockSpec((1, ATTN_WIDTH, tq), lambda i, j: (i, 0, j)),
            pl.BlockSpec((1, lk, KV_WIDTH), lambda i, j: (i, 0, 0)),
            pl.BlockSpec((1, N_KV_HEADS, n_kv, V_EXT_ROWS, KV_TILE), lambda i, j: (i, 0, 0, 0, 0)),
        ],
        out_specs=pl.BlockSpec((1, tq, ATTN_WIDTH), lambda i, j: (i, j, 0)),
        out_shape=jax.ShapeDtypeStruct((b, lq, ATTN_WIDTH), BF16),
        scratch_shapes=[
            pltpu.VMEM((N_HEADS, 2 * HEAD_DIM, tq), BF16),
            pltpu.VMEM((N_HEADS, 1, tq), F32),
            pltpu.VMEM((N_HEADS, V_EXT_ROWS, tq), F32),
            pltpu.VMEM((N_HEADS, KV_TILE, tq), BF16),
            pltpu.VMEM((N_HEADS, 1, tq), F32),
        ],
        compiler_params=_params("arbitrary", "arbitrary"),
        name="attention",
    )(qt, k, vt)


def _mix_kernel(zp_ref, zn_ref, z_ref, attn_ref, x_ref, g1_ref, sh2_ref, sc2_ref,
                poolw_ref, pscale_ref, gavg_ref, sgug_ref, sguw_ref, sgub_ref,
                wout_ref, lng_ref, lnb_ref, wr_hi_ref, wr_lo_ref,
                xo_ref, h2_ref, logit_ref, *, seq_len, alpha):
    j = pl.program_id(1)
    nt = pl.num_programs(1)
    z = z_ref[0]
    tm = z.shape[0]
    lane = lax.broadcasted_iota(jnp.int32, (tm, POOL_WIDTH), 1)
    group = lax.shift_right_logical(lane, POOL_GROUP.bit_length() - 1)

    p = z[:, :POOL_WIDTH]
    prev = zp_ref[0] * (j > 0).astype(F32)
    nxt = zn_ref[0] * (j < nt - 1).astype(F32)
    ext = jnp.concatenate([prev, p, nxt], axis=0)
    n_ext = ext.shape[0]
    s2 = ext + pltpu.roll(ext, 1, 0)
    s4 = pltpu.roll(s2, 1, 0) + pltpu.roll(s2, n_ext - 1, 0)
    s8 = pltpu.roll(s4, 2, 0) + pltpu.roll(s4, n_ext - 2, 0)
    s16 = pltpu.roll(s8, 4, 0) + pltpu.roll(s8, n_ext - 4, 0)
    sums = [s[POOL_HALO:POOL_HALO + tm] for s in (s2, s4, s8, s16)]
    wsum = jnp.where(group == 0, sums[0], jnp.where(group == 1, sums[1], jnp.where(group == 2, sums[2], sums[3])))
    half = jnp.where(group == 0, 1, jnp.where(group == 1, 2, jnp.where(group == 2, 4, 8)))
    t = j * tm + lax.broadcasted_iota(jnp.int32, (tm, POOL_WIDTH), 0)
    cnt = jnp.minimum(t + half, seq_len) - jnp.maximum(t - half, 0)
    pin = (wsum / cnt.astype(F32) - p).astype(BF16)

    gavg = gavg_ref[...]
    head = lax.shift_right_logical(lax.broadcasted_iota(jnp.int32, (SGU_CHUNK, SGU_WIDTH), 1),
                                   (SGU_WIDTH // SGU_HEADS).bit_length() - 1)
    split = POOL_WIDTH + SGU_WIDTH

    for r in range(tm // KV_TILE):
        sub = slice(r * KV_TILE, (r + 1) * KV_TILE)
        pool = _dot(pin[sub], poolw_ref[...]) * pscale_ref[...]

        sgu_parts = []
        for c in range(KV_TILE // SGU_CHUNK):
            rows = slice(r * KV_TILE + c * SGU_CHUNK, r * KV_TILE + (c + 1) * SGU_CHUNK)
            u = _gelu_tanh(z[rows, POOL_WIDTH:POOL_WIDTH + SGU_WIDTH])
            v = _gelu_tanh(z[rows, POOL_WIDTH + SGU_WIDTH:])
            mu = _dot_exact_rhs(v, gavg)
            vc = v - mu
            var = _dot_exact_rhs(vc * vc, gavg)
            vn = vc * lax.rsqrt(var + LN_EPS) * sgug_ref[...]
            stacked = jnp.concatenate(
                [jnp.where(head == hd, vn, 0.0).astype(BF16) for hd in range(SGU_HEADS)], axis=0)
            sgu_parts.append(u * (_dot(sguw_ref[...], stacked) + sgub_ref[...]))
        sgu = jnp.concatenate(sgu_parts, axis=0)

        ps = jnp.concatenate([pool, sgu], axis=1).astype(BF16)
        y = _dot(ps, wout_ref[:split, :]) + _dot(attn_ref[0, sub, :], wout_ref[split:, :])
        xn = _layer_norm(alpha * x_ref[0, sub, :] + g1_ref[0] * y) * lng_ref[...] + lnb_ref[...]
        xo_ref[0, sub, :] = xn

        h2 = _layer_norm(xn) * (1.0 + sc2_ref[0]) + sh2_ref[0]
        h2_ref[0, sub, :] = h2.astype(BF16)
        logits_t = _dot3(h2, wr_hi_ref[...], wr_lo_ref[...]).T
        for c in range(KV_TILE // TOKEN_BLOCK):
            logit_ref[0, r * (KV_TILE // TOKEN_BLOCK) + c] = (
                logits_t[:N_EXPERTS, c * TOKEN_BLOCK:(c + 1) * TOKEN_BLOCK])


def _mixer_output(zpu, attn, x, g1, sh2, sc2, lw, alpha):
    b, l, d = x.shape
    tm = min(ROW_TILE, l)
    nt = l // tm
    hb = tm // POOL_HALO
    n_halo = l // POOL_HALO
    vec = pl.BlockSpec((1, 1, d), lambda i, j: (i, 0, 0))

    def full(a):
        nd = a.ndim
        return pl.BlockSpec(a.shape, lambda i, j: (0,) * nd)

    weights = [lw["pool_w"], lw["pool_scale"], lw["gavg256"], lw["sgu_g"], lw["sgu_w"], lw["sgu_b"],
               lw["w_out"], lw["ln1_g"], lw["ln1_b"], lw["wr_hi"], lw["wr_lo"]]
    return pl.pallas_call(
        functools.partial(_mix_kernel, seq_len=l, alpha=alpha),
        grid=(b, nt),
        in_specs=[
            pl.BlockSpec((1, POOL_HALO, POOL_WIDTH), lambda i, j: (i, jnp.maximum(j * hb - 1, 0), 0)),
            pl.BlockSpec((1, POOL_HALO, POOL_WIDTH), lambda i, j: (i, jnp.minimum((j + 1) * hb, n_halo - 1), 0)),
            pl.BlockSpec((1, tm, PU_WIDTH), lambda i, j: (i, j, 0)),
            pl.BlockSpec((1, tm, ATTN_WIDTH), lambda i, j: (i, j, 0)),
            pl.BlockSpec((1, tm, d), lambda i, j: (i, j, 0)),
            vec, vec, vec,
        ] + [full(w) for w in weights],
        out_specs=[
            pl.BlockSpec((1, tm, d), lambda i, j: (i, j, 0)),
            pl.BlockSpec((1, tm, d), lambda i, j: (i, j, 0)),
            pl.BlockSpec((1, tm // TOKEN_BLOCK, N_EXPERTS, TOKEN_BLOCK), lambda i, j: (i, j, 0, 0)),
        ],
        out_shape=[
            jax.ShapeDtypeStruct((b, l, d), F32),
            jax.ShapeDtypeStruct((b, l, d), BF16),
            jax.ShapeDtypeStruct((b, l // TOKEN_BLOCK, N_EXPERTS, TOKEN_BLOCK), F32),
        ],
        compiler_params=_params("arbitrary", "arbitrary"),
        name="mixer_output",
    )(zpu, zpu, zpu, attn, x, g1, sh2, sc2, *weights)


def _route_kernel(logit_ref, tri_ref, rank_ref, gate_ref, start_ref, bits_ref, sel_ref, *, cap):
    nb = logit_ref.shape[1]
    x = logit_ref[0]
    e = jnp.exp(x - jnp.max(x, axis=1, keepdims=True))
    aff = e / jnp.sum(e, axis=1, keepdims=True)
    gate_ref[0] = aff
    bits_ref[...] = pltpu.bitcast(aff, jnp.int32)

    def count(mask):
        return jnp.sum(jnp.sum(mask.astype(F32), axis=0), axis=1, keepdims=True)

    def bisect(i, thr):
        cand = thr | lax.shift_left(jnp.int32(1), 30 - i)
        return jnp.where(count(bits_ref[...] >= cand) >= float(cap), cand, thr)

    thr = lax.fori_loop(0, 31, bisect, jnp.zeros((N_EXPERTS, 1), jnp.int32))
    need = float(cap) - count(bits_ref[...] > thr)
    tri = tri_ref[...]
    zero = jnp.zeros((N_EXPERTS, 1), F32)

    def pass_ties(j, carry):
        bits = bits_ref[j]
        eq = (bits == thr).astype(F32)
        cum = _dot(eq.astype(BF16), tri) + carry
        sel_ref[j] = jnp.where((bits > thr) | ((eq > 0.0) & (cum <= need)), 1.0, 0.0)
        return carry + jnp.sum(eq, axis=1, keepdims=True)

    lax.fori_loop(0, nb, pass_ties, zero)
    lane = lax.broadcasted_iota(jnp.int32, (N_EXPERTS, LANES), 1)

    start_ref[0] = jnp.zeros((N_EXPERTS, LANES), jnp.int32)

    def pass_rank(j, before):
        sel = sel_ref[j]
        cum = _dot(sel.astype(BF16), tri) + before
        rank_ref[0, j] = jnp.where(sel > 0.0, cum - 1.0, -1.0)
        start_ref[0] = jnp.where(lane == j, before.astype(jnp.int32), start_ref[0])
        return before + jnp.sum(sel, axis=1, keepdims=True)

    lax.fori_loop(0, nb, pass_rank, zero)


def _route(logits_t, tri, cap):
    b, nb = logits_t.shape[:2]
    assert nb <= LANES
    blocks = pl.BlockSpec((1, nb, N_EXPERTS, TOKEN_BLOCK), lambda i: (i, 0, 0, 0))
    table = jax.ShapeDtypeStruct((b, nb, N_EXPERTS, TOKEN_BLOCK), F32)
    return pl.pallas_call(
        functools.partial(_route_kernel, cap=cap),
        grid=(b,),
        in_specs=[blocks, pl.BlockSpec((TOKEN_BLOCK, TOKEN_BLOCK), lambda i: (0, 0))],
        out_specs=[blocks, blocks, pl.BlockSpec((1, N_EXPERTS, LANES), lambda i: (i, 0, 0))],
        out_shape=[
            table,
            table,
            jax.ShapeDtypeStruct((b, N_EXPERTS, LANES), jnp.int32),
        ],
        scratch_shapes=[pltpu.VMEM((nb, N_EXPERTS, TOKEN_BLOCK), jnp.int32),
                        pltpu.VMEM((nb, N_EXPERTS, TOKEN_BLOCK), F32)],
        compiler_params=_params("arbitrary"),
        name="routing",
    )(logits_t, tri)


def _window_start(s0, cap, window):
    return pl.multiple_of(jnp.minimum(s0 & ~(BF16_ROWS - 1), cap - window), BF16_ROWS)


def _block_fits(starts_ref, idx, is_last, cap, window):
    s0 = starts_ref[idx]
    s1 = jnp.where(is_last, cap, starts_ref[jnp.where(is_last, idx, idx + 1)])
    return (s1 - _window_start(s0, cap, window)) <= window


def _ffn_kernel(starts_ref, h_ref, rank_t_ref, w1_ref, w3_ref, w2_ref, y_ref, xg_ref,
                *, cap, window, small_window):
    e = pl.program_id(1)
    group, nb = rank_t_ref.shape[0], rank_t_ref.shape[1]
    first = pl.program_id(0) * group
    bases = [((first + i) * N_EXPERTS + e) * nb for i in range(group)]
    xg_ref[...] = jnp.zeros(xg_ref.shape, F32)

    def gather_all(win):
        slot = lax.broadcasted_iota(jnp.int32, (win, TOKEN_BLOCK), 0)
        for i in range(group):
            def gather(jb, carry, i=i):
                a = _window_start(starts_ref[bases[i] + jb], cap, win)
                rank = rank_t_ref[i, jb, pl.ds(e, 1), :]
                onehot = ((slot + a).astype(F32) == rank).astype(BF16)
                rows = pl.ds(pl.multiple_of(jb * TOKEN_BLOCK, TOKEN_BLOCK), TOKEN_BLOCK)
                xg_ref[pl.ds(i * cap + a, win), :] += _dot(onehot, h_ref[i, rows, :])
                return carry

            lax.fori_loop(0, nb, gather, 0, unroll=min(nb, GATHER_UNROLL))

    if small_window < window:
        fits = True
        for i in range(group):
            fits = lax.fori_loop(
                0, nb,
                lambda jb, ok, i=i: ok & _block_fits(starts_ref, bases[i] + jb, jb == nb - 1, cap, small_window),
                fits)
        pl.when(fits)(lambda: gather_all(small_window))
        pl.when(jnp.logical_not(fits))(lambda: gather_all(window))
    else:
        gather_all(window)

    xg = xg_ref[...].astype(BF16)
    a = _dot(xg, w1_ref[0])
    hid = (a * _sigmoid(a)) * _dot(xg, w3_ref[0])
    y = _dot(hid.astype(BF16), w2_ref[0]).astype(BF16)
    for i in range(group):
        y_ref[i, 0] = y[i * cap:(i + 1) * cap]


def _expert_ffn(starts, h2, rank_t, w1, w3, w2, cap):
    b, l, d = h2.shape
    f = w1.shape[2]
    nb = l // TOKEN_BLOCK
    window = min(cap, TOKEN_BLOCK + BF16_ROWS)
    small_window = min(cap, SMALL_WINDOW)
    group = max(1, min(b, FFN_ROWS // cap))
    grid_spec = pltpu.PrefetchScalarGridSpec(
        num_scalar_prefetch=1,
        grid=(b // group, N_EXPERTS),
        in_specs=[
            pl.BlockSpec((group, l, d), lambda i, j, s: (i, 0, 0)),
            pl.BlockSpec((group, nb, N_EXPERTS, TOKEN_BLOCK), lambda i, j, s: (i, 0, 0, 0)),
            pl.BlockSpec((1, d, f), lambda i, j, s: (j, 0, 0)),
            pl.BlockSpec((1, d, f), lambda i, j, s: (j, 0, 0)),
            pl.BlockSpec((1, f, d), lambda i, j, s: (j, 0, 0)),
        ],
        out_specs=pl.BlockSpec((group, 1, cap, d), lambda i, j, s: (i, j, 0, 0)),
        scratch_shapes=[pltpu.VMEM((group * cap, d), F32)],
    )
    return pl.pallas_call(
        functools.partial(_ffn_kernel, cap=cap, window=window, small_window=small_window),
        grid_spec=grid_spec,
        out_shape=jax.ShapeDtypeStruct((b, N_EXPERTS, cap, d), BF16),
        compiler_params=_params("arbitrary", "arbitrary"),
        name="expert_ffn",
    )(starts, h2, rank_t, w1, w3, w2)


def _combine_kernel(starts_ref, y_ref, rank_t_ref, gate_t_ref, x_ref, g2_ref, lng_ref, lnb_ref,
                    xo_ref, ml_ref, ystack_ref, *, cap, window, small_window, alpha):
    b = pl.program_id(0)
    jb = pl.program_id(1)
    nb = pl.num_programs(1)
    table = [(b * N_EXPERTS + e) * nb + jb for e in range(N_EXPERTS)]

    def stacked(win):
        slot = lax.broadcasted_iota(jnp.int32, (win, TOKEN_BLOCK), 0)
        hi, lo = [], []
        for e in range(N_EXPERTS):
            a = _window_start(starts_ref[table[e]], cap, win)
            ystack_ref[e * win:(e + 1) * win, :] = y_ref[0, e, pl.ds(a, win), :]
            hit = (slot + a).astype(F32) == rank_t_ref[0, 0, e:e + 1, :]
            g_hi, g_lo = _split(jnp.where(hit, gate_t_ref[0, 0, e:e + 1, :], 0.0))
            hi.append(g_hi)
            lo.append(g_lo)
        ystack = ystack_ref[:N_EXPERTS * win, :]
        contract_rows = (((0,), (0,)), ((), ()))
        ml_ref[...] = (
            lax.dot_general(jnp.concatenate(hi, axis=0), ystack, contract_rows, preferred_element_type=F32)
            + lax.dot_general(jnp.concatenate(lo, axis=0), ystack, contract_rows, preferred_element_type=F32))

    if small_window < window:
        fits = _block_fits(starts_ref, table[0], jb == nb - 1, cap, small_window)
        for e in range(1, N_EXPERTS):
            fits = fits & _block_fits(starts_ref, table[e], jb == nb - 1, cap, small_window)
        pl.when(fits)(lambda: stacked(small_window))
        pl.when(jnp.logical_not(fits))(lambda: stacked(window))
    else:
        stacked(window)
    xo_ref[0] = _layer_norm(alpha * x_ref[0] + g2_ref[0] * ml_ref[...]) * lng_ref[...] + lnb_ref[...]


def _combine(starts, y, rank_t, gate_t, x, g2, ln_g, ln_b, cap, alpha):
    b, l, d = x.shape
    nb = l // TOKEN_BLOCK
    window = min(cap, TOKEN_BLOCK + BF16_ROWS)
    small_window = min(cap, SMALL_WINDOW)
    grid_spec = pltpu.PrefetchScalarGridSpec(
        num_scalar_prefetch=1,
        grid=(b, nb),
        in_specs=[
            pl.BlockSpec((1, N_EXPERTS, cap, d), lambda i, j, s: (i, 0, 0, 0)),
            pl.BlockSpec((1, 1, N_EXPERTS, TOKEN_BLOCK), lambda i, j, s: (i, j, 0, 0)),
            pl.BlockSpec((1, 1, N_EXPERTS, TOKEN_BLOCK), lambda i, j, s: (i, j, 0, 0)),
            pl.BlockSpec((1, TOKEN_BLOCK, d), lambda i, j, s: (i, j, 0)),
            pl.BlockSpec((1, 1, d), lambda i, j, s: (i, 0, 0)),
            pl.BlockSpec((1, d), lambda i, j, s: (0, 0)),
            pl.BlockSpec((1, d), lambda i, j, s: (0, 0)),
        ],
        out_specs=pl.BlockSpec((1, TOKEN_BLOCK, d), lambda i, j, s: (i, j, 0)),
        scratch_shapes=[
            pltpu.VMEM((TOKEN_BLOCK, d), F32),
            pltpu.VMEM((N_EXPERTS * window, d), BF16),
        ],
    )
    return pl.pallas_call(
        functools.partial(_combine_kernel, cap=cap, window=window, small_window=small_window, alpha=alpha),
        grid_spec=grid_spec,
        out_shape=jax.ShapeDtypeStruct((b, l, d), F32),
        compiler_params=_params("arbitrary", "arbitrary"),
        name="moe_combine",
    )(starts, y, rank_t, gate_t, x, g2, ln_g, ln_b)


def _group_mean_matrix(n, group):
    idx = jnp.arange(n) // group
    return (idx[:, None] == idx[None, :]).astype(F32) / group


def _rope_tables(n):
    rows = n // GRID_W
    r = jnp.repeat(jnp.arange(rows, dtype=F32), GRID_W)
    col = jnp.tile(jnp.arange(GRID_W, dtype=F32), rows)
    inv = ROPE_THETA ** (-jnp.arange(ROPE_AXIS_FREQS, dtype=F32) / ROPE_AXIS_FREQS)
    ang = jnp.concatenate([r[:, None] * inv, col[:, None] * inv], axis=-1)
    cos, sin = jnp.cos(ang), jnp.sin(ang)
    cos_t = jnp.tile(cos, (1, LANES // (HEAD_DIM // 2)))
    sin_t = jnp.tile(jnp.concatenate([-sin, sin], axis=-1), (1, LANES // HEAD_DIM))
    return cos_t, sin_t


def _layer_weights(l, w_in, pool_w, pool_scale, sgu_g, sgu_w, sgu_b, q_g, k_g, w_out, ln1_g, ln1_b,
                   w_router, w1, w3, w2, ln2_g, ln2_b):
    d = w_in.shape[1]
    q_scale = HEAD_DIM ** -0.5 * math.log2(math.e)
    gain = jnp.concatenate([jnp.tile(q_g[l] * q_scale, N_HEADS), jnp.tile(k_g[l], N_KV_HEADS)])[None, :]
    pool_bd = jax.scipy.linalg.block_diag(*[pool_w[l, g] for g in range(len(POOL_WINDOWS))])
    wr = jnp.pad(w_router[l], ((0, 0), (0, LANES - N_EXPERTS)))
    wr_hi = wr.astype(BF16)
    return dict(
        w_in=w_in[l].astype(BF16),
        gain=gain,
        pool_w=pool_bd.astype(BF16),
        pool_scale=pool_scale[l][None, :],
        gavg256=_group_mean_matrix(SGU_WIDTH, SGU_WIDTH // SGU_HEADS).astype(BF16),
        sgu_g=sgu_g[l].reshape(1, SGU_WIDTH),
        sgu_w=jnp.swapaxes(sgu_w[l], 0, 1).reshape(SGU_CHUNK, SGU_HEADS * SGU_CHUNK).astype(BF16),
        sgu_b=jnp.repeat(sgu_b[l].T, SGU_WIDTH // SGU_HEADS, axis=1),
        w_out=w_out[l].astype(BF16),
        ln1_g=ln1_g[l][None, :], ln1_b=ln1_b[l][None, :],
        wr_hi=wr_hi, wr_lo=(wr - wr_hi.astype(F32)).astype(BF16),
        w1=w1[l].astype(BF16), w3=w3[l].astype(BF16), w2=w2[l].astype(BF16),
        ln2_g=ln2_g[l][None, :], ln2_b=ln2_b[l][None, :],
    )


def _moe(x, h2, logits, g2, lw, tri, alpha):
    b, l, d = x.shape
    cap = EC_CAPACITY_FACTOR * l // N_EXPERTS
    rank_t, gate_t, starts = _route(logits, tri, cap)
    starts = starts[:, :, :l // TOKEN_BLOCK].reshape(-1)
    y = _expert_ffn(starts, h2, rank_t, lw["w1"], lw["w3"], lw["w2"], cap)
    return _combine(starts, y, rank_t, gate_t, x, g2, lw["ln2_g"], lw["ln2_b"], cap, alpha)


def kernel(x, c, ctx, c_ctx, w_mod, b_mod, w_in, pool_w, pool_scale, sgu_g, sgu_w, sgu_b, q_g, k_g, w_out,
           ln1_g, ln1_b, w_router, w1, w3, w2, ln2_g, ln2_b):
    batch, seq, d = x.shape
    ctx_len = ctx.shape[1]
    depth = w_mod.shape[0]
    alpha = (2 * depth) ** 0.25

    cond = jnp.concatenate([c, c_ctx[None, :], jnp.zeros((COND_ROWS - batch - 1, d), F32)], axis=0)
    mod = _modulation(cond, w_mod, b_mod)

    cos, sin = _rope_tables(seq)
    cos_c = jnp.ones((ctx_len, LANES), F32)
    sin_c = jnp.zeros((ctx_len, LANES), F32)
    gavg128 = _group_mean_matrix(LANES, HEAD_DIM).astype(BF16)
    tri = jnp.triu(jnp.ones((TOKEN_BLOCK, TOKEN_BLOCK), F32)).astype(BF16)

    xc = ctx
    for l in range(depth):
        last = l == depth - 1
        lw = _layer_weights(l, w_in, pool_w, pool_scale, sgu_g, sgu_w, sgu_b, q_g, k_g, w_out, ln1_g, ln1_b,
                            w_router, w1, w3, w2, ln2_g, ln2_b)
        m = mod[l].reshape(COND_ROWS, 6, d)
        sh1, sc1, g1, sh2, sc2, g2 = [m[:batch, i][:, None, :] for i in range(6)]
        csh1, csc1, cg1, csh2, csc2, cg2 = [jnp.broadcast_to(m[batch, i][None, None, :], (batch, 1, d))
                                            for i in range(6)]

        zpu, qt, kl, vtl = _in_projection(x, sh1, sc1, lw["w_in"], lw["gain"], cos, sin, gavg128)
        zpu_c, qt_c, kc, vtc = _in_projection(xc, csh1, csc1, lw["w_in"], lw["gain"], cos_c, sin_c, gavg128)

        attn = _attention(qt, jnp.concatenate([kl, kc], axis=1), jnp.concatenate([vtl, vtc], axis=2))
        if not last:
            attn_c = _attention(qt_c, kc, vtc)
            xc, h2c, logits_c = _mixer_output(zpu_c, attn_c, xc, cg1, csh2, csc2, lw, alpha)
            xc = _moe(xc, h2c, logits_c, cg2, lw, tri, alpha)

        x, h2, logits = _mixer_output(zpu, attn, x, g1, sh2, sc2, lw, alpha)
        x = _moe(x, h2, logits, g2, lw, tri, alpha)
    return x
```

```python
import functools
import math

import jax
import jax.numpy as jnp
from jax import lax
from jax.experimental import pallas as pl
from jax.experimental.pallas import tpu as pltpu

F32 = jnp.float32
BF16 = jnp.bfloat16

GRID_W = 64
POOL_WIDTH = 256
POOL_WINDOWS = (2, 4, 8, 16)
POOL_GROUP = 64
POOL_HALO = 8
SGU_WIDTH = 256
SGU_HEADS = 4
SGU_CHUNK = 128
HEAD_DIM = 64
N_HEADS = 8
N_KV_HEADS = 2
KV_GROUP = N_HEADS // N_KV_HEADS
ATTN_WIDTH = N_HEADS * HEAD_DIM
KV_WIDTH = N_KV_HEADS * HEAD_DIM
PU_WIDTH = POOL_WIDTH + 2 * SGU_WIDTH
QK_WIDTH = ATTN_WIDTH + KV_WIDTH
ROPE_THETA = 10000.0
ROPE_AXIS_FREQS = HEAD_DIM // 4
N_EXPERTS = 16
EC_CAPACITY_FACTOR = 2
LN_EPS = 1e-6

LANES = 128
SUBLANES = 8
BF16_ROWS = 16
V_EXT_ROWS = HEAD_DIM + BF16_ROWS
VMEM_LIMIT_BYTES = 56 * 1024 * 1024

ROW_TILE = 512
KV_TILE = 256
ATTN_Q_TILE = 512
TOKEN_BLOCK = 128
GATHER_UNROLL = 4
FFN_ROWS = 512
SMALL_WINDOW = 48
COND_ROWS = 16


def _params(*sem):
    return pltpu.CompilerParams(dimension_semantics=sem, vmem_limit_bytes=VMEM_LIMIT_BYTES)


def _split(x):
    hi = x.astype(BF16)
    lo = (x - hi.astype(F32)).astype(BF16)
    return hi, lo


def _dot(a, b):
    return jnp.dot(a, b, preferred_element_type=F32)


def _dot3(a, b_hi, b_lo):
    a_hi, a_lo = _split(a)
    return _dot(a_hi, b_hi) + _dot(a_hi, b_lo) + _dot(a_lo, b_hi)


def _dot_exact_rhs(a, b):
    a_hi, a_lo = _split(a)
    return _dot(a_hi, b) + _dot(a_lo, b)


def _layer_norm(x):
    mu = jnp.mean(x, axis=-1, keepdims=True)
    xc = x - mu
    var = jnp.mean(xc * xc, axis=-1, keepdims=True)
    return xc * lax.rsqrt(var + LN_EPS)


def _sigmoid(x):
    return 1.0 / (1.0 + jnp.exp(-x))


def _gelu_tanh(x):
    c = math.sqrt(2.0 / math.pi)
    return 0.5 * x * (1.0 + jnp.tanh(c * (x + 0.044715 * (x * x * x))))


def _mod_kernel(cond_ref, w_ref, b_ref, o_ref):
    a = cond_ref[...]
    a = a * _sigmoid(a)
    w_hi, w_lo = _split(w_ref[0])
    o_ref[0] = _dot3(a, w_hi, w_lo) + b_ref[0]


def _modulation(cond, w_mod, b_mod):
    depth, d, n = w_mod.shape
    tn = n // 4
    return pl.pallas_call(
        _mod_kernel,
        grid=(depth, n // tn),
        in_specs=[
            pl.BlockSpec((COND_ROWS, d), lambda l, j: (0, 0)),
            pl.BlockSpec((1, d, tn), lambda l, j: (l, 0, j)),
            pl.BlockSpec((1, 1, tn), lambda l, j: (l, 0, j)),
        ],
        out_specs=pl.BlockSpec((1, COND_ROWS, tn), lambda l, j: (l, 0, j)),
        out_shape=jax.ShapeDtypeStruct((depth, COND_ROWS, n), F32),
        compiler_params=_params("arbitrary", "arbitrary"),
        name="modulation",
    )(cond, w_mod, b_mod.reshape(depth, 1, n))


def _in_kernel(x_ref, sh_ref, sc_ref, w_ref, gain_ref, cos_ref, sin_ref, gavg_ref,
               zpu_ref, qt_ref, k_ref, vt_ref):
    gavg = gavg_ref[...]
    lane = lax.broadcasted_iota(jnp.int32, (KV_TILE, LANES), 1)
    first_half = (lane & (HEAD_DIM - 1)) < (HEAD_DIM // 2)
    row = lax.broadcasted_iota(jnp.int32, (V_EXT_ROWS - HEAD_DIM, KV_TILE), 0)
    ones_row = (row == 0).astype(F32)

    for r in range(x_ref.shape[1] // KV_TILE):
        rows = slice(r * KV_TILE, (r + 1) * KV_TILE)
        h = _layer_norm(x_ref[0, rows, :]) * (1.0 + sc_ref[0]) + sh_ref[0]
        z = _dot(h.astype(BF16), w_ref[...])
        zpu_ref[0, rows, :] = z[:, :PU_WIDTH]

        cos = cos_ref[rows, :]
        sin = sin_ref[rows, :]
        pieces = []
        for c in range(QK_WIDTH // LANES):
            lo = PU_WIDTH + c * LANES
            y = z[:, lo:lo + LANES]
            ms = _dot_exact_rhs(y * y, gavg)
            y = y * lax.rsqrt(ms + LN_EPS) * gain_ref[:, c * LANES:(c + 1) * LANES]
            partner = jnp.where(first_half,
                                pltpu.roll(y, LANES - HEAD_DIM // 2, 1),
                                pltpu.roll(y, HEAD_DIM // 2, 1))
            pieces.append(y * cos + partner * sin)
        q = jnp.concatenate(pieces[:ATTN_WIDTH // LANES], axis=1)
        qt_ref[0, :, rows] = q.T.astype(BF16)
        k_ref[0, rows, :] = pieces[-1].astype(BF16)

        vt = z[:, PU_WIDTH + QK_WIDTH:].T
        for g in range(N_KV_HEADS):
            ext = jnp.concatenate([vt[g * HEAD_DIM:(g + 1) * HEAD_DIM], ones_row], axis=0)
            vt_ref[0, g, r] = ext.astype(BF16)


def _in_projection(x, sh, sc, w_in, gain, cos, sin, gavg):
    b, l, d = x.shape
    n = w_in.shape[1]
    tm = min(ROW_TILE, l)
    nt = l // tm
    sub = tm // KV_TILE
    vec = pl.BlockSpec((1, 1, d), lambda i, j: (i, 0, 0))
    return pl.pallas_call(
        _in_kernel,
        grid=(b, nt),
        in_specs=[
            pl.BlockSpec((1, tm, d), lambda i, j: (i, j, 0)),
            vec, vec,
            pl.BlockSpec((d, n), lambda i, j: (0, 0)),
            pl.BlockSpec((1, QK_WIDTH), lambda i, j: (0, 0)),
            pl.BlockSpec((tm, LANES), lambda i, j: (j, 0)),
            pl.BlockSpec((tm, LANES), lambda i, j: (j, 0)),
            pl.BlockSpec((LANES, LANES), lambda i, j: (0, 0)),
        ],
        out_specs=[
            pl.BlockSpec((1, tm, PU_WIDTH), lambda i, j: (i, j, 0)),
            pl.BlockSpec((1, ATTN_WIDTH, tm), lambda i, j: (i, 0, j)),
            pl.BlockSpec((1, tm, KV_WIDTH), lambda i, j: (i, j, 0)),
            pl.BlockSpec((1, N_KV_HEADS, sub, V_EXT_ROWS, KV_TILE), lambda i, j: (i, 0, j, 0, 0)),
        ],
        out_shape=[
            jax.ShapeDtypeStruct((b, l, PU_WIDTH), F32),
            jax.ShapeDtypeStruct((b, ATTN_WIDTH, l), BF16),
            jax.ShapeDtypeStruct((b, l, KV_WIDTH), BF16),
            jax.ShapeDtypeStruct((b, N_KV_HEADS, l // KV_TILE, V_EXT_ROWS, KV_TILE), BF16),
        ],
        compiler_params=_params("arbitrary", "arbitrary"),
        name="in_projection",
    )(x, sh, sc, w_in, gain, cos, sin, gavg)


def _attn_kernel(qt_ref, k_ref, vt_ref, o_ref, qpad_ref, m_ref, acc_ref, p_ref, scale_ref, *, n_kv):
    tq = qt_ref.shape[2]
    zeros = jnp.zeros((HEAD_DIM, tq), BF16)
    for h in range(N_HEADS):
        qh = qt_ref[0, h * HEAD_DIM:(h + 1) * HEAD_DIM, :]
        qpad_ref[h] = jnp.concatenate([qh, zeros] if h // KV_GROUP == 0 else [zeros, qh], axis=0)
    m_ref[...] = jnp.full(m_ref.shape, -1e30, F32)
    acc_ref[...] = jnp.zeros(acc_ref.shape, F32)

    def scores(c):
        start = pl.multiple_of(c * KV_TILE, KV_TILE)
        kc = k_ref[0, pl.ds(start, KV_TILE), :]
        return [_dot(kc, qpad_ref[h]) for h in range(N_HEADS)]

    def softmax(c, s):
        slot = c & 1
        for h in range(N_HEADS):
            m = m_ref[h]
            m_new = jnp.maximum(m, jnp.max(s[h], axis=0, keepdims=True))
            p_ref[slot, h] = jnp.exp2(s[h] - m_new).astype(BF16)
            scale_ref[slot, h] = jnp.exp2(m - m_new)
            m_ref[h] = m_new

    def values(c):
        slot = c & 1
        for h in range(N_HEADS):
            acc_ref[h] = acc_ref[h] * scale_ref[slot, h] + _dot(vt_ref[0, h // KV_GROUP, c], p_ref[slot, h])

    softmax(0, scores(0))

    def body(c, carry):
        s = scores(c)
        values(c - 1)
        softmax(c, s)
        return carry

    lax.fori_loop(1, n_kv, body, 0)
    values(n_kv - 1)
    blocks = []
    for hp in range(N_HEADS // 2):
        pair = []
        for h in (2 * hp, 2 * hp + 1):
            acc = acc_ref[h]
            pair.append(acc[:HEAD_DIM] / acc[HEAD_DIM:HEAD_DIM + 1])
        blocks.append(jnp.concatenate(pair, axis=0).T)
    o_ref[0] = jnp.concatenate(blocks, axis=1).astype(BF16)


def _attention(qt, k, vt):
    b, _, lq = qt.shape
    lk = k.shape[1]
    n_kv = lk // KV_TILE
    tq = min(ATTN_Q_TILE, lq)
    return pl.pallas_call(
        functools.partial(_attn_kernel, n_kv=n_kv),
        grid=(b, lq // tq),
        in_specs=[
            pl.BlockSpec((1, ATTN_WIDTH, tq), lambda i, j: (i, 0, j)),
            pl.BlockSpec((1, lk, KV_WIDTH), lambda i, j: (i, 0, 0)),
            pl.BlockSpec((1, N_KV_HEADS, n_kv, V_EXT_ROWS, KV_TILE), lambda i, j: (i, 0, 0, 0, 0)),
        ],
        out_specs=pl.BlockSpec((1, tq, ATTN_WIDTH), lambda i, j: (i, j, 0)),
        out_shape=jax.ShapeDtypeStruct((b, lq, ATTN_WIDTH), BF16),
        scratch_shapes=[
            pltpu.VMEM((N_HEADS, 2 * HEAD_DIM, tq), BF16),
            pltpu.VMEM((N_HEADS, 1, tq), F32),
            pltpu.VMEM((N_HEADS, V_EXT_ROWS, tq), F32),
            pltpu.VMEM((2, N_HEADS, KV_TILE, tq), BF16),
            pltpu.VMEM((2, N_HEADS, 1, tq), F32),
        ],
        compiler_params=_params("arbitrary", "arbitrary"),
        name="attention",
    )(qt, k, vt)


def _mix_kernel(zp_ref, zn_ref, z_ref, attn_ref, x_ref, g1_ref, sh2_ref, sc2_ref,
                poolw_ref, pscale_ref, gavg_ref, sgug_ref, sguw_ref, sgub_ref,
                wout_ref, lng_ref, lnb_ref, wr_hi_ref, wr_lo_ref,
                xo_ref, h2_ref, logit_ref, *, seq_len, alpha):
    j = pl.program_id(1)
    nt = pl.num_programs(1)
    z = z_ref[0]
    tm = z.shape[0]
    lane = lax.broadcasted_iota(jnp.int32, (tm, POOL_WIDTH), 1)
    group = lax.shift_right_logical(lane, POOL_GROUP.bit_length() - 1)

    p = z[:, :POOL_WIDTH]
    prev = zp_ref[0] * (j > 0).astype(F32)
    nxt = zn_ref[0] * (j < nt - 1).astype(F32)
    ext = jnp.concatenate([prev, p, nxt], axis=0)
    n_ext = ext.shape[0]
    s2 = ext + pltpu.roll(ext, 1, 0)
    s4 = pltpu.roll(s2, 1, 0) + pltpu.roll(s2, n_ext - 1, 0)
    s8 = pltpu.roll(s4, 2, 0) + pltpu.roll(s4, n_ext - 2, 0)
    s16 = pltpu.roll(s8, 4, 0) + pltpu.roll(s8, n_ext - 4, 0)
    sums = [s[POOL_HALO:POOL_HALO + tm] for s in (s2, s4, s8, s16)]
    wsum = jnp.where(group == 0, sums[0], jnp.where(group == 1, sums[1], jnp.where(group == 2, sums[2], sums[3])))
    half = jnp.where(group == 0, 1, jnp.where(group == 1, 2, jnp.where(group == 2, 4, 8)))
    t = j * tm + lax.broadcasted_iota(jnp.int32, (tm, POOL_WIDTH), 0)
    cnt = jnp.minimum(t + half, seq_len) - jnp.maximum(t - half, 0)
    pin = (wsum / cnt.astype(F32) - p).astype(BF16)

    gavg = gavg_ref[...]
    head = lax.shift_right_logical(lax.broadcasted_iota(jnp.int32, (SGU_CHUNK, SGU_WIDTH), 1),
                                   (SGU_WIDTH // SGU_HEADS).bit_length() - 1)
    split = POOL_WIDTH + SGU_WIDTH

    for r in range(tm // KV_TILE):
        sub = slice(r * KV_TILE, (r + 1) * KV_TILE)
        pool = _dot(pin[sub], poolw_ref[...]) * pscale_ref[...]

        sgu_parts = []
        for c in range(KV_TILE // SGU_CHUNK):
            rows = slice(r * KV_TILE + c * SGU_CHUNK, r * KV_TILE + (c + 1) * SGU_CHUNK)
            u = _gelu_tanh(z[rows, POOL_WIDTH:POOL_WIDTH + SGU_WIDTH])
            v = _gelu_tanh(z[rows, POOL_WIDTH + SGU_WIDTH:])
            mu = _dot_exact_rhs(v, gavg)
            vc = v - mu
            var = _dot_exact_rhs(vc * vc, gavg)
            vn = vc * lax.rsqrt(var + LN_EPS) * sgug_ref[...]
            stacked = jnp.concatenate(
                [jnp.where(head == hd, vn, 0.0).astype(BF16) for hd in range(SGU_HEADS)], axis=0)
            sgu_parts.append(u * (_dot(sguw_ref[...], stacked) + sgub_ref[...]))
        sgu = jnp.concatenate(sgu_parts, axis=0)

        ps = jnp.concatenate([pool, sgu], axis=1).astype(BF16)
        y = _dot(ps, wout_ref[:split, :]) + _dot(attn_ref[0, sub, :], wout_ref[split:, :])
        xn = _layer_norm(alpha * x_ref[0, sub, :] + g1_ref[0] * y) * lng_ref[...] + lnb_ref[...]
        xo_ref[0, sub, :] = xn

        h2 = _layer_norm(xn) * (1.0 + sc2_ref[0]) + sh2_ref[0]
        h2_ref[0, sub, :] = h2.astype(BF16)
        logits_t = _dot3(h2, wr_hi_ref[...], wr_lo_ref[...]).T
        for c in range(KV_TILE // TOKEN_BLOCK):
            logit_ref[0, r * (KV_TILE // TOKEN_BLOCK) + c] = (
                logits_t[:N_EXPERTS, c * TOKEN_BLOCK:(c + 1) * TOKEN_BLOCK])


def _mixer_output(zpu, attn, x, g1, sh2, sc2, lw, alpha):
    b, l, d = x.shape
    tm = min(ROW_TILE, l)
    nt = l // tm
    hb = tm // POOL_HALO
    n_halo = l // POOL_HALO
    vec = pl.BlockSpec((1, 1, d), lambda i, j: (i, 0, 0))

    def full(a):
        nd = a.ndim
        return pl.BlockSpec(a.shape, lambda i, j: (0,) * nd)

    weights = [lw["pool_w"], lw["pool_scale"], lw["gavg256"], lw["sgu_g"], lw["sgu_w"], lw["sgu_b"],
               lw["w_out"], lw["ln1_g"], lw["ln1_b"], lw["wr_hi"], lw["wr_lo"]]
    return pl.pallas_call(
        functools.partial(_mix_kernel, seq_len=l, alpha=alpha),
        grid=(b, nt),
        in_specs=[
            pl.BlockSpec((1, POOL_HALO, POOL_WIDTH), lambda i, j: (i, jnp.maximum(j * hb - 1, 0), 0)),
            pl.BlockSpec((1, POOL_HALO, POOL_WIDTH), lambda i, j: (i, jnp.minimum((j + 1) * hb, n_halo - 1), 0)),
            pl.BlockSpec((1, tm, PU_WIDTH), lambda i, j: (i, j, 0)),
            pl.BlockSpec((1, tm, ATTN_WIDTH), lambda i, j: (i, j, 0)),
            pl.BlockSpec((1, tm, d), lambda i, j: (i, j, 0)),
            vec, vec, vec,
        ] + [full(w) for w in weights],
        out_specs=[
            pl.BlockSpec((1, tm, d), lambda i, j: (i, j, 0)),
            pl.BlockSpec((1, tm, d), lambda i, j: (i, j, 0)),
            pl.BlockSpec((1, tm // TOKEN_BLOCK, N_EXPERTS, TOKEN_BLOCK), lambda i, j: (i, j, 0, 0)),
        ],
        out_shape=[
            jax.ShapeDtypeStruct((b, l, d), F32),
            jax.ShapeDtypeStruct((b, l, d), BF16),
            jax.ShapeDtypeStruct((b, l // TOKEN_BLOCK, N_EXPERTS, TOKEN_BLOCK), F32),
        ],
        compiler_params=_params("arbitrary", "arbitrary"),
        name="mixer_output",
    )(zpu, zpu, zpu, attn, x, g1, sh2, sc2, *weights)


def _route_kernel(logit_ref, tri_ref, rank_ref, gate_ref, start_ref, bits_ref, sel_ref, *, cap):
    nb = logit_ref.shape[1]
    x = logit_ref[0]
    e = jnp.exp(x - jnp.max(x, axis=1, keepdims=True))
    aff = e / jnp.sum(e, axis=1, keepdims=True)
    gate_ref[0] = aff
    bits_ref[...] = pltpu.bitcast(aff, jnp.int32)

    def count(mask):
        return jnp.sum(jnp.sum(mask.astype(F32), axis=0), axis=1, keepdims=True)

    def bisect(i, thr):
        cand = thr | lax.shift_left(jnp.int32(1), 30 - i)
        return jnp.where(count(bits_ref[...] >= cand) >= float(cap), cand, thr)

    thr = lax.fori_loop(0, 31, bisect, jnp.zeros((N_EXPERTS, 1), jnp.int32))
    need = float(cap) - count(bits_ref[...] > thr)
    tri = tri_ref[...]
    zero = jnp.zeros((N_EXPERTS, 1), F32)

    def pass_ties(j, carry):
        bits = bits_ref[j]
        eq = (bits == thr).astype(F32)
        cum = _dot(eq.astype(BF16), tri) + carry
        sel_ref[j] = jnp.where((bits > thr) | ((eq > 0.0) & (cum <= need)), 1.0, 0.0)
        return carry + jnp.sum(eq, axis=1, keepdims=True)

    lax.fori_loop(0, nb, pass_ties, zero)
    lane = lax.broadcasted_iota(jnp.int32, (N_EXPERTS, LANES), 1)

    start_ref[0] = jnp.zeros((N_EXPERTS, LANES), jnp.int32)

    def pass_rank(j, before):
        sel = sel_ref[j]
        cum = _dot(sel.astype(BF16), tri) + before
        rank_ref[0, j] = jnp.where(sel > 0.0, cum - 1.0, -1.0)
        start_ref[0] = jnp.where(lane == j, before.astype(jnp.int32), start_ref[0])
        return before + jnp.sum(sel, axis=1, keepdims=True)

    lax.fori_loop(0, nb, pass_rank, zero)


def _route(logits_t, tri, cap):
    b, nb = logits_t.shape[:2]
    assert nb <= LANES
    blocks = pl.BlockSpec((1, nb, N_EXPERTS, TOKEN_BLOCK), lambda i: (i, 0, 0, 0))
    table = jax.ShapeDtypeStruct((b, nb, N_EXPERTS, TOKEN_BLOCK), F32)
    return pl.pallas_call(
        functools.partial(_route_kernel, cap=cap),
        grid=(b,),
        in_specs=[blocks, pl.BlockSpec((TOKEN_BLOCK, TOKEN_BLOCK), lambda i: (0, 0))],
        out_specs=[blocks, blocks, pl.BlockSpec((1, N_EXPERTS, LANES), lambda i: (i, 0, 0))],
        out_shape=[
            table,
            table,
            jax.ShapeDtypeStruct((b, N_EXPERTS, LANES), jnp.int32),
        ],
        scratch_shapes=[pltpu.VMEM((nb, N_EXPERTS, TOKEN_BLOCK), jnp.int32),
                        pltpu.VMEM((nb, N_EXPERTS, TOKEN_BLOCK), F32)],
        compiler_params=_params("arbitrary"),
        name="routing",
    )(logits_t, tri)


def _window_start(s0, cap, window):
    return pl.multiple_of(jnp.minimum(s0 & ~(BF16_ROWS - 1), cap - window), BF16_ROWS)


def _block_fits(starts_ref, idx, is_last, cap, window):
    s0 = starts_ref[idx]
    s1 = jnp.where(is_last, cap, starts_ref[jnp.where(is_last, idx, idx + 1)])
    return (s1 - _window_start(s0, cap, window)) <= window


def _ffn_kernel(starts_ref, h_ref, rank_t_ref, w1_ref, w3_ref, w2_ref, y_ref, xg_ref,
                *, cap, window, small_window):
    e = pl.program_id(1)
    group, nb = rank_t_ref.shape[0], rank_t_ref.shape[1]
    first = pl.program_id(0) * group
    bases = [((first + i) * N_EXPERTS + e) * nb for i in range(group)]
    xg_ref[...] = jnp.zeros(xg_ref.shape, F32)

    def gather_all(win):
        slot = lax.broadcasted_iota(jnp.int32, (win, TOKEN_BLOCK), 0)
        for i in range(group):
            def gather(jb, carry, i=i):
                a = _window_start(starts_ref[bases[i] + jb], cap, win)
                rank = rank_t_ref[i, jb, pl.ds(e, 1), :]
                onehot = ((slot + a).astype(F32) == rank).astype(BF16)
                rows = pl.ds(pl.multiple_of(jb * TOKEN_BLOCK, TOKEN_BLOCK), TOKEN_BLOCK)
                xg_ref[pl.ds(i * cap + a, win), :] += _dot(onehot, h_ref[i, rows, :])
                return carry

            lax.fori_loop(0, nb, gather, 0, unroll=min(nb, GATHER_UNROLL))

    if small_window < window:
        fits = True
        for i in range(group):
            fits = lax.fori_loop(
                0, nb,
                lambda jb, ok, i=i: ok & _block_fits(starts_ref, bases[i] + jb, jb == nb - 1, cap, small_window),
                fits)
        pl.when(fits)(lambda: gather_all(small_window))
        pl.when(jnp.logical_not(fits))(lambda: gather_all(window))
    else:
        gather_all(window)

    xg = xg_ref[...].astype(BF16)
    a = _dot(xg, w1_ref[0, 0])
    hid = (a * _sigmoid(a)) * _dot(xg, w3_ref[0, 0])
    y = _dot(hid.astype(BF16), w2_ref[0, 0]).astype(BF16)
    for i in range(group):
        y_ref[i, 0] = y[i * cap:(i + 1) * cap]


def _expert_ffn(starts, h2, rank_t, w1, w3, w2, layer, cap):
    b, l, d = h2.shape
    f = w1.shape[3]
    nb = l // TOKEN_BLOCK
    window = min(cap, TOKEN_BLOCK + BF16_ROWS)
    small_window = min(cap, SMALL_WINDOW)
    group = max(1, min(b, FFN_ROWS // cap))
    grid_spec = pltpu.PrefetchScalarGridSpec(
        num_scalar_prefetch=1,
        grid=(b // group, N_EXPERTS),
        in_specs=[
            pl.BlockSpec((group, l, d), lambda i, j, s: (i, 0, 0)),
            pl.BlockSpec((group, nb, N_EXPERTS, TOKEN_BLOCK), lambda i, j, s: (i, 0, 0, 0)),
            pl.BlockSpec((1, 1, d, f), lambda i, j, s: (layer, j, 0, 0)),
            pl.BlockSpec((1, 1, d, f), lambda i, j, s: (layer, j, 0, 0)),
            pl.BlockSpec((1, 1, f, d), lambda i, j, s: (layer, j, 0, 0)),
        ],
        out_specs=pl.BlockSpec((group, 1, cap, d), lambda i, j, s: (i, j, 0, 0)),
        scratch_shapes=[pltpu.VMEM((group * cap, d), F32)],
    )
    return pl.pallas_call(
        functools.partial(_ffn_kernel, cap=cap, window=window, small_window=small_window),
        grid_spec=grid_spec,
        out_shape=jax.ShapeDtypeStruct((b, N_EXPERTS, cap, d), BF16),
        compiler_params=_params("arbitrary", "arbitrary"),
        name="expert_ffn",
    )(starts, h2, rank_t, w1, w3, w2)


def _combine_kernel(starts_ref, y_ref, rank_t_ref, gate_t_ref, x_ref, g2_ref, lng_ref, lnb_ref,
                    xo_ref, ml_ref, ystack_ref, *, cap, window, small_window, alpha):
    b = pl.program_id(0)
    jb = pl.program_id(1)
    nb = pl.num_programs(1)
    table = [(b * N_EXPERTS + e) * nb + jb for e in range(N_EXPERTS)]

    def stacked(win):
        slot = lax.broadcasted_iota(jnp.int32, (win, TOKEN_BLOCK), 0)
        hi, lo = [], []
        for e in range(N_EXPERTS):
            a = _window_start(starts_ref[table[e]], cap, win)
            ystack_ref[e * win:(e + 1) * win, :] = y_ref[0, e, pl.ds(a, win), :]
            hit = (slot + a).astype(F32) == rank_t_ref[0, 0, e:e + 1, :]
            g_hi, g_lo = _split(jnp.where(hit, gate_t_ref[0, 0, e:e + 1, :], 0.0))
            hi.append(g_hi)
            lo.append(g_lo)
        ystack = ystack_ref[:N_EXPERTS * win, :]
        contract_rows = (((0,), (0,)), ((), ()))
        ml_ref[...] = (
            lax.dot_general(jnp.concatenate(hi, axis=0), ystack, contract_rows, preferred_element_type=F32)
            + lax.dot_general(jnp.concatenate(lo, axis=0), ystack, contract_rows, preferred_element_type=F32))

    if small_window < window:
        fits = _block_fits(starts_ref, table[0], jb == nb - 1, cap, small_window)
        for e in range(1, N_EXPERTS):
            fits = fits & _block_fits(starts_ref, table[e], jb == nb - 1, cap, small_window)
        pl.when(fits)(lambda: stacked(small_window))
        pl.when(jnp.logical_not(fits))(lambda: stacked(window))
    else:
        stacked(window)
    xo_ref[0] = _layer_norm(alpha * x_ref[0] + g2_ref[0] * ml_ref[...]) * lng_ref[...] + lnb_ref[...]


def _combine(starts, y, rank_t, gate_t, x, g2, ln_g, ln_b, cap, alpha):
    b, l, d = x.shape
    nb = l // TOKEN_BLOCK
    window = min(cap, TOKEN_BLOCK + BF16_ROWS)
    small_window = min(cap, SMALL_WINDOW)
    grid_spec = pltpu.PrefetchScalarGridSpec(
        num_scalar_prefetch=1,
        grid=(b, nb),
        in_specs=[
            pl.BlockSpec((1, N_EXPERTS, cap, d), lambda i, j, s: (i, 0, 0, 0)),
            pl.BlockSpec((1, 1, N_EXPERTS, TOKEN_BLOCK), lambda i, j, s: (i, j, 0, 0)),
            pl.BlockSpec((1, 1, N_EXPERTS, TOKEN_BLOCK), lambda i, j, s: (i, j, 0, 0)),
            pl.BlockSpec((1, TOKEN_BLOCK, d), lambda i, j, s: (i, j, 0)),
            pl.BlockSpec((1, 1, d), lambda i, j, s: (i, 0, 0)),
            pl.BlockSpec((1, d), lambda i, j, s: (0, 0)),
            pl.BlockSpec((1, d), lambda i, j, s: (0, 0)),
        ],
        out_specs=pl.BlockSpec((1, TOKEN_BLOCK, d), lambda i, j, s: (i, j, 0)),
        scratch_shapes=[
            pltpu.VMEM((TOKEN_BLOCK, d), F32),
            pltpu.VMEM((N_EXPERTS * window, d), BF16),
        ],
    )
    return pl.pallas_call(
        functools.partial(_combine_kernel, cap=cap, window=window, small_window=small_window, alpha=alpha),
        grid_spec=grid_spec,
        out_shape=jax.ShapeDtypeStruct((b, l, d), F32),
        compiler_params=_params("arbitrary", "arbitrary"),
        name="moe_combine",
    )(starts, y, rank_t, gate_t, x, g2, ln_g, ln_b)


def _group_mean_matrix(n, group):
    idx = jnp.arange(n) // group
    return (idx[:, None] == idx[None, :]).astype(F32) / group


def _rope_tables(n):
    rows = n // GRID_W
    r = jnp.repeat(jnp.arange(rows, dtype=F32), GRID_W)
    col = jnp.tile(jnp.arange(GRID_W, dtype=F32), rows)
    inv = ROPE_THETA ** (-jnp.arange(ROPE_AXIS_FREQS, dtype=F32) / ROPE_AXIS_FREQS)
    ang = jnp.concatenate([r[:, None] * inv, col[:, None] * inv], axis=-1)
    cos, sin = jnp.cos(ang), jnp.sin(ang)
    cos_t = jnp.tile(cos, (1, LANES // (HEAD_DIM // 2)))
    sin_t = jnp.tile(jnp.concatenate([-sin, sin], axis=-1), (1, LANES // HEAD_DIM))
    return cos_t, sin_t


def _layer_weights(l, w_in, pool_w, pool_scale, sgu_g, sgu_w, sgu_b, q_g, k_g, w_out, ln1_g, ln1_b,
                   w_router, w1, w3, w2, ln2_g, ln2_b):
    d = w_in.shape[1]
    q_scale = HEAD_DIM ** -0.5 * math.log2(math.e)
    gain = jnp.concatenate([jnp.tile(q_g[l] * q_scale, N_HEADS), jnp.tile(k_g[l], N_KV_HEADS)])[None, :]
    pool_bd = jax.scipy.linalg.block_diag(*[pool_w[l, g] for g in range(len(POOL_WINDOWS))])
    wr = jnp.pad(w_router[l], ((0, 0), (0, LANES - N_EXPERTS)))
    wr_hi = wr.astype(BF16)
    return dict(
        w_in=w_in[l].astype(BF16),
        gain=gain,
        pool_w=pool_bd.astype(BF16),
        pool_scale=pool_scale[l][None, :],
        gavg256=_group_mean_matrix(SGU_WIDTH, SGU_WIDTH // SGU_HEADS).astype(BF16),
        sgu_g=sgu_g[l].reshape(1, SGU_WIDTH),
        sgu_w=jnp.swapaxes(sgu_w[l], 0, 1).reshape(SGU_CHUNK, SGU_HEADS * SGU_CHUNK).astype(BF16),
        sgu_b=jnp.repeat(sgu_b[l].T, SGU_WIDTH // SGU_HEADS, axis=1),
        w_out=w_out[l].astype(BF16),
        ln1_g=ln1_g[l][None, :], ln1_b=ln1_b[l][None, :],
        wr_hi=wr_hi, wr_lo=(wr - wr_hi.astype(F32)).astype(BF16),
        layer=l, w1=w1, w3=w3, w2=w2,
        ln2_g=ln2_g[l][None, :], ln2_b=ln2_b[l][None, :],
    )


def _moe(x, h2, logits, g2, lw, tri, alpha):
    b, l, d = x.shape
    cap = EC_CAPACITY_FACTOR * l // N_EXPERTS
    rank_t, gate_t, starts = _route(logits, tri, cap)
    starts = starts[:, :, :l // TOKEN_BLOCK].reshape(-1)
    y = _expert_ffn(starts, h2, rank_t, lw["w1"], lw["w3"], lw["w2"], lw["layer"], cap)
    return _combine(starts, y, rank_t, gate_t, x, g2, lw["ln2_g"], lw["ln2_b"], cap, alpha)


def kernel(x, c, ctx, c_ctx, w_mod, b_mod, w_in, pool_w, pool_scale, sgu_g, sgu_w, sgu_b, q_g, k_g, w_out,
           ln1_g, ln1_b, w_router, w1, w3, w2, ln2_g, ln2_b):
    batch, seq, d = x.shape
    ctx_len = ctx.shape[1]
    depth = w_mod.shape[0]
    alpha = (2 * depth) ** 0.25

    cond = jnp.concatenate([c, c_ctx[None, :], jnp.zeros((COND_ROWS - batch - 1, d), F32)], axis=0)
    mod = _modulation(cond, w_mod, b_mod)

    cos, sin = _rope_tables(seq)
    cos_c = jnp.ones((ctx_len, LANES), F32)
    sin_c = jnp.zeros((ctx_len, LANES), F32)
    gavg128 = _group_mean_matrix(LANES, HEAD_DIM).astype(BF16)
    tri = jnp.triu(jnp.ones((TOKEN_BLOCK, TOKEN_BLOCK), F32)).astype(BF16)

    w1, w3, w2 = w1.astype(BF16), w3.astype(BF16), w2.astype(BF16)

    xc = ctx
    for l in range(depth):
        last = l == depth - 1
        lw = _layer_weights(l, w_in, pool_w, pool_scale, sgu_g, sgu_w, sgu_b, q_g, k_g, w_out, ln1_g, ln1_b,
                            w_router, w1, w3, w2, ln2_g, ln2_b)
        m = mod[l].reshape(COND_ROWS, 6, d)
        sh1, sc1, g1, sh2, sc2, g2 = [m[:batch, i][:, None, :] for i in range(6)]
        csh1, csc1, cg1, csh2, csc2, cg2 = [jnp.broadcast_to(m[batch, i][None, None, :], (batch, 1, d))
                                            for i in range(6)]

        zpu, qt, kl, vtl = _in_projection(x, sh1, sc1, lw["w_in"], lw["gain"], cos, sin, gavg128)
        zpu_c, qt_c, kc, vtc = _in_projection(xc, csh1, csc1, lw["w_in"], lw["gain"], cos_c, sin_c, gavg128)

        attn = _attention(qt, jnp.concatenate([kl, kc], axis=1), jnp.concatenate([vtl, vtc], axis=2))
        if not last:
            attn_c = _attention(qt_c, kc, vtc)
            xc, h2c, logits_c = _mixer_output(zpu_c, attn_c, xc, cg1, csh2, csc2, lw, alpha)
            xc = _moe(xc, h2c, logits_c, cg2, lw, tri, alpha)

        x, h2, logits = _mixer_output(zpu, attn, x, g1, sh2, sc2, lw, alpha)
        x = _moe(x, h2, logits, g2, lw, tri, alpha)
    return x
```

```python
import functools
import math

import jax
import jax.numpy as jnp
from jax import lax
from jax.experimental import pallas as pl
from jax.experimental.pallas import tpu as pltpu

F32 = jnp.float32
BF16 = jnp.bfloat16

GRID_W = 64
POOL_WIDTH = 256
POOL_WINDOWS = (2, 4, 8, 16)
POOL_GROUP = 64
POOL_HALO = 8
SGU_WIDTH = 256
SGU_HEADS = 4
SGU_CHUNK = 128
HEAD_DIM = 64
N_HEADS = 8
N_KV_HEADS = 2
KV_GROUP = N_HEADS // N_KV_HEADS
ATTN_WIDTH = N_HEADS * HEAD_DIM
KV_WIDTH = N_KV_HEADS * HEAD_DIM
PU_WIDTH = POOL_WIDTH + 2 * SGU_WIDTH
QK_WIDTH = ATTN_WIDTH + KV_WIDTH
ROPE_THETA = 10000.0
ROPE_AXIS_FREQS = HEAD_DIM // 4
N_EXPERTS = 16
EC_CAPACITY_FACTOR = 2
LN_EPS = 1e-6

LANES = 128
SUBLANES = 8
BF16_ROWS = 16
V_EXT_ROWS = HEAD_DIM + BF16_ROWS
VMEM_LIMIT_BYTES = 56 * 1024 * 1024

ROW_TILE = 512
KV_TILE = 256
ATTN_Q_TILE = 512
TOKEN_BLOCK = 128
GATHER_UNROLL = 4
FFN_ROWS = 512
SMALL_WINDOW = 48
COND_ROWS = 16


def _params(*sem):
    return pltpu.CompilerParams(dimension_semantics=sem, vmem_limit_bytes=VMEM_LIMIT_BYTES)


def _split(x):
    hi = x.astype(BF16)
    lo = (x - hi.astype(F32)).astype(BF16)
    return hi, lo


def _dot(a, b):
    return jnp.dot(a, b, preferred_element_type=F32)


def _dot3(a, b_hi, b_lo):
    a_hi, a_lo = _split(a)
    return _dot(a_hi, b_hi) + _dot(a_hi, b_lo) + _dot(a_lo, b_hi)


def _dot_exact_rhs(a, b):
    a_hi, a_lo = _split(a)
    return _dot(a_hi, b) + _dot(a_lo, b)


def _layer_norm(x):
    mu = jnp.mean(x, axis=-1, keepdims=True)
    xc = x - mu
    var = jnp.mean(xc * xc, axis=-1, keepdims=True)
    return xc * lax.rsqrt(var + LN_EPS)


def _sigmoid(x):
    return 1.0 / (1.0 + jnp.exp(-x))


def _gelu_tanh(x):
    c = math.sqrt(2.0 / math.pi)
    return 0.5 * x * (1.0 + jnp.tanh(c * (x + 0.044715 * (x * x * x))))


def _mod_kernel(cond_ref, w_ref, b_ref, o_ref):
    a = cond_ref[...]
    a = a * _sigmoid(a)
    w_hi, w_lo = _split(w_ref[0])
    o_ref[0] = _dot3(a, w_hi, w_lo) + b_ref[0]


def _modulation(cond, w_mod, b_mod):
    depth, d, n = w_mod.shape
    tn = n // 4
    return pl.pallas_call(
        _mod_kernel,
        grid=(depth, n // tn),
        in_specs=[
            pl.BlockSpec((COND_ROWS, d), lambda l, j: (0, 0)),
            pl.BlockSpec((1, d, tn), lambda l, j: (l, 0, j)),
            pl.BlockSpec((1, 1, tn), lambda l, j: (l, 0, j)),
        ],
        out_specs=pl.BlockSpec((1, COND_ROWS, tn), lambda l, j: (l, 0, j)),
        out_shape=jax.ShapeDtypeStruct((depth, COND_ROWS, n), F32),
        compiler_params=_params("arbitrary", "arbitrary"),
        name="modulation",
    )(cond, w_mod, b_mod.reshape(depth, 1, n))


def _in_kernel(x_ref, sh_ref, sc_ref, w_ref, gain_ref, cos_ref, sin_ref, gavg_ref,
               zpu_ref, qt_ref, k_ref, vt_ref):
    gavg = gavg_ref[...]
    lane = lax.broadcasted_iota(jnp.int32, (KV_TILE, LANES), 1)
    first_half = (lane & (HEAD_DIM - 1)) < (HEAD_DIM // 2)
    row = lax.broadcasted_iota(jnp.int32, (V_EXT_ROWS - HEAD_DIM, KV_TILE), 0)
    ones_row = (row == 0).astype(F32)

    for r in range(x_ref.shape[1] // KV_TILE):
        rows = slice(r * KV_TILE, (r + 1) * KV_TILE)
        h = _layer_norm(x_ref[0, rows, :]) * (1.0 + sc_ref[0]) + sh_ref[0]
        z = _dot(h.astype(BF16), w_ref[...])
        zpu_ref[0, rows, :] = z[:, :PU_WIDTH]

        cos = cos_ref[rows, :]
        sin = sin_ref[rows, :]
        pieces = []
        for c in range(QK_WIDTH // LANES):
            lo = PU_WIDTH + c * LANES
            y = z[:, lo:lo + LANES]
            ms = _dot_exact_rhs(y * y, gavg)
            y = y * lax.rsqrt(ms + LN_EPS) * gain_ref[:, c * LANES:(c + 1) * LANES]
            partner = jnp.where(first_half,
                                pltpu.roll(y, LANES - HEAD_DIM // 2, 1),
                                pltpu.roll(y, HEAD_DIM // 2, 1))
            pieces.append(y * cos + partner * sin)
        q = jnp.concatenate(pieces[:ATTN_WIDTH // LANES], axis=1)
        qt_ref[0, :, rows] = q.T.astype(BF16)
        k_ref[0, rows, :] = pieces[-1].astype(BF16)

        vt = z[:, PU_WIDTH + QK_WIDTH:].T
        for g in range(N_KV_HEADS):
            ext = jnp.concatenate([vt[g * HEAD_DIM:(g + 1) * HEAD_DIM], ones_row], axis=0)
            vt_ref[0, g, r] = ext.astype(BF16)


def _in_projection(x, sh, sc, w_in, gain, cos, sin, gavg):
    b, l, d = x.shape
    n = w_in.shape[1]
    tm = min(ROW_TILE, l)
    nt = l // tm
    sub = tm // KV_TILE
    vec = pl.BlockSpec((1, 1, d), lambda i, j: (i, 0, 0))
    return pl.pallas_call(
        _in_kernel,
        grid=(b, nt),
        in_specs=[
            pl.BlockSpec((1, tm, d), lambda i, j: (i, j, 0)),
            vec, vec,
            pl.BlockSpec((d, n), lambda i, j: (0, 0)),
            pl.BlockSpec((1, QK_WIDTH), lambda i, j: (0, 0)),
            pl.BlockSpec((tm, LANES), lambda i, j: (j, 0)),
            pl.BlockSpec((tm, LANES), lambda i, j: (j, 0)),
            pl.BlockSpec((LANES, LANES), lambda i, j: (0, 0)),
        ],
        out_specs=[
            pl.BlockSpec((1, tm, PU_WIDTH), lambda i, j: (i, j, 0)),
            pl.BlockSpec((1, ATTN_WIDTH, tm), lambda i, j: (i, 0, j)),
            pl.BlockSpec((1, tm, KV_WIDTH), lambda i, j: (i, j, 0)),
            pl.BlockSpec((1, N_KV_HEADS, sub, V_EXT_ROWS, KV_TILE), lambda i, j: (i, 0, j, 0, 0)),
        ],
        out_shape=[
            jax.ShapeDtypeStruct((b, l, PU_WIDTH), F32),
            jax.ShapeDtypeStruct((b, ATTN_WIDTH, l), BF16),
            jax.ShapeDtypeStruct((b, l, KV_WIDTH), BF16),
            jax.ShapeDtypeStruct((b, N_KV_HEADS, l // KV_TILE, V_EXT_ROWS, KV_TILE), BF16),
        ],
        compiler_params=_params("arbitrary", "arbitrary"),
        name="in_projection",
    )(x, sh, sc, w_in, gain, cos, sin, gavg)


def _attn_kernel(qt_ref, k_ref, vt_ref, o_ref, qpad_ref, m_ref, acc_ref, p_ref, scale_ref, *, n_kv):
    tq = qt_ref.shape[2]
    zeros = jnp.zeros((HEAD_DIM, tq), BF16)
    for h in range(N_HEADS):
        qh = qt_ref[0, h * HEAD_DIM:(h + 1) * HEAD_DIM, :]
        qpad_ref[h] = jnp.concatenate([qh, zeros] if h // KV_GROUP == 0 else [zeros, qh], axis=0)
    m_ref[...] = jnp.full(m_ref.shape, -1e30, F32)
    acc_ref[...] = jnp.zeros(acc_ref.shape, F32)

    def scores(c):
        start = pl.multiple_of(c * KV_TILE, KV_TILE)
        kc = k_ref[0, pl.ds(start, KV_TILE), :]
        return [_dot(kc, qpad_ref[h]) for h in range(N_HEADS)]

    def softmax(s, slot):
        for h in range(N_HEADS):
            m = m_ref[h]
            m_new = jnp.maximum(m, jnp.max(s[h], axis=0, keepdims=True))
            p_ref[slot, h] = jnp.exp2(s[h] - m_new).astype(BF16)
            scale_ref[slot, h] = jnp.exp2(m - m_new)
            m_ref[h] = m_new

    def values(c, slot):
        for h in range(N_HEADS):
            acc_ref[h] = acc_ref[h] * scale_ref[slot, h] + _dot(vt_ref[0, h // KV_GROUP, c], p_ref[slot, h])

    def step(c, slot_prev, slot_cur):
        s = scores(c)
        values(c - 1, slot_prev)
        softmax(s, slot_cur)

    softmax(scores(0), 0)
    pairs, leftover = divmod(n_kv - 1, 2)

    def body(i, carry):
        step(2 * i + 1, 0, 1)
        step(2 * i + 2, 1, 0)
        return carry

    lax.fori_loop(0, pairs, body, 0)
    if leftover:
        step(n_kv - 1, 0, 1)
    values(n_kv - 1, leftover)
    blocks = []
    for hp in range(N_HEADS // 2):
        pair = []
        for h in (2 * hp, 2 * hp + 1):
            acc = acc_ref[h]
            pair.append(acc[:HEAD_DIM] / acc[HEAD_DIM:HEAD_DIM + 1])
        blocks.append(jnp.concatenate(pair, axis=0).T)
    o_ref[0] = jnp.concatenate(blocks, axis=1).astype(BF16)


def _attention(qt, k, vt):
    b, _, lq = qt.shape
    lk = k.shape[1]
    n_kv = lk // KV_TILE
    tq = min(ATTN_Q_TILE, lq)
    return pl.pallas_call(
        functools.partial(_attn_kernel, n_kv=n_kv),
        grid=(b, lq // tq),
        in_specs=[
            pl.BlockSpec((1, ATTN_WIDTH, tq), lambda i, j: (i, 0, j)),
            pl.BlockSpec((1, lk, KV_WIDTH), lambda i, j: (i, 0, 0)),
            pl.BlockSpec((1, N_KV_HEADS, n_kv, V_EXT_ROWS, KV_TILE), lambda i, j: (i, 0, 0, 0, 0)),
        ],
        out_specs=pl.BlockSpec((1, tq, ATTN_WIDTH), lambda i, j: (i, j, 0)),
        out_shape=jax.ShapeDtypeStruct((b, lq, ATTN_WIDTH), BF16),
        scratch_shapes=[
            pltpu.VMEM((N_HEADS, 2 * HEAD_DIM, tq), BF16),
            pltpu.VMEM((N_HEADS, 1, tq), F32),
            pltpu.VMEM((N_HEADS, V_EXT_ROWS, tq), F32),
            pltpu.VMEM((2, N_HEADS, KV_TILE, tq), BF16),
            pltpu.VMEM((2, N_HEADS, 1, tq), F32),
        ],
        compiler_params=_params("arbitrary", "arbitrary"),
        name="attention",
    )(qt, k, vt)


def _mix_kernel(zp_ref, zn_ref, z_ref, attn_ref, x_ref, g1_ref, sh2_ref, sc2_ref,
                poolw_ref, pscale_ref, gavg_ref, sgug_ref, sguw_ref, sgub_ref,
                wout_ref, lng_ref, lnb_ref, wr_hi_ref, wr_lo_ref,
                xo_ref, h2_ref, logit_ref, *, seq_len, alpha):
    j = pl.program_id(1)
    nt = pl.num_programs(1)
    z = z_ref[0]
    tm = z.shape[0]
    lane = lax.broadcasted_iota(jnp.int32, (tm, POOL_WIDTH), 1)
    group = lax.shift_right_logical(lane, POOL_GROUP.bit_length() - 1)

    p = z[:, :POOL_WIDTH]
    prev = zp_ref[0] * (j > 0).astype(F32)
    nxt = zn_ref[0] * (j < nt - 1).astype(F32)
    ext = jnp.concatenate([prev, p, nxt], axis=0)
    n_ext = ext.shape[0]
    s2 = ext + pltpu.roll(ext, 1, 0)
    s4 = pltpu.roll(s2, 1, 0) + pltpu.roll(s2, n_ext - 1, 0)
    s8 = pltpu.roll(s4, 2, 0) + pltpu.roll(s4, n_ext - 2, 0)
    s16 = pltpu.roll(s8, 4, 0) + pltpu.roll(s8, n_ext - 4, 0)
    sums = [s[POOL_HALO:POOL_HALO + tm] for s in (s2, s4, s8, s16)]
    wsum = jnp.where(group == 0, sums[0], jnp.where(group == 1, sums[1], jnp.where(group == 2, sums[2], sums[3])))
    half = jnp.where(group == 0, 1, jnp.where(group == 1, 2, jnp.where(group == 2, 4, 8)))
    t = j * tm + lax.broadcasted_iota(jnp.int32, (tm, POOL_WIDTH), 0)
    cnt = jnp.minimum(t + half, seq_len) - jnp.maximum(t - half, 0)
    pin = (wsum / cnt.astype(F32) - p).astype(BF16)

    gavg = gavg_ref[...]
    head = lax.shift_right_logical(lax.broadcasted_iota(jnp.int32, (SGU_CHUNK, SGU_WIDTH), 1),
                                   (SGU_WIDTH // SGU_HEADS).bit_length() - 1)
    split = POOL_WIDTH + SGU_WIDTH

    for r in range(tm // KV_TILE):
        sub = slice(r * KV_TILE, (r + 1) * KV_TILE)
        pool = _dot(pin[sub], poolw_ref[...]) * pscale_ref[...]

        sgu_parts = []
        for c in range(KV_TILE // SGU_CHUNK):
            rows = slice(r * KV_TILE + c * SGU_CHUNK, r * KV_TILE + (c + 1) * SGU_CHUNK)
            u = _gelu_tanh(z[rows, POOL_WIDTH:POOL_WIDTH + SGU_WIDTH])
            v = _gelu_tanh(z[rows, POOL_WIDTH + SGU_WIDTH:])
            mu = _dot_exact_rhs(v, gavg)
            vc = v - mu
            var = _dot_exact_rhs(vc * vc, gavg)
            vn = vc * lax.rsqrt(var + LN_EPS) * sgug_ref[...]
            stacked = jnp.concatenate(
                [jnp.where(head == hd, vn, 0.0).astype(BF16) for hd in range(SGU_HEADS)], axis=0)
            sgu_parts.append(u * (_dot(sguw_ref[...], stacked) + sgub_ref[...]))
        sgu = jnp.concatenate(sgu_parts, axis=0)

        ps = jnp.concatenate([pool, sgu], axis=1).astype(BF16)
        y = _dot(ps, wout_ref[:split, :]) + _dot(attn_ref[0, sub, :], wout_ref[split:, :])
        xn = _layer_norm(alpha * x_ref[0, sub, :] + g1_ref[0] * y) * lng_ref[...] + lnb_ref[...]
        xo_ref[0, sub, :] = xn

        h2 = _layer_norm(xn) * (1.0 + sc2_ref[0]) + sh2_ref[0]
        h2_ref[0, sub, :] = h2.astype(BF16)
        logits_t = _dot3(h2, wr_hi_ref[...], wr_lo_ref[...]).T
        for c in range(KV_TILE // TOKEN_BLOCK):
            logit_ref[0, r * (KV_TILE // TOKEN_BLOCK) + c] = (
                logits_t[:N_EXPERTS, c * TOKEN_BLOCK:(c + 1) * TOKEN_BLOCK])


def _mixer_output(zpu, attn, x, g1, sh2, sc2, lw, alpha):
    b, l, d = x.shape
    tm = min(ROW_TILE, l)
    nt = l // tm
    hb = tm // POOL_HALO
    n_halo = l // POOL_HALO
    vec = pl.BlockSpec((1, 1, d), lambda i, j: (i, 0, 0))

    def full(a):
        nd = a.ndim
        return pl.BlockSpec(a.shape, lambda i, j: (0,) * nd)

    weights = [lw["pool_w"], lw["pool_scale"], lw["gavg256"], lw["sgu_g"], lw["sgu_w"], lw["sgu_b"],
               lw["w_out"], lw["ln1_g"], lw["ln1_b"], lw["wr_hi"], lw["wr_lo"]]
    return pl.pallas_call(
        functools.partial(_mix_kernel, seq_len=l, alpha=alpha),
        grid=(b, nt),
        in_specs=[
            pl.BlockSpec((1, POOL_HALO, POOL_WIDTH), lambda i, j: (i, jnp.maximum(j * hb - 1, 0), 0)),
            pl.BlockSpec((1, POOL_HALO, POOL_WIDTH), lambda i, j: (i, jnp.minimum((j + 1) * hb, n_halo - 1), 0)),
            pl.BlockSpec((1, tm, PU_WIDTH), lambda i, j: (i, j, 0)),
            pl.BlockSpec((1, tm, ATTN_WIDTH), lambda i, j: (i, j, 0)),
            pl.BlockSpec((1, tm, d), lambda i, j: (i, j, 0)),
            vec, vec, vec,
        ] + [full(w) for w in weights],
        out_specs=[
            pl.BlockSpec((1, tm, d), lambda i, j: (i, j, 0)),
            pl.BlockSpec((1, tm, d), lambda i, j: (i, j, 0)),
            pl.BlockSpec((1, tm // TOKEN_BLOCK, N_EXPERTS, TOKEN_BLOCK), lambda i, j: (i, j, 0, 0)),
        ],
        out_shape=[
            jax.ShapeDtypeStruct((b, l, d), F32),
            jax.ShapeDtypeStruct((b, l, d), BF16),
            jax.ShapeDtypeStruct((b, l // TOKEN_BLOCK, N_EXPERTS, TOKEN_BLOCK), F32),
        ],
        compiler_params=_params("arbitrary", "arbitrary"),
        name="mixer_output",
    )(zpu, zpu, zpu, attn, x, g1, sh2, sc2, *weights)


def _route_kernel(logit_ref, tri_ref, rank_ref, gate_ref, start_ref, bits_ref, sel_ref, *, cap):
    nb = logit_ref.shape[1]
    x = logit_ref[0]
    e = jnp.exp(x - jnp.max(x, axis=1, keepdims=True))
    aff = e / jnp.sum(e, axis=1, keepdims=True)
    gate_ref[0] = aff
    bits_ref[...] = pltpu.bitcast(aff, jnp.int32)

    def count(mask):
        return jnp.sum(jnp.sum(mask.astype(F32), axis=0), axis=1, keepdims=True)

    def bisect(i, thr):
        cand = thr | lax.shift_left(jnp.int32(1), 30 - i)
        return jnp.where(count(bits_ref[...] >= cand) >= float(cap), cand, thr)

    thr = lax.fori_loop(0, 31, bisect, jnp.zeros((N_EXPERTS, 1), jnp.int32))
    need = float(cap) - count(bits_ref[...] > thr)
    tri = tri_ref[...]
    zero = jnp.zeros((N_EXPERTS, 1), F32)

    def pass_ties(j, carry):
        bits = bits_ref[j]
        eq = (bits == thr).astype(F32)
        cum = _dot(eq.astype(BF16), tri) + carry
        sel_ref[j] = jnp.where((bits > thr) | ((eq > 0.0) & (cum <= need)), 1.0, 0.0)
        return carry + jnp.sum(eq, axis=1, keepdims=True)

    lax.fori_loop(0, nb, pass_ties, zero)
    lane = lax.broadcasted_iota(jnp.int32, (N_EXPERTS, LANES), 1)

    start_ref[0] = jnp.zeros((N_EXPERTS, LANES), jnp.int32)

    def pass_rank(j, before):
        sel = sel_ref[j]
        cum = _dot(sel.astype(BF16), tri) + before
        rank_ref[0, j] = jnp.where(sel > 0.0, cum - 1.0, -1.0)
        start_ref[0] = jnp.where(lane == j, before.astype(jnp.int32), start_ref[0])
        return before + jnp.sum(sel, axis=1, keepdims=True)

    lax.fori_loop(0, nb, pass_rank, zero)


def _route(logits_t, tri, cap):
    b, nb = logits_t.shape[:2]
    assert nb <= LANES
    blocks = pl.BlockSpec((1, nb, N_EXPERTS, TOKEN_BLOCK), lambda i: (i, 0, 0, 0))
    table = jax.ShapeDtypeStruct((b, nb, N_EXPERTS, TOKEN_BLOCK), F32)
    return pl.pallas_call(
        functools.partial(_route_kernel, cap=cap),
        grid=(b,),
        in_specs=[blocks, pl.BlockSpec((TOKEN_BLOCK, TOKEN_BLOCK), lambda i: (0, 0))],
        out_specs=[blocks, blocks, pl.BlockSpec((1, N_EXPERTS, LANES), lambda i: (i, 0, 0))],
        out_shape=[
            table,
            table,
            jax.ShapeDtypeStruct((b, N_EXPERTS, LANES), jnp.int32),
        ],
        scratch_shapes=[pltpu.VMEM((nb, N_EXPERTS, TOKEN_BLOCK), jnp.int32),
                        pltpu.VMEM((nb, N_EXPERTS, TOKEN_BLOCK), F32)],
        compiler_params=_params("arbitrary"),
        name="routing",
    )(logits_t, tri)


def _window_start(s0, cap, window):
    return pl.multiple_of(jnp.minimum(s0 & ~(BF16_ROWS - 1), cap - window), BF16_ROWS)


def _block_fits(starts_ref, idx, is_last, cap, window):
    s0 = starts_ref[idx]
    s1 = jnp.where(is_last, cap, starts_ref[jnp.where(is_last, idx, idx + 1)])
    return (s1 - _window_start(s0, cap, window)) <= window


def _ffn_kernel(starts_ref, h_ref, rank_t_ref, w1_ref, w3_ref, w2_ref, y_ref, xg_ref,
                *, cap, window, small_window):
    e = pl.program_id(1)
    group, nb = rank_t_ref.shape[0], rank_t_ref.shape[1]
    first = pl.program_id(0) * group
    bases = [((first + i) * N_EXPERTS + e) * nb for i in range(group)]
    xg_ref[...] = jnp.zeros(xg_ref.shape, F32)

    def gather_all(win):
        slot = lax.broadcasted_iota(jnp.int32, (win, TOKEN_BLOCK), 0)
        for i in range(group):
            def gather(jb, carry, i=i):
                a = _window_start(starts_ref[bases[i] + jb], cap, win)
                rank = rank_t_ref[i, jb, pl.ds(e, 1), :]
                onehot = ((slot + a).astype(F32) == rank).astype(BF16)
                rows = pl.ds(pl.multiple_of(jb * TOKEN_BLOCK, TOKEN_BLOCK), TOKEN_BLOCK)
                xg_ref[pl.ds(i * cap + a, win), :] += _dot(onehot, h_ref[i, rows, :])
                return carry

            lax.fori_loop(0, nb, gather, 0, unroll=min(nb, GATHER_UNROLL))

    if small_window < window:
        fits = True
        for i in range(group):
            fits = lax.fori_loop(
                0, nb,
                lambda jb, ok, i=i: ok & _block_fits(starts_ref, bases[i] + jb, jb == nb - 1, cap, small_window),
                fits)
        pl.when(fits)(lambda: gather_all(small_window))
        pl.when(jnp.logical_not(fits))(lambda: gather_all(window))
    else:
        gather_all(window)

    xg = xg_ref[...].astype(BF16)
    a = _dot(xg, w1_ref[0, 0])
    hid = (a * _sigmoid(a)) * _dot(xg, w3_ref[0, 0])
    y = _dot(hid.astype(BF16), w2_ref[0, 0]).astype(BF16)
    for i in range(group):
        y_ref[i, 0] = y[i * cap:(i + 1) * cap]


def _expert_ffn(starts, h2, rank_t, w1, w3, w2, layer, cap):
    b, l, d = h2.shape
    f = w1.shape[3]
    nb = l // TOKEN_BLOCK
    window = min(cap, TOKEN_BLOCK + BF16_ROWS)
    small_window = min(cap, SMALL_WINDOW)
    group = max(1, min(b, FFN_ROWS // cap))
    grid_spec = pltpu.PrefetchScalarGridSpec(
        num_scalar_prefetch=1,
        grid=(b // group, N_EXPERTS),
        in_specs=[
            pl.BlockSpec((group, l, d), lambda i, j, s: (i, 0, 0)),
            pl.BlockSpec((group, nb, N_EXPERTS, TOKEN_BLOCK), lambda i, j, s: (i, 0, 0, 0)),
            pl.BlockSpec((1, 1, d, f), lambda i, j, s: (layer, j, 0, 0)),
            pl.BlockSpec((1, 1, d, f), lambda i, j, s: (layer, j, 0, 0)),
            pl.BlockSpec((1, 1, f, d), lambda i, j, s: (layer, j, 0, 0)),
        ],
        out_specs=pl.BlockSpec((group, 1, cap, d), lambda i, j, s: (i, j, 0, 0)),
        scratch_shapes=[pltpu.VMEM((group * cap, d), F32)],
    )
    return pl.pallas_call(
        functools.partial(_ffn_kernel, cap=cap, window=window, small_window=small_window),
        grid_spec=grid_spec,
        out_shape=jax.ShapeDtypeStruct((b, N_EXPERTS, cap, d), BF16),
        compiler_params=_params("arbitrary", "arbitrary"),
        name="expert_ffn",
    )(starts, h2, rank_t, w1, w3, w2)


def _combine_kernel(starts_ref, y_ref, rank_t_ref, gate_t_ref, x_ref, g2_ref, lng_ref, lnb_ref,
                    xo_ref, ml_ref, ystack_ref, *, cap, window, small_window, alpha):
    b = pl.program_id(0)
    jb = pl.program_id(1)
    nb = pl.num_programs(1)
    table = [(b * N_EXPERTS + e) * nb + jb for e in range(N_EXPERTS)]

    def stacked(win):
        slot = lax.broadcasted_iota(jnp.int32, (win, TOKEN_BLOCK), 0)
        hi, lo = [], []
        for e in range(N_EXPERTS):
            a = _window_start(starts_ref[table[e]], cap, win)
            ystack_ref[e * win:(e + 1) * win, :] = y_ref[0, e, pl.ds(a, win), :]
            hit = (slot + a).astype(F32) == rank_t_ref[0, 0, e:e + 1, :]
            g_hi, g_lo = _split(jnp.where(hit, gate_t_ref[0, 0, e:e + 1, :], 0.0))
            hi.append(g_hi)
            lo.append(g_lo)
        ystack = ystack_ref[:N_EXPERTS * win, :]
        contract_rows = (((0,), (0,)), ((), ()))
        ml_ref[...] = (
            lax.dot_general(jnp.concatenate(hi, axis=0), ystack, contract_rows, preferred_element_type=F32)
            + lax.dot_general(jnp.concatenate(lo, axis=0), ystack, contract_rows, preferred_element_type=F32))

    if small_window < window:
        fits = _block_fits(starts_ref, table[0], jb == nb - 1, cap, small_window)
        for e in range(1, N_EXPERTS):
            fits = fits & _block_fits(starts_ref, table[e], jb == nb - 1, cap, small_window)
        pl.when(fits)(lambda: stacked(small_window))
        pl.when(jnp.logical_not(fits))(lambda: stacked(window))
    else:
        stacked(window)
    xo_ref[0] = _layer_norm(alpha * x_ref[0] + g2_ref[0] * ml_ref[...]) * lng_ref[...] + lnb_ref[...]


def _combine(starts, y, rank_t, gate_t, x, g2, ln_g, ln_b, cap, alpha):
    b, l, d = x.shape
    nb = l // TOKEN_BLOCK
    window = min(cap, TOKEN_BLOCK + BF16_ROWS)
    small_window = min(cap, SMALL_WINDOW)
    grid_spec = pltpu.PrefetchScalarGridSpec(
        num_scalar_prefetch=1,
        grid=(b, nb),
        in_specs=[
            pl.BlockSpec((1, N_EXPERTS, cap, d), lambda i, j, s: (i, 0, 0, 0)),
            pl.BlockSpec((1, 1, N_EXPERTS, TOKEN_BLOCK), lambda i, j, s: (i, j, 0, 0)),
            pl.BlockSpec((1, 1, N_EXPERTS, TOKEN_BLOCK), lambda i, j, s: (i, j, 0, 0)),
            pl.BlockSpec((1, TOKEN_BLOCK, d), lambda i, j, s: (i, j, 0)),
            pl.BlockSpec((1, 1, d), lambda i, j, s: (i, 0, 0)),
            pl.BlockSpec((1, d), lambda i, j, s: (0, 0)),
            pl.BlockSpec((1, d), lambda i, j, s: (0, 0)),
        ],
        out_specs=pl.BlockSpec((1, TOKEN_BLOCK, d), lambda i, j, s: (i, j, 0)),
        scratch_shapes=[
            pltpu.VMEM((TOKEN_BLOCK, d), F32),
            pltpu.VMEM((N_EXPERTS * window, d), BF16),
        ],
    )
    return pl.pallas_call(
        functools.partial(_combine_kernel, cap=cap, window=window, small_window=small_window, alpha=alpha),
        grid_spec=grid_spec,
        out_shape=jax.ShapeDtypeStruct((b, l, d), F32),
        compiler_params=_params("arbitrary", "arbitrary"),
        name="moe_combine",
    )(starts, y, rank_t, gate_t, x, g2, ln_g, ln_b)


def _group_mean_matrix(n, group):
    idx = jnp.arange(n) // group
    return (idx[:, None] == idx[None, :]).astype(F32) / group


def _rope_tables(n):
    rows = n // GRID_W
    r = jnp.repeat(jnp.arange(rows, dtype=F32), GRID_W)
    col = jnp.tile(jnp.arange(GRID_W, dtype=F32), rows)
    inv = ROPE_THETA ** (-jnp.arange(ROPE_AXIS_FREQS, dtype=F32) / ROPE_AXIS_FREQS)
    ang = jnp.concatenate([r[:, None] * inv, col[:, None] * inv], axis=-1)
    cos, sin = jnp.cos(ang), jnp.sin(ang)
    cos_t = jnp.tile(cos, (1, LANES // (HEAD_DIM // 2)))
    sin_t = jnp.tile(jnp.concatenate([-sin, sin], axis=-1), (1, LANES // HEAD_DIM))
    return cos_t, sin_t


def _layer_weights(l, w_in, pool_w, pool_scale, sgu_g, sgu_w, sgu_b, q_g, k_g, w_out, ln1_g, ln1_b,
                   w_router, w1, w3, w2, ln2_g, ln2_b):
    d = w_in.shape[1]
    q_scale = HEAD_DIM ** -0.5 * math.log2(math.e)
    gain = jnp.concatenate([jnp.tile(q_g[l] * q_scale, N_HEADS), jnp.tile(k_g[l], N_KV_HEADS)])[None, :]
    pool_bd = jax.scipy.linalg.block_diag(*[pool_w[l, g] for g in range(len(POOL_WINDOWS))])
    wr = jnp.pad(w_router[l], ((0, 0), (0, LANES - N_EXPERTS)))
    wr_hi = wr.astype(BF16)
    return dict(
        w_in=w_in[l].astype(BF16),
        gain=gain,
        pool_w=pool_bd.astype(BF16),
        pool_scale=pool_scale[l][None, :],
        gavg256=_group_mean_matrix(SGU_WIDTH, SGU_WIDTH // SGU_HEADS).astype(BF16),
        sgu_g=sgu_g[l].reshape(1, SGU_WIDTH),
        sgu_w=jnp.swapaxes(sgu_w[l], 0, 1).reshape(SGU_CHUNK, SGU_HEADS * SGU_CHUNK).astype(BF16),
        sgu_b=jnp.repeat(sgu_b[l].T, SGU_WIDTH // SGU_HEADS, axis=1),
        w_out=w_out[l].astype(BF16),
        ln1_g=ln1_g[l][None, :], ln1_b=ln1_b[l][None, :],
        wr_hi=wr_hi, wr_lo=(wr - wr_hi.astype(F32)).astype(BF16),
        layer=l, w1=w1, w3=w3, w2=w2,
        ln2_g=ln2_g[l][None, :], ln2_b=ln2_b[l][None, :],
    )


def _moe(x, h2, logits, g2, lw, tri, alpha):
    b, l, d = x.shape
    cap = EC_CAPACITY_FACTOR * l // N_EXPERTS
    rank_t, gate_t, starts = _route(logits, tri, cap)
    starts = starts[:, :, :l // TOKEN_BLOCK].reshape(-1)
    y = _expert_ffn(starts, h2, rank_t, lw["w1"], lw["w3"], lw["w2"], lw["layer"], cap)
    return _combine(starts, y, rank_t, gate_t, x, g2, lw["ln2_g"], lw["ln2_b"], cap, alpha)


def kernel(x, c, ctx, c_ctx, w_mod, b_mod, w_in, pool_w, pool_scale, sgu_g, sgu_w, sgu_b, q_g, k_g, w_out,
           ln1_g, ln1_b, w_router, w1, w3, w2, ln2_g, ln2_b):
    batch, seq, d = x.shape
    ctx_len = ctx.shape[1]
    depth = w_mod.shape[0]
    alpha = (2 * depth) ** 0.25

    cond = jnp.concatenate([c, c_ctx[None, :], jnp.zeros((COND_ROWS - batch - 1, d), F32)], axis=0)
    mod = _modulation(cond, w_mod, b_mod)

    cos, sin = _rope_tables(seq)
    cos_c = jnp.ones((ctx_len, LANES), F32)
    sin_c = jnp.zeros((ctx_len, LANES), F32)
    gavg128 = _group_mean_matrix(LANES, HEAD_DIM).astype(BF16)
    tri = jnp.triu(jnp.ones((TOKEN_BLOCK, TOKEN_BLOCK), F32)).astype(BF16)

    w1, w3, w2 = w1.astype(BF16), w3.astype(BF16), w2.astype(BF16)

    xc = ctx
    for l in range(depth):
        last = l == depth - 1
        lw = _layer_weights(l, w_in, pool_w, pool_scale, sgu_g, sgu_w, sgu_b, q_g, k_g, w_out, ln1_g, ln1_b,
                            w_router, w1, w3, w2, ln2_g, ln2_b)
        m = mod[l].reshape(COND_ROWS, 6, d)
        sh1, sc1, g1, sh2, sc2, g2 = [m[:batch, i][:, None, :] for i in range(6)]
        csh1, csc1, cg1, csh2, csc2, cg2 = [jnp.broadcast_to(m[batch, i][None, None, :], (batch, 1, d))
                                            for i in range(6)]

        zpu, qt, kl, vtl = _in_projection(x, sh1, sc1, lw["w_in"], lw["gain"], cos, sin, gavg128)
        zpu_c, qt_c, kc, vtc = _in_projection(xc, csh1, csc1, lw["w_in"], lw["gain"], cos_c, sin_c, gavg128)

        attn = _attention(qt, jnp.concatenate([kl, kc], axis=1), jnp.concatenate([vtl, vtc], axis=2))
        if not last:
            attn_c = _attention(qt_c, kc, vtc)
            xc, h2c, logits_c = _mixer_output(zpu_c, attn_c, xc, cg1, csh2, csc2, lw, alpha)
            xc = _moe(xc, h2c, logits_c, cg2, lw, tri, alpha)

        x, h2, logits = _mixer_output(zpu, attn, x, g1, sh2, sc2, lw, alpha)
        x = _moe(x, h2, logits, g2, lw, tri, alpha)
    return x
```

```python
import functools
import math

import jax
import jax.numpy as jnp
from jax import lax
from jax.experimental import pallas as pl
from jax.experimental.pallas import tpu as pltpu

F32 = jnp.float32
BF16 = jnp.bfloat16

GRID_W = 64
POOL_WIDTH = 256
POOL_WINDOWS = (2, 4, 8, 16)
POOL_GROUP = 64
POOL_HALO = 8
SGU_WIDTH = 256
SGU_HEADS = 4
SGU_CHUNK = 128
HEAD_DIM = 64
N_HEADS = 8
N_KV_HEADS = 2
KV_GROUP = N_HEADS // N_KV_HEADS
ATTN_WIDTH = N_HEADS * HEAD_DIM
KV_WIDTH = N_KV_HEADS * HEAD_DIM
PU_WIDTH = POOL_WIDTH + 2 * SGU_WIDTH
QK_WIDTH = ATTN_WIDTH + KV_WIDTH
ROPE_THETA = 10000.0
ROPE_AXIS_FREQS = HEAD_DIM // 4
N_EXPERTS = 16
EC_CAPACITY_FACTOR = 2
LN_EPS = 1e-6

LANES = 128
SUBLANES = 8
BF16_ROWS = 16
V_EXT_ROWS = HEAD_DIM + BF16_ROWS
VMEM_LIMIT_BYTES = 56 * 1024 * 1024

ROW_TILE = 512
KV_TILE = 256
ATTN_Q_TILE = 512
TOKEN_BLOCK = 128
GATHER_UNROLL = 4
FFN_ROWS = 512
SMALL_WINDOW = 48
COND_ROWS = 16


def _params(*sem):
    return pltpu.CompilerParams(dimension_semantics=sem, vmem_limit_bytes=VMEM_LIMIT_BYTES)


def _split(x):
    hi = x.astype(BF16)
    lo = (x - hi.astype(F32)).astype(BF16)
    return hi, lo


def _dot(a, b):
    return jnp.dot(a, b, preferred_element_type=F32)


def _dot3(a, b_hi, b_lo):
    a_hi, a_lo = _split(a)
    return _dot(a_hi, b_hi) + _dot(a_hi, b_lo) + _dot(a_lo, b_hi)


def _dot_exact_rhs(a, b):
    a_hi, a_lo = _split(a)
    return _dot(a_hi, b) + _dot(a_lo, b)


def _layer_norm(x):
    mu = jnp.mean(x, axis=-1, keepdims=True)
    xc = x - mu
    var = jnp.mean(xc * xc, axis=-1, keepdims=True)
    return xc * lax.rsqrt(var + LN_EPS)


def _sigmoid(x):
    return 1.0 / (1.0 + jnp.exp(-x))


def _gelu_tanh(x):
    c = math.sqrt(2.0 / math.pi)
    return 0.5 * x * (1.0 + jnp.tanh(c * (x + 0.044715 * (x * x * x))))


def _mod_kernel(cond_ref, w_ref, b_ref, o_ref):
    a = cond_ref[...]
    a = a * _sigmoid(a)
    w_hi, w_lo = _split(w_ref[0])
    o_ref[0] = _dot3(a, w_hi, w_lo) + b_ref[0]


def _modulation(cond, w_mod, b_mod):
    depth, d, n = w_mod.shape
    tn = n // 4
    return pl.pallas_call(
        _mod_kernel,
        grid=(depth, n // tn),
        in_specs=[
            pl.BlockSpec((COND_ROWS, d), lambda l, j: (0, 0)),
            pl.BlockSpec((1, d, tn), lambda l, j: (l, 0, j)),
            pl.BlockSpec((1, 1, tn), lambda l, j: (l, 0, j)),
        ],
        out_specs=pl.BlockSpec((1, COND_ROWS, tn), lambda l, j: (l, 0, j)),
        out_shape=jax.ShapeDtypeStruct((depth, COND_ROWS, n), F32),
        compiler_params=_params("arbitrary", "arbitrary"),
        name="modulation",
    )(cond, w_mod, b_mod.reshape(depth, 1, n))


def _in_kernel(x_ref, sh_ref, sc_ref, w_ref, gain_ref, cos_ref, sin_ref, gavg_ref,
               zpu_ref, qt_ref, k_ref, vt_ref):
    gavg = gavg_ref[...]
    lane = lax.broadcasted_iota(jnp.int32, (KV_TILE, LANES), 1)
    first_half = (lane & (HEAD_DIM - 1)) < (HEAD_DIM // 2)
    row = lax.broadcasted_iota(jnp.int32, (V_EXT_ROWS - HEAD_DIM, KV_TILE), 0)
    ones_row = (row == 0).astype(F32)

    for r in range(x_ref.shape[1] // KV_TILE):
        rows = slice(r * KV_TILE, (r + 1) * KV_TILE)
        h = _layer_norm(x_ref[0, rows, :]) * (1.0 + sc_ref[0]) + sh_ref[0]
        z = _dot(h.astype(BF16), w_ref[...])
        zpu_ref[0, rows, :] = z[:, :PU_WIDTH]

        cos = cos_ref[rows, :]
        sin = sin_ref[rows, :]
        pieces = []
        for c in range(QK_WIDTH // LANES):
            lo = PU_WIDTH + c * LANES
            y = z[:, lo:lo + LANES]
            ms = _dot_exact_rhs(y * y, gavg)
            y = y * lax.rsqrt(ms + LN_EPS) * gain_ref[:, c * LANES:(c + 1) * LANES]
            partner = jnp.where(first_half,
                                pltpu.roll(y, LANES - HEAD_DIM // 2, 1),
                                pltpu.roll(y, HEAD_DIM // 2, 1))
            pieces.append(y * cos + partner * sin)
        q = jnp.concatenate(pieces[:ATTN_WIDTH // LANES], axis=1)
        qt_ref[0, :, rows] = q.T.astype(BF16)
        k_ref[0, rows, :] = pieces[-1].astype(BF16)

        vt = z[:, PU_WIDTH + QK_WIDTH:].T
        for g in range(N_KV_HEADS):
            ext = jnp.concatenate([vt[g * HEAD_DIM:(g + 1) * HEAD_DIM], ones_row], axis=0)
            vt_ref[0, g, r] = ext.astype(BF16)


def _in_projection(x, sh, sc, w_in, gain, cos, sin, gavg):
    b, l, d = x.shape
    n = w_in.shape[1]
    tm = min(ROW_TILE, l)
    nt = l // tm
    sub = tm // KV_TILE
    vec = pl.BlockSpec((1, 1, d), lambda i, j: (i, 0, 0))
    return pl.pallas_call(
        _in_kernel,
        grid=(b, nt),
        in_specs=[
            pl.BlockSpec((1, tm, d), lambda i, j: (i, j, 0)),
            vec, vec,
            pl.BlockSpec((d, n), lambda i, j: (0, 0)),
            pl.BlockSpec((1, QK_WIDTH), lambda i, j: (0, 0)),
            pl.BlockSpec((tm, LANES), lambda i, j: (j, 0)),
            pl.BlockSpec((tm, LANES), lambda i, j: (j, 0)),
            pl.BlockSpec((LANES, LANES), lambda i, j: (0, 0)),
        ],
        out_specs=[
            pl.BlockSpec((1, tm, PU_WIDTH), lambda i, j: (i, j, 0)),
            pl.BlockSpec((1, ATTN_WIDTH, tm), lambda i, j: (i, 0, j)),
            pl.BlockSpec((1, tm, KV_WIDTH), lambda i, j: (i, j, 0)),
            pl.BlockSpec((1, N_KV_HEADS, sub, V_EXT_ROWS, KV_TILE), lambda i, j: (i, 0, j, 0, 0)),
        ],
        out_shape=[
            jax.ShapeDtypeStruct((b, l, PU_WIDTH), F32),
            jax.ShapeDtypeStruct((b, ATTN_WIDTH, l), BF16),
            jax.ShapeDtypeStruct((b, l, KV_WIDTH), BF16),
            jax.ShapeDtypeStruct((b, N_KV_HEADS, l // KV_TILE, V_EXT_ROWS, KV_TILE), BF16),
        ],
        compiler_params=_params("arbitrary", "arbitrary"),
        name="in_projection",
    )(x, sh, sc, w_in, gain, cos, sin, gavg)


def _attn_kernel(qt_ref, k_ref, vt_ref, o_ref, qpad_ref, m_ref, acc_ref, p_ref, scale_ref, *, n_kv):
    tq = qt_ref.shape[2]
    zeros = jnp.zeros((HEAD_DIM, tq), BF16)
    for h in range(N_HEADS):
        qh = qt_ref[0, h * HEAD_DIM:(h + 1) * HEAD_DIM, :]
        qpad_ref[h] = jnp.concatenate([qh, zeros] if h // KV_GROUP == 0 else [zeros, qh], axis=0)
    m_ref[...] = jnp.full(m_ref.shape, -1e30, F32)
    acc_ref[...] = jnp.zeros(acc_ref.shape, F32)

    def scores(c):
        start = pl.multiple_of(c * KV_TILE, KV_TILE)
        kc = k_ref[0, pl.ds(start, KV_TILE), :]
        return [_dot(kc, qpad_ref[h]) for h in range(N_HEADS)]

    def softmax(s, slot):
        for h in range(N_HEADS):
            m = m_ref[h]
            m_new = jnp.maximum(m, jnp.max(s[h], axis=0, keepdims=True))
            p_ref[slot, h] = jnp.exp2(s[h] - m_new).astype(BF16)
            scale_ref[slot, h] = jnp.exp2(m - m_new)
            m_ref[h] = m_new

    def values(c, slot):
        for h in range(N_HEADS):
            acc_ref[h] = acc_ref[h] * scale_ref[slot, h] + _dot(vt_ref[0, h // KV_GROUP, c], p_ref[slot, h])

    def step(c, slot_prev, slot_cur):
        s = scores(c)
        values(c - 1, slot_prev)
        softmax(s, slot_cur)

    softmax(scores(0), 0)
    pairs, leftover = divmod(n_kv - 1, 2)

    def body(i, carry):
        step(2 * i + 1, 0, 1)
        step(2 * i + 2, 1, 0)
        return carry

    lax.fori_loop(0, pairs, body, 0)
    if leftover:
        step(n_kv - 1, 0, 1)
    values(n_kv - 1, leftover)
    blocks = []
    for hp in range(N_HEADS // 2):
        pair = []
        for h in (2 * hp, 2 * hp + 1):
            acc = acc_ref[h]
            pair.append(acc[:HEAD_DIM] / acc[HEAD_DIM:HEAD_DIM + 1])
        blocks.append(jnp.concatenate(pair, axis=0).T)
    o_ref[0] = jnp.concatenate(blocks, axis=1).astype(BF16)


def _attention(qt, k, vt):
    b, _, lq = qt.shape
    lk = k.shape[1]
    n_kv = lk // KV_TILE
    tq = min(ATTN_Q_TILE, lq)
    return pl.pallas_call(
        functools.partial(_attn_kernel, n_kv=n_kv),
        grid=(b, lq // tq),
        in_specs=[
            pl.BlockSpec((1, ATTN_WIDTH, tq), lambda i, j: (i, 0, j)),
            pl.BlockSpec((1, lk, KV_WIDTH), lambda i, j: (i, 0, 0)),
            pl.BlockSpec((1, N_KV_HEADS, n_kv, V_EXT_ROWS, KV_TILE), lambda i, j: (i, 0, 0, 0, 0)),
        ],
        out_specs=pl.BlockSpec((1, tq, ATTN_WIDTH), lambda i, j: (i, j, 0)),
        out_shape=jax.ShapeDtypeStruct((b, lq, ATTN_WIDTH), BF16),
        scratch_shapes=[
            pltpu.VMEM((N_HEADS, 2 * HEAD_DIM, tq), BF16),
            pltpu.VMEM((N_HEADS, 1, tq), F32),
            pltpu.VMEM((N_HEADS, V_EXT_ROWS, tq), F32),
            pltpu.VMEM((2, N_HEADS, KV_TILE, tq), BF16),
            pltpu.VMEM((2, N_HEADS, 1, tq), F32),
        ],
        compiler_params=_params("arbitrary", "arbitrary"),
        name="attention",
    )(qt, k, vt)


def _mix_kernel(zp_ref, zn_ref, z_ref, attn_ref, x_ref, g1_ref, sh2_ref, sc2_ref,
                poolw_ref, pscale_ref, gavg_ref, sgug_ref, sguw_ref, sgub_ref,
                wout_ref, lng_ref, lnb_ref, wr_hi_ref, wr_lo_ref,
                xo_ref, h2_ref, logit_ref, *, seq_len, alpha):
    j = pl.program_id(1)
    nt = pl.num_programs(1)
    z = z_ref[0]
    tm = z.shape[0]
    lane = lax.broadcasted_iota(jnp.int32, (tm, POOL_WIDTH), 1)
    group = lax.shift_right_logical(lane, POOL_GROUP.bit_length() - 1)

    p = z[:, :POOL_WIDTH]
    prev = zp_ref[0] * (j > 0).astype(F32)
    nxt = zn_ref[0] * (j < nt - 1).astype(F32)
    ext = jnp.concatenate([prev, p, nxt], axis=0)
    n_ext = ext.shape[0]
    s2 = ext + pltpu.roll(ext, 1, 0)
    s4 = pltpu.roll(s2, 1, 0) + pltpu.roll(s2, n_ext - 1, 0)
    s8 = pltpu.roll(s4, 2, 0) + pltpu.roll(s4, n_ext - 2, 0)
    s16 = pltpu.roll(s8, 4, 0) + pltpu.roll(s8, n_ext - 4, 0)
    sums = [s[POOL_HALO:POOL_HALO + tm] for s in (s2, s4, s8, s16)]
    wsum = jnp.where(group == 0, sums[0], jnp.where(group == 1, sums[1], jnp.where(group == 2, sums[2], sums[3])))
    half = jnp.where(group == 0, 1, jnp.where(group == 1, 2, jnp.where(group == 2, 4, 8)))
    t = j * tm + lax.broadcasted_iota(jnp.int32, (tm, POOL_WIDTH), 0)
    cnt = jnp.minimum(t + half, seq_len) - jnp.maximum(t - half, 0)
    pin = (wsum / cnt.astype(F32) - p).astype(BF16)

    gavg = gavg_ref[...]
    head = lax.shift_right_logical(lax.broadcasted_iota(jnp.int32, (SGU_CHUNK, SGU_WIDTH), 1),
                                   (SGU_WIDTH // SGU_HEADS).bit_length() - 1)
    split = POOL_WIDTH + SGU_WIDTH

    for r in range(tm // KV_TILE):
        sub = slice(r * KV_TILE, (r + 1) * KV_TILE)
        pool = _dot(pin[sub], poolw_ref[...]) * pscale_ref[...]

        sgu_parts = []
        for c in range(KV_TILE // SGU_CHUNK):
            rows = slice(r * KV_TILE + c * SGU_CHUNK, r * KV_TILE + (c + 1) * SGU_CHUNK)
            u = _gelu_tanh(z[rows, POOL_WIDTH:POOL_WIDTH + SGU_WIDTH])
            v = _gelu_tanh(z[rows, POOL_WIDTH + SGU_WIDTH:])
            mu = _dot_exact_rhs(v, gavg)
            vc = v - mu
            var = _dot_exact_rhs(vc * vc, gavg)
            vn = vc * lax.rsqrt(var + LN_EPS) * sgug_ref[...]
            stacked = jnp.concatenate(
                [jnp.where(head == hd, vn, 0.0).astype(BF16) for hd in range(SGU_HEADS)], axis=0)
            sgu_parts.append(u * (_dot(sguw_ref[...], stacked) + sgub_ref[...]))
        sgu = jnp.concatenate(sgu_parts, axis=0)

        ps = jnp.concatenate([pool, sgu], axis=1).astype(BF16)
        y = _dot(ps, wout_ref[:split, :]) + _dot(attn_ref[0, sub, :], wout_ref[split:, :])
        xn = _layer_norm(alpha * x_ref[0, sub, :] + g1_ref[0] * y) * lng_ref[...] + lnb_ref[...]
        xo_ref[0, sub, :] = xn

        h2 = _layer_norm(xn) * (1.0 + sc2_ref[0]) + sh2_ref[0]
        h2_ref[0, sub, :] = h2.astype(BF16)
        logits_t = _dot3(h2, wr_hi_ref[...], wr_lo_ref[...]).T
        for c in range(KV_TILE // TOKEN_BLOCK):
            logit_ref[0, r * (KV_TILE // TOKEN_BLOCK) + c] = (
                logits_t[:N_EXPERTS, c * TOKEN_BLOCK:(c + 1) * TOKEN_BLOCK])


def _mixer_output(zpu, attn, x, g1, sh2, sc2, lw, alpha):
    b, l, d = x.shape
    tm = min(ROW_TILE, l)
    nt = l // tm
    hb = tm // POOL_HALO
    n_halo = l // POOL_HALO
    vec = pl.BlockSpec((1, 1, d), lambda i, j: (i, 0, 0))

    def full(a):
        nd = a.ndim
        return pl.BlockSpec(a.shape, lambda i, j: (0,) * nd)

    weights = [lw["pool_w"], lw["pool_scale"], lw["gavg256"], lw["sgu_g"], lw["sgu_w"], lw["sgu_b"],
               lw["w_out"], lw["ln1_g"], lw["ln1_b"], lw["wr_hi"], lw["wr_lo"]]
    return pl.pallas_call(
        functools.partial(_mix_kernel, seq_len=l, alpha=alpha),
        grid=(b, nt),
        in_specs=[
            pl.BlockSpec((1, POOL_HALO, POOL_WIDTH), lambda i, j: (i, jnp.maximum(j * hb - 1, 0), 0)),
            pl.BlockSpec((1, POOL_HALO, POOL_WIDTH), lambda i, j: (i, jnp.minimum((j + 1) * hb, n_halo - 1), 0)),
            pl.BlockSpec((1, tm, PU_WIDTH), lambda i, j: (i, j, 0)),
            pl.BlockSpec((1, tm, ATTN_WIDTH), lambda i, j: (i, j, 0)),
            pl.BlockSpec((1, tm, d), lambda i, j: (i, j, 0)),
            vec, vec, vec,
        ] + [full(w) for w in weights],
        out_specs=[
            pl.BlockSpec((1, tm, d), lambda i, j: (i, j, 0)),
            pl.BlockSpec((1, tm, d), lambda i, j: (i, j, 0)),
            pl.BlockSpec((1, tm // TOKEN_BLOCK, N_EXPERTS, TOKEN_BLOCK), lambda i, j: (i, j, 0, 0)),
        ],
        out_shape=[
            jax.ShapeDtypeStruct((b, l, d), F32),
            jax.ShapeDtypeStruct((b, l, d), BF16),
            jax.ShapeDtypeStruct((b, l // TOKEN_BLOCK, N_EXPERTS, TOKEN_BLOCK), F32),
        ],
        compiler_params=_params("arbitrary", "arbitrary"),
        name="mixer_output",
    )(zpu, zpu, zpu, attn, x, g1, sh2, sc2, *weights)


def _route_kernel(logit_ref, tri_ref, earlier_ref, rank_ref, gate_ref, start_ref, bits_ref, *, cap):
    nb = logit_ref.shape[1]
    x = logit_ref[0]
    e = jnp.exp(x - jnp.max(x, axis=1, keepdims=True))
    aff = e / jnp.sum(e, axis=1, keepdims=True)
    gate_ref[0] = aff
    bits_ref[...] = pltpu.bitcast(aff, jnp.int32)

    def count(mask):
        return jnp.sum(jnp.sum(mask.astype(F32), axis=0), axis=1, keepdims=True)

    def bisect(i, thr):
        cand = thr | lax.shift_left(jnp.int32(1), 30 - i)
        return jnp.where(count(bits_ref[...] >= cand) >= float(cap), cand, thr)

    thr = lax.fori_loop(0, 31, bisect, jnp.zeros((N_EXPERTS, 1), jnp.int32))
    bits = bits_ref[...]
    need = float(cap) - count(bits > thr)
    tri = tri_ref[...]
    ones = jnp.ones((TOKEN_BLOCK, TOKEN_BLOCK), BF16)
    earlier = earlier_ref[...]

    def running_count(mask):
        flat = mask.reshape(nb * N_EXPERTS, TOKEN_BLOCK).astype(BF16)
        totals = _dot(flat, ones)
        before = _dot(earlier, totals.astype(BF16))
        return (_dot(flat, tri) + before).reshape(mask.shape), before.reshape(mask.shape)

    eq = (bits == thr).astype(F32)
    sel = jnp.where((bits > thr) | ((eq > 0.0) & (running_count(eq)[0] <= need)), 1.0, 0.0)
    chosen, before = running_count(sel)
    rank_ref[0] = jnp.where(sel > 0.0, chosen - 1.0, -1.0)
    start_ref[0] = before.astype(jnp.int32)


def _route(logits_t, tri, earlier, cap):
    b, nb = logits_t.shape[:2]
    blocks = pl.BlockSpec((1, nb, N_EXPERTS, TOKEN_BLOCK), lambda i: (i, 0, 0, 0))
    table = jax.ShapeDtypeStruct((b, nb, N_EXPERTS, TOKEN_BLOCK), F32)
    return pl.pallas_call(
        functools.partial(_route_kernel, cap=cap),
        grid=(b,),
        in_specs=[blocks, pl.BlockSpec((TOKEN_BLOCK, TOKEN_BLOCK), lambda i: (0, 0)),
                  pl.BlockSpec(earlier.shape, lambda i: (0, 0))],
        out_specs=[blocks, blocks, blocks],
        out_shape=[
            table,
            table,
            jax.ShapeDtypeStruct(table.shape, jnp.int32),
        ],
        scratch_shapes=[pltpu.VMEM((nb, N_EXPERTS, TOKEN_BLOCK), jnp.int32)],
        compiler_params=_params("arbitrary"),
        name="routing",
    )(logits_t, tri, earlier)


def _window_start(s0, cap, window):
    return pl.multiple_of(jnp.minimum(s0 & ~(BF16_ROWS - 1), cap - window), BF16_ROWS)


def _block_fits(starts_ref, idx, is_last, cap, window):
    s0 = starts_ref[idx]
    s1 = jnp.where(is_last, cap, starts_ref[jnp.where(is_last, idx, idx + 1)])
    return (s1 - _window_start(s0, cap, window)) <= window


def _ffn_kernel(starts_ref, h_ref, rank_t_ref, w1_ref, w3_ref, w2_ref, y_ref, xg_ref,
                *, cap, window, small_window):
    e = pl.program_id(1)
    group, nb = rank_t_ref.shape[0], rank_t_ref.shape[1]
    first = pl.program_id(0) * group
    bases = [((first + i) * N_EXPERTS + e) * nb for i in range(group)]
    xg_ref[...] = jnp.zeros(xg_ref.shape, F32)

    def gather_all(win):
        slot = lax.broadcasted_iota(jnp.int32, (win, TOKEN_BLOCK), 0)
        for i in range(group):
            def gather(jb, carry, i=i):
                a = _window_start(starts_ref[bases[i] + jb], cap, win)
                rank = rank_t_ref[i, jb, pl.ds(e, 1), :]
                onehot = ((slot + a).astype(F32) == rank).astype(BF16)
                rows = pl.ds(pl.multiple_of(jb * TOKEN_BLOCK, TOKEN_BLOCK), TOKEN_BLOCK)
                xg_ref[pl.ds(i * cap + a, win), :] += _dot(onehot, h_ref[i, rows, :])
                return carry

            lax.fori_loop(0, nb, gather, 0, unroll=min(nb, GATHER_UNROLL))

    if small_window < window:
        fits = True
        for i in range(group):
            fits = lax.fori_loop(
                0, nb,
                lambda jb, ok, i=i: ok & _block_fits(starts_ref, bases[i] + jb, jb == nb - 1, cap, small_window),
                fits, unroll=True)
        pl.when(fits)(lambda: gather_all(small_window))
        pl.when(jnp.logical_not(fits))(lambda: gather_all(window))
    else:
        gather_all(window)

    xg = xg_ref[...].astype(BF16)
    a = _dot(xg, w1_ref[0, 0])
    hid = (a * _sigmoid(a)) * _dot(xg, w3_ref[0, 0])
    y = _dot(hid.astype(BF16), w2_ref[0, 0]).astype(BF16)
    for i in range(group):
        y_ref[i, 0] = y[i * cap:(i + 1) * cap]


def _expert_ffn(starts, h2, rank_t, w1, w3, w2, layer, cap):
    b, l, d = h2.shape
    f = w1.shape[3]
    nb = l // TOKEN_BLOCK
    window = min(cap, TOKEN_BLOCK + BF16_ROWS)
    small_window = min(cap, SMALL_WINDOW)
    group = max(1, min(b, FFN_ROWS // cap))
    grid_spec = pltpu.PrefetchScalarGridSpec(
        num_scalar_prefetch=1,
        grid=(b // group, N_EXPERTS),
        in_specs=[
            pl.BlockSpec((group, l, d), lambda i, j, s: (i, 0, 0)),
            pl.BlockSpec((group, nb, N_EXPERTS, TOKEN_BLOCK), lambda i, j, s: (i, 0, 0, 0)),
            pl.BlockSpec((1, 1, d, f), lambda i, j, s: (layer, j, 0, 0)),
            pl.BlockSpec((1, 1, d, f), lambda i, j, s: (layer, j, 0, 0)),
            pl.BlockSpec((1, 1, f, d), lambda i, j, s: (layer, j, 0, 0)),
        ],
        out_specs=pl.BlockSpec((group, 1, cap, d), lambda i, j, s: (i, j, 0, 0)),
        scratch_shapes=[pltpu.VMEM((group * cap, d), F32)],
    )
    return pl.pallas_call(
        functools.partial(_ffn_kernel, cap=cap, window=window, small_window=small_window),
        grid_spec=grid_spec,
        out_shape=jax.ShapeDtypeStruct((b, N_EXPERTS, cap, d), BF16),
        compiler_params=_params("arbitrary", "arbitrary"),
        name="expert_ffn",
    )(starts, h2, rank_t, w1, w3, w2)


def _combine_kernel(starts_ref, y_ref, rank_t_ref, gate_t_ref, x_ref, g2_ref, lng_ref, lnb_ref,
                    xo_ref, ml_ref, ystack_ref, *, cap, window, small_window, alpha):
    b = pl.program_id(0)
    jb = pl.program_id(1)
    nb = pl.num_programs(1)
    table = [(b * N_EXPERTS + e) * nb + jb for e in range(N_EXPERTS)]

    def stacked(win):
        slot = lax.broadcasted_iota(jnp.int32, (win, TOKEN_BLOCK), 0)
        hi, lo = [], []
        for e in range(N_EXPERTS):
            a = _window_start(starts_ref[table[e]], cap, win)
            ystack_ref[e * win:(e + 1) * win, :] = y_ref[0, e, pl.ds(a, win), :]
            hit = (slot + a).astype(F32) == rank_t_ref[0, 0, e:e + 1, :]
            g_hi, g_lo = _split(jnp.where(hit, gate_t_ref[0, 0, e:e + 1, :], 0.0))
            hi.append(g_hi)
            lo.append(g_lo)
        ystack = ystack_ref[:N_EXPERTS * win, :]
        contract_rows = (((0,), (0,)), ((), ()))
        ml_ref[...] = (
            lax.dot_general(jnp.concatenate(hi, axis=0), ystack, contract_rows, preferred_element_type=F32)
            + lax.dot_general(jnp.concatenate(lo, axis=0), ystack, contract_rows, preferred_element_type=F32))

    if small_window < window:
        fits = _block_fits(starts_ref, table[0], jb == nb - 1, cap, small_window)
        for e in range(1, N_EXPERTS):
            fits = fits & _block_fits(starts_ref, table[e], jb == nb - 1, cap, small_window)
        pl.when(fits)(lambda: stacked(small_window))
        pl.when(jnp.logical_not(fits))(lambda: stacked(window))
    else:
        stacked(window)
    xo_ref[0] = _layer_norm(alpha * x_ref[0] + g2_ref[0] * ml_ref[...]) * lng_ref[...] + lnb_ref[...]


def _combine(starts, y, rank_t, gate_t, x, g2, ln_g, ln_b, cap, alpha):
    b, l, d = x.shape
    nb = l // TOKEN_BLOCK
    window = min(cap, TOKEN_BLOCK + BF16_ROWS)
    small_window = min(cap, SMALL_WINDOW)
    grid_spec = pltpu.PrefetchScalarGridSpec(
        num_scalar_prefetch=1,
        grid=(b, nb),
        in_specs=[
            pl.BlockSpec((1, N_EXPERTS, cap, d), lambda i, j, s: (i, 0, 0, 0)),
            pl.BlockSpec((1, 1, N_EXPERTS, TOKEN_BLOCK), lambda i, j, s: (i, j, 0, 0)),
            pl.BlockSpec((1, 1, N_EXPERTS, TOKEN_BLOCK), lambda i, j, s: (i, j, 0, 0)),
            pl.BlockSpec((1, TOKEN_BLOCK, d), lambda i, j, s: (i, j, 0)),
            pl.BlockSpec((1, 1, d), lambda i, j, s: (i, 0, 0)),
            pl.BlockSpec((1, d), lambda i, j, s: (0, 0)),
            pl.BlockSpec((1, d), lambda i, j, s: (0, 0)),
        ],
        out_specs=pl.BlockSpec((1, TOKEN_BLOCK, d), lambda i, j, s: (i, j, 0)),
        scratch_shapes=[
            pltpu.VMEM((TOKEN_BLOCK, d), F32),
            pltpu.VMEM((N_EXPERTS * window, d), BF16),
        ],
    )
    return pl.pallas_call(
        functools.partial(_combine_kernel, cap=cap, window=window, small_window=small_window, alpha=alpha),
        grid_spec=grid_spec,
        out_shape=jax.ShapeDtypeStruct((b, l, d), F32),
        compiler_params=_params("arbitrary", "arbitrary"),
        name="moe_combine",
    )(starts, y, rank_t, gate_t, x, g2, ln_g, ln_b)


def _group_mean_matrix(n, group):
    idx = jnp.arange(n) // group
    return (idx[:, None] == idx[None, :]).astype(F32) / group


def _rope_tables(n):
    rows = n // GRID_W
    r = jnp.repeat(jnp.arange(rows, dtype=F32), GRID_W)
    col = jnp.tile(jnp.arange(GRID_W, dtype=F32), rows)
    inv = ROPE_THETA ** (-jnp.arange(ROPE_AXIS_FREQS, dtype=F32) / ROPE_AXIS_FREQS)
    ang = jnp.concatenate([r[:, None] * inv, col[:, None] * inv], axis=-1)
    cos, sin = jnp.cos(ang), jnp.sin(ang)
    cos_t = jnp.tile(cos, (1, LANES // (HEAD_DIM // 2)))
    sin_t = jnp.tile(jnp.concatenate([-sin, sin], axis=-1), (1, LANES // HEAD_DIM))
    return cos_t, sin_t


def _layer_weights(l, w_in, pool_w, pool_scale, sgu_g, sgu_w, sgu_b, q_g, k_g, w_out, ln1_g, ln1_b,
                   w_router, w1, w3, w2, ln2_g, ln2_b):
    d = w_in.shape[1]
    q_scale = HEAD_DIM ** -0.5 * math.log2(math.e)
    gain = jnp.concatenate([jnp.tile(q_g[l] * q_scale, N_HEADS), jnp.tile(k_g[l], N_KV_HEADS)])[None, :]
    pool_bd = jax.scipy.linalg.block_diag(*[pool_w[l, g] for g in range(len(POOL_WINDOWS))])
    wr = jnp.pad(w_router[l], ((0, 0), (0, LANES - N_EXPERTS)))
    wr_hi = wr.astype(BF16)
    return dict(
        w_in=w_in[l].astype(BF16),
        gain=gain,
        pool_w=pool_bd.astype(BF16),
        pool_scale=pool_scale[l][None, :],
        gavg256=_group_mean_matrix(SGU_WIDTH, SGU_WIDTH // SGU_HEADS).astype(BF16),
        sgu_g=sgu_g[l].reshape(1, SGU_WIDTH),
        sgu_w=jnp.swapaxes(sgu_w[l], 0, 1).reshape(SGU_CHUNK, SGU_HEADS * SGU_CHUNK).astype(BF16),
        sgu_b=jnp.repeat(sgu_b[l].T, SGU_WIDTH // SGU_HEADS, axis=1),
        w_out=w_out[l].astype(BF16),
        ln1_g=ln1_g[l][None, :], ln1_b=ln1_b[l][None, :],
        wr_hi=wr_hi, wr_lo=(wr - wr_hi.astype(F32)).astype(BF16),
        layer=l, w1=w1, w3=w3, w2=w2,
        ln2_g=ln2_g[l][None, :], ln2_b=ln2_b[l][None, :],
    )


def _moe(x, h2, logits, g2, lw, tri, alpha):
    b, l, d = x.shape
    cap = EC_CAPACITY_FACTOR * l // N_EXPERTS
    nb = l // TOKEN_BLOCK
    same_expert = jnp.eye(N_EXPERTS, dtype=F32)
    earlier = jnp.kron(jnp.tril(jnp.ones((nb, nb), F32), -1), same_expert).astype(BF16)
    rank_t, gate_t, starts = _route(logits, tri, earlier, cap)
    starts = jnp.swapaxes(starts[:, :, :, 0], 1, 2).reshape(-1)
    y = _expert_ffn(starts, h2, rank_t, lw["w1"], lw["w3"], lw["w2"], lw["layer"], cap)
    return _combine(starts, y, rank_t, gate_t, x, g2, lw["ln2_g"], lw["ln2_b"], cap, alpha)


def kernel(x, c, ctx, c_ctx, w_mod, b_mod, w_in, pool_w, pool_scale, sgu_g, sgu_w, sgu_b, q_g, k_g, w_out,
           ln1_g, ln1_b, w_router, w1, w3, w2, ln2_g, ln2_b):
    batch, seq, d = x.shape
    ctx_len = ctx.shape[1]
    depth = w_mod.shape[0]
    alpha = (2 * depth) ** 0.25

    cond = jnp.concatenate([c, c_ctx[None, :], jnp.zeros((COND_ROWS - batch - 1, d), F32)], axis=0)
    mod = _modulation(cond, w_mod, b_mod)

    cos, sin = _rope_tables(seq)
    cos_c = jnp.ones((ctx_len, LANES), F32)
    sin_c = jnp.zeros((ctx_len, LANES), F32)
    gavg128 = _group_mean_matrix(LANES, HEAD_DIM).astype(BF16)
    tri = jnp.triu(jnp.ones((TOKEN_BLOCK, TOKEN_BLOCK), F32)).astype(BF16)

    w1, w3, w2 = w1.astype(BF16), w3.astype(BF16), w2.astype(BF16)

    xc = ctx
    for l in range(depth):
        last = l == depth - 1
        lw = _layer_weights(l, w_in, pool_w, pool_scale, sgu_g, sgu_w, sgu_b, q_g, k_g, w_out, ln1_g, ln1_b,
                            w_router, w1, w3, w2, ln2_g, ln2_b)
        m = mod[l].reshape(COND_ROWS, 6, d)
        sh1, sc1, g1, sh2, sc2, g2 = [m[:batch, i][:, None, :] for i in range(6)]
        csh1, csc1, cg1, csh2, csc2, cg2 = [jnp.broadcast_to(m[batch, i][None, None, :], (batch, 1, d))
                                            for i in range(6)]

        zpu, qt, kl, vtl = _in_projection(x, sh1, sc1, lw["w_in"], lw["gain"], cos, sin, gavg128)
        zpu_c, qt_c, kc, vtc = _in_projection(xc, csh1, csc1, lw["w_in"], lw["gain"], cos_c, sin_c, gavg128)

        attn = _attention(qt, jnp.concatenate([kl, kc], axis=1), jnp.concatenate([vtl, vtc], axis=2))
        if not last:
            attn_c = _attention(qt_c, kc, vtc)
            xc, h2c, logits_c = _mixer_output(zpu_c, attn_c, xc, cg1, csh2, csc2, lw, alpha)
            xc = _moe(xc, h2c, logits_c, cg2, lw, tri, alpha)

        x, h2, logits = _mixer_output(zpu, attn, x, g1, sh2, sc2, lw, alpha)
        x = _moe(x, h2, logits, g2, lw, tri, alpha)
    return x
```

```python
import functools
import math

import jax
import jax.numpy as jnp
from jax import lax
from jax.experimental import pallas as pl
from jax.experimental.pallas import tpu as pltpu

F32 = jnp.float32
BF16 = jnp.bfloat16

GRID_W = 64
POOL_WIDTH = 256
POOL_WINDOWS = (2, 4, 8, 16)
POOL_GROUP = 64
POOL_HALO = 8
SGU_WIDTH = 256
SGU_HEADS = 4
SGU_CHUNK = 128
HEAD_DIM = 64
N_HEADS = 8
N_KV_HEADS = 2
KV_GROUP = N_HEADS // N_KV_HEADS
ATTN_WIDTH = N_HEADS * HEAD_DIM
KV_WIDTH = N_KV_HEADS * HEAD_DIM
PU_WIDTH = POOL_WIDTH + 2 * SGU_WIDTH
QK_WIDTH = ATTN_WIDTH + KV_WIDTH
ROPE_THETA = 10000.0
ROPE_AXIS_FREQS = HEAD_DIM // 4
N_EXPERTS = 16
EC_CAPACITY_FACTOR = 2
LN_EPS = 1e-6

LANES = 128
BF16_ROWS = 16
V_EXT_ROWS = HEAD_DIM + BF16_ROWS
VMEM_LIMIT_BYTES = 56 * 1024 * 1024
F32_ORDER_BITS = 31

ROW_TILE = 1024
KV_TILE = 256
ATTN_Q_TILE = 512
TOKEN_BLOCK = 128
GATHER_UNROLL = 32
FFN_ROWS = 512
SMALL_WINDOW = 48
COND_ROWS = 16
MOD_COLUMN_TILES = 4
SCORE_FLOOR = -1e30


def _params(*sem):
    return pltpu.CompilerParams(dimension_semantics=sem, vmem_limit_bytes=VMEM_LIMIT_BYTES)


def _split(x):
    hi = x.astype(BF16)
    lo = (x - hi.astype(F32)).astype(BF16)
    return hi, lo


def _dot(a, b):
    return jnp.dot(a, b, preferred_element_type=F32)


def _dot3(a, b_hi, b_lo):
    a_hi, a_lo = _split(a)
    return _dot(a_hi, b_hi) + _dot(a_hi, b_lo) + _dot(a_lo, b_hi)


def _dot_exact_rhs(a, b):
    a_hi, a_lo = _split(a)
    return _dot(a_hi, b) + _dot(a_lo, b)


def _layer_norm(x):
    mu = jnp.mean(x, axis=-1, keepdims=True)
    xc = x - mu
    var = jnp.mean(xc * xc, axis=-1, keepdims=True)
    return xc * lax.rsqrt(var + LN_EPS)


def _sigmoid(x):
    return 1.0 / (1.0 + jnp.exp(-x))


def _gelu_tanh(x):
    c = math.sqrt(2.0 / math.pi)
    return 0.5 * x * (1.0 + jnp.tanh(c * (x + 0.044715 * (x * x * x))))


def _mod_kernel(cond_ref, w_ref, b_ref, o_ref):
    a = cond_ref[...]
    a = a * _sigmoid(a)
    w_hi, w_lo = _split(w_ref[0])
    o_ref[0] = _dot3(a, w_hi, w_lo) + b_ref[0]


def _modulation(cond, w_mod, b_mod):
    depth, d, n = w_mod.shape
    tn = n // MOD_COLUMN_TILES
    return pl.pallas_call(
        _mod_kernel,
        grid=(depth, n // tn),
        in_specs=[
            pl.BlockSpec((COND_ROWS, d), lambda l, j: (0, 0)),
            pl.BlockSpec((1, d, tn), lambda l, j: (l, 0, j)),
            pl.BlockSpec((1, 1, tn), lambda l, j: (l, 0, j)),
        ],
        out_specs=pl.BlockSpec((1, COND_ROWS, tn), lambda l, j: (l, 0, j)),
        out_shape=jax.ShapeDtypeStruct((depth, COND_ROWS, n), F32),
        compiler_params=_params("arbitrary", "arbitrary"),
        name="modulation",
    )(cond, w_mod, b_mod.reshape(depth, 1, n))


def _in_kernel(x_ref, sh_ref, sc_ref, w_ref, gain_ref, cos_ref, sin_ref, gavg_ref,
               zpu_ref, qt_ref, k_ref, vt_ref):
    gavg = gavg_ref[...]
    lane = lax.broadcasted_iota(jnp.int32, (KV_TILE, LANES), 1)
    first_half = (lane & (HEAD_DIM - 1)) < (HEAD_DIM // 2)
    row = lax.broadcasted_iota(jnp.int32, (V_EXT_ROWS - HEAD_DIM, KV_TILE), 0)
    ones_row = (row == 0).astype(F32)

    for r in range(x_ref.shape[1] // KV_TILE):
        rows = slice(r * KV_TILE, (r + 1) * KV_TILE)
        h = _layer_norm(x_ref[0, rows, :]) * (1.0 + sc_ref[0]) + sh_ref[0]
        z = _dot(h.astype(BF16), w_ref[...])
        zpu_ref[0, rows, :] = z[:, :PU_WIDTH]

        cos = cos_ref[rows, :]
        sin = sin_ref[rows, :]
        pieces = []
        for c in range(QK_WIDTH // LANES):
            lo = PU_WIDTH + c * LANES
            y = z[:, lo:lo + LANES]
            ms = _dot_exact_rhs(y * y, gavg)
            y = y * lax.rsqrt(ms + LN_EPS) * gain_ref[:, c * LANES:(c + 1) * LANES]
            partner = jnp.where(first_half,
                                pltpu.roll(y, LANES - HEAD_DIM // 2, 1),
                                pltpu.roll(y, HEAD_DIM // 2, 1))
            pieces.append(y * cos + partner * sin)
        q = jnp.concatenate(pieces[:ATTN_WIDTH // LANES], axis=1)
        qt_ref[0, :, rows] = q.T.astype(BF16)
        k_ref[0, rows, :] = pieces[-1].astype(BF16)

        vt = z[:, PU_WIDTH + QK_WIDTH:].T
        for g in range(N_KV_HEADS):
            ext = jnp.concatenate([vt[g * HEAD_DIM:(g + 1) * HEAD_DIM], ones_row], axis=0)
            vt_ref[0, g, r] = ext.astype(BF16)


def _in_projection(x, sh, sc, w_in, gain, cos, sin, gavg):
    b, l, d = x.shape
    n = w_in.shape[1]
    tm = min(ROW_TILE, l)
    nt = l // tm
    sub = tm // KV_TILE
    vec = pl.BlockSpec((1, 1, d), lambda i, j: (i, 0, 0))
    return pl.pallas_call(
        _in_kernel,
        grid=(b, nt),
        in_specs=[
            pl.BlockSpec((1, tm, d), lambda i, j: (i, j, 0)),
            vec, vec,
            pl.BlockSpec((d, n), lambda i, j: (0, 0)),
            pl.BlockSpec((1, QK_WIDTH), lambda i, j: (0, 0)),
            pl.BlockSpec((tm, LANES), lambda i, j: (j, 0)),
            pl.BlockSpec((tm, LANES), lambda i, j: (j, 0)),
            pl.BlockSpec((LANES, LANES), lambda i, j: (0, 0)),
        ],
        out_specs=[
            pl.BlockSpec((1, tm, PU_WIDTH), lambda i, j: (i, j, 0)),
            pl.BlockSpec((1, ATTN_WIDTH, tm), lambda i, j: (i, 0, j)),
            pl.BlockSpec((1, tm, KV_WIDTH), lambda i, j: (i, j, 0)),
            pl.BlockSpec((1, N_KV_HEADS, sub, V_EXT_ROWS, KV_TILE), lambda i, j: (i, 0, j, 0, 0)),
        ],
        out_shape=[
            jax.ShapeDtypeStruct((b, l, PU_WIDTH), F32),
            jax.ShapeDtypeStruct((b, ATTN_WIDTH, l), BF16),
            jax.ShapeDtypeStruct((b, l, KV_WIDTH), BF16),
            jax.ShapeDtypeStruct((b, N_KV_HEADS, l // KV_TILE, V_EXT_ROWS, KV_TILE), BF16),
        ],
        compiler_params=_params("arbitrary", "arbitrary"),
        name="in_projection",
    )(x, sh, sc, w_in, gain, cos, sin, gavg)


def _attn_kernel(qt_ref, k_ref, vt_ref, o_ref, qpad_ref, m_ref, acc_ref, p_ref, scale_ref, *, n_kv):
    tq = qt_ref.shape[2]
    zeros = jnp.zeros((HEAD_DIM, tq), BF16)
    for h in range(N_HEADS):
        qh = qt_ref[0, h * HEAD_DIM:(h + 1) * HEAD_DIM, :]
        qpad_ref[h] = jnp.concatenate([qh, zeros] if h // KV_GROUP == 0 else [zeros, qh], axis=0)
    m_ref[...] = jnp.full(m_ref.shape, SCORE_FLOOR, F32)
    acc_ref[...] = jnp.zeros(acc_ref.shape, F32)

    def scores(c):
        start = pl.multiple_of(c * KV_TILE, KV_TILE)
        kc = k_ref[0, pl.ds(start, KV_TILE), :]
        return [_dot(kc, qpad_ref[h]) for h in range(N_HEADS)]

    def softmax(s, slot):
        for h in range(N_HEADS):
            m = m_ref[h]
            m_new = jnp.maximum(m, jnp.max(s[h], axis=0, keepdims=True))
            p_ref[slot, h] = jnp.exp2(s[h] - m_new).astype(BF16)
            scale_ref[slot, h] = jnp.exp2(m - m_new)
            m_ref[h] = m_new

    def values(c, slot):
        for h in range(N_HEADS):
            acc_ref[h] = acc_ref[h] * scale_ref[slot, h] + _dot(vt_ref[0, h // KV_GROUP, c], p_ref[slot, h])

    def step(c, slot_prev, slot_cur):
        s = scores(c)
        values(c - 1, slot_prev)
        softmax(s, slot_cur)

    softmax(scores(0), 0)
    pairs, leftover = divmod(n_kv - 1, 2)

    def body(i, carry):
        step(2 * i + 1, 0, 1)
        step(2 * i + 2, 1, 0)
        return carry

    lax.fori_loop(0, pairs, body, 0)
    if leftover:
        step(n_kv - 1, 0, 1)
    values(n_kv - 1, leftover)
    blocks = []
    for hp in range(N_HEADS // 2):
        pair = []
        for h in (2 * hp, 2 * hp + 1):
            acc = acc_ref[h]
            pair.append(acc[:HEAD_DIM] / acc[HEAD_DIM:HEAD_DIM + 1])
        blocks.append(jnp.concatenate(pair, axis=0).T)
    o_ref[0] = jnp.concatenate(blocks, axis=1).astype(BF16)


def _attention(qt, k, vt):
    b, _, lq = qt.shape
    lk = k.shape[1]
    n_kv = lk // KV_TILE
    tq = min(ATTN_Q_TILE, lq)
    return pl.pallas_call(
        functools.partial(_attn_kernel, n_kv=n_kv),
        grid=(b, lq // tq),
        in_specs=[
            pl.BlockSpec((1, ATTN_WIDTH, tq), lambda i, j: (i, 0, j)),
            pl.BlockSpec((1, lk, KV_WIDTH), lambda i, j: (i, 0, 0)),
            pl.BlockSpec((1, N_KV_HEADS, n_kv, V_EXT_ROWS, KV_TILE), lambda i, j: (i, 0, 0, 0, 0)),
        ],
        out_specs=pl.BlockSpec((1, tq, ATTN_WIDTH), lambda i, j: (i, j, 0)),
        out_shape=jax.ShapeDtypeStruct((b, lq, ATTN_WIDTH), BF16),
        scratch_shapes=[
            pltpu.VMEM((N_HEADS, 2 * HEAD_DIM, tq), BF16),
            pltpu.VMEM((N_HEADS, 1, tq), F32),
            pltpu.VMEM((N_HEADS, V_EXT_ROWS, tq), F32),
            pltpu.VMEM((2, N_HEADS, KV_TILE, tq), BF16),
            pltpu.VMEM((2, N_HEADS, 1, tq), F32),
        ],
        compiler_params=_params("arbitrary", "arbitrary"),
        name="attention",
    )(qt, k, vt)


def _mix_kernel(zp_ref, zn_ref, z_ref, attn_ref, x_ref, g1_ref, sh2_ref, sc2_ref,
                poolw_ref, pscale_ref, gavg_ref, sgug_ref, sguw_ref, sgub_ref,
                wout_ref, lng_ref, lnb_ref, wr_hi_ref, wr_lo_ref,
                xo_ref, h2_ref, logit_ref, *, seq_len, alpha):
    j = pl.program_id(1)
    nt = pl.num_programs(1)
    z = z_ref[0]
    tm = z.shape[0]
    lane = lax.broadcasted_iota(jnp.int32, (tm, POOL_WIDTH), 1)
    group = lax.shift_right_logical(lane, POOL_GROUP.bit_length() - 1)

    p = z[:, :POOL_WIDTH]
    prev = zp_ref[0] * (j > 0).astype(F32)
    nxt = zn_ref[0] * (j < nt - 1).astype(F32)
    ext = jnp.concatenate([prev, p, nxt], axis=0)
    n_ext = ext.shape[0]
    s2 = ext + pltpu.roll(ext, 1, 0)
    s4 = pltpu.roll(s2, 1, 0) + pltpu.roll(s2, n_ext - 1, 0)
    s8 = pltpu.roll(s4, 2, 0) + pltpu.roll(s4, n_ext - 2, 0)
    s16 = pltpu.roll(s8, 4, 0) + pltpu.roll(s8, n_ext - 4, 0)
    sums = [s[POOL_HALO:POOL_HALO + tm] for s in (s2, s4, s8, s16)]
    wsum = jnp.where(group == 0, sums[0], jnp.where(group == 1, sums[1], jnp.where(group == 2, sums[2], sums[3])))
    half = jnp.where(group == 0, 1, jnp.where(group == 1, 2, jnp.where(group == 2, 4, 8)))
    t = j * tm + lax.broadcasted_iota(jnp.int32, (tm, POOL_WIDTH), 0)
    cnt = jnp.minimum(t + half, seq_len) - jnp.maximum(t - half, 0)
    pin = (wsum / cnt.astype(F32) - p).astype(BF16)

    gavg = gavg_ref[...]
    head = lax.shift_right_logical(lax.broadcasted_iota(jnp.int32, (SGU_CHUNK, SGU_WIDTH), 1),
                                   (SGU_WIDTH // SGU_HEADS).bit_length() - 1)
    split = POOL_WIDTH + SGU_WIDTH

    for r in range(tm // KV_TILE):
        sub = slice(r * KV_TILE, (r + 1) * KV_TILE)
        pool = _dot(pin[sub], poolw_ref[...]) * pscale_ref[...]

        sgu_parts = []
        for c in range(KV_TILE // SGU_CHUNK):
            rows = slice(r * KV_TILE + c * SGU_CHUNK, r * KV_TILE + (c + 1) * SGU_CHUNK)
            u = _gelu_tanh(z[rows, POOL_WIDTH:POOL_WIDTH + SGU_WIDTH])
            v = _gelu_tanh(z[rows, POOL_WIDTH + SGU_WIDTH:])
            mu = _dot_exact_rhs(v, gavg)
            vc = v - mu
            var = _dot_exact_rhs(vc * vc, gavg)
            vn = vc * lax.rsqrt(var + LN_EPS) * sgug_ref[...]
            stacked = jnp.concatenate(
                [jnp.where(head == hd, vn, 0.0).astype(BF16) for hd in range(SGU_HEADS)], axis=0)
            sgu_parts.append(u * (_dot(sguw_ref[...], stacked) + sgub_ref[...]))
        sgu = jnp.concatenate(sgu_parts, axis=0)

        ps = jnp.concatenate([pool, sgu], axis=1).astype(BF16)
        y = _dot(ps, wout_ref[:split, :]) + _dot(attn_ref[0, sub, :], wout_ref[split:, :])
        xn = _layer_norm(alpha * x_ref[0, sub, :] + g1_ref[0] * y) * lng_ref[...] + lnb_ref[...]
        xo_ref[0, sub, :] = xn

        h2 = _layer_norm(xn) * (1.0 + sc2_ref[0]) + sh2_ref[0]
        h2_ref[0, sub, :] = h2.astype(BF16)
        logits_t = _dot3(h2, wr_hi_ref[...], wr_lo_ref[...]).T
        for c in range(KV_TILE // TOKEN_BLOCK):
            logit_ref[0, r * (KV_TILE // TOKEN_BLOCK) + c] = (
                logits_t[:N_EXPERTS, c * TOKEN_BLOCK:(c + 1) * TOKEN_BLOCK])


def _mixer_output(zpu, attn, x, g1, sh2, sc2, lw, alpha):
    b, l, d = x.shape
    tm = min(ROW_TILE, l)
    nt = l // tm
    hb = tm // POOL_HALO
    n_halo = l // POOL_HALO
    vec = pl.BlockSpec((1, 1, d), lambda i, j: (i, 0, 0))

    def full(a):
        nd = a.ndim
        return pl.BlockSpec(a.shape, lambda i, j: (0,) * nd)

    weights = [lw["pool_w"], lw["pool_scale"], lw["gavg256"], lw["sgu_g"], lw["sgu_w"], lw["sgu_b"],
               lw["w_out"], lw["ln1_g"], lw["ln1_b"], lw["wr_hi"], lw["wr_lo"]]
    return pl.pallas_call(
        functools.partial(_mix_kernel, seq_len=l, alpha=alpha),
        grid=(b, nt),
        in_specs=[
            pl.BlockSpec((1, POOL_HALO, POOL_WIDTH), lambda i, j: (i, jnp.maximum(j * hb - 1, 0), 0)),
            pl.BlockSpec((1, POOL_HALO, POOL_WIDTH), lambda i, j: (i, jnp.minimum((j + 1) * hb, n_halo - 1), 0)),
            pl.BlockSpec((1, tm, PU_WIDTH), lambda i, j: (i, j, 0)),
            pl.BlockSpec((1, tm, ATTN_WIDTH), lambda i, j: (i, j, 0)),
            pl.BlockSpec((1, tm, d), lambda i, j: (i, j, 0)),
            vec, vec, vec,
        ] + [full(w) for w in weights],
        out_specs=[
            pl.BlockSpec((1, tm, d), lambda i, j: (i, j, 0)),
            pl.BlockSpec((1, tm, d), lambda i, j: (i, j, 0)),
            pl.BlockSpec((1, tm // TOKEN_BLOCK, N_EXPERTS, TOKEN_BLOCK), lambda i, j: (i, j, 0, 0)),
        ],
        out_shape=[
            jax.ShapeDtypeStruct((b, l, d), F32),
            jax.ShapeDtypeStruct((b, l, d), BF16),
            jax.ShapeDtypeStruct((b, l // TOKEN_BLOCK, N_EXPERTS, TOKEN_BLOCK), F32),
        ],
        compiler_params=_params("arbitrary", "arbitrary"),
        name="mixer_output",
    )(zpu, zpu, zpu, attn, x, g1, sh2, sc2, *weights)


def _route_kernel(logit_ref, tri_ref, earlier_ref, rank_ref, gate_ref, start_ref, bits_ref, *, cap):
    nb = logit_ref.shape[1]
    x = logit_ref[0]
    e = jnp.exp(x - jnp.max(x, axis=1, keepdims=True))
    aff = e / jnp.sum(e, axis=1, keepdims=True)
    gate_ref[0] = aff
    bits_ref[...] = pltpu.bitcast(aff, jnp.int32)

    def count(mask):
        return jnp.sum(jnp.sum(mask.astype(F32), axis=0), axis=1, keepdims=True)

    def bisect(i, thr):
        cand = thr | lax.shift_left(jnp.int32(1), F32_ORDER_BITS - 1 - i)
        return jnp.where(count(bits_ref[...] >= cand) >= float(cap), cand, thr)

    thr = lax.fori_loop(0, F32_ORDER_BITS, bisect, jnp.zeros((N_EXPERTS, 1), jnp.int32))
    bits = bits_ref[...]
    need = float(cap) - count(bits > thr)
    tri = tri_ref[...]
    ones = jnp.ones((TOKEN_BLOCK, TOKEN_BLOCK), BF16)
    earlier = earlier_ref[...]

    def running_count(mask):
        flat = mask.reshape(nb * N_EXPERTS, TOKEN_BLOCK).astype(BF16)
        totals = _dot(flat, ones)
        before = _dot(earlier, totals.astype(BF16))
        return (_dot(flat, tri) + before).reshape(mask.shape), before.reshape(mask.shape)

    eq = (bits == thr).astype(F32)
    sel = jnp.where((bits > thr) | ((eq > 0.0) & (running_count(eq)[0] <= need)), 1.0, 0.0)
    chosen, before = running_count(sel)
    rank_ref[0] = jnp.where(sel > 0.0, chosen - 1.0, -1.0)
    start_ref[0] = before.astype(jnp.int32)


def _route(logits_t, tri, earlier, cap):
    b, nb = logits_t.shape[:2]
    blocks = pl.BlockSpec((1, nb, N_EXPERTS, TOKEN_BLOCK), lambda i: (i, 0, 0, 0))
    table = jax.ShapeDtypeStruct((b, nb, N_EXPERTS, TOKEN_BLOCK), F32)
    return pl.pallas_call(
        functools.partial(_route_kernel, cap=cap),
        grid=(b,),
        in_specs=[blocks, pl.BlockSpec((TOKEN_BLOCK, TOKEN_BLOCK), lambda i: (0, 0)),
                  pl.BlockSpec(earlier.shape, lambda i: (0, 0))],
        out_specs=[blocks, blocks, blocks],
        out_shape=[
            table,
            table,
            jax.ShapeDtypeStruct(table.shape, jnp.int32),
        ],
        scratch_shapes=[pltpu.VMEM((nb, N_EXPERTS, TOKEN_BLOCK), jnp.int32)],
        compiler_params=_params("arbitrary"),
        name="routing",
    )(logits_t, tri, earlier)


def _window_start(s0, cap, window):
    return pl.multiple_of(jnp.minimum(s0 & ~(BF16_ROWS - 1), cap - window), BF16_ROWS)


def _block_fits(starts_ref, idx, is_last, cap, window):
    s0 = starts_ref[idx]
    s1 = jnp.where(is_last, cap, starts_ref[jnp.where(is_last, idx, idx + 1)])
    return (s1 - _window_start(s0, cap, window)) <= window


def _ffn_kernel(starts_ref, h_ref, rank_t_ref, w1_ref, w3_ref, w2_ref, y_ref, xg_ref,
                *, cap, window, small_window):
    e = pl.program_id(1)
    group, nb = rank_t_ref.shape[0], rank_t_ref.shape[1]
    first = pl.program_id(0) * group
    bases = [((first + i) * N_EXPERTS + e) * nb for i in range(group)]
    xg_ref[...] = jnp.zeros(xg_ref.shape, F32)

    def gather_all(win):
        slot = lax.broadcasted_iota(jnp.int32, (win, TOKEN_BLOCK), 0)
        for i in range(group):
            def gather(jb, carry, i=i):
                a = _window_start(starts_ref[bases[i] + jb], cap, win)
                rank = rank_t_ref[i, jb, pl.ds(e, 1), :]
                onehot = ((slot + a).astype(F32) == rank).astype(BF16)
                rows = pl.ds(pl.multiple_of(jb * TOKEN_BLOCK, TOKEN_BLOCK), TOKEN_BLOCK)
                xg_ref[pl.ds(i * cap + a, win), :] += _dot(onehot, h_ref[i, rows, :])
                return carry

            lax.fori_loop(0, nb, gather, 0, unroll=min(nb, GATHER_UNROLL))

    if small_window < window:
        fits = True
        for i in range(group):
            fits = lax.fori_loop(
                0, nb,
                lambda jb, ok, i=i: ok & _block_fits(starts_ref, bases[i] + jb, jb == nb - 1, cap, small_window),
                fits, unroll=True)
        pl.when(fits)(lambda: gather_all(small_window))
        pl.when(jnp.logical_not(fits))(lambda: gather_all(window))
    else:
        gather_all(window)

    xg = xg_ref[...].astype(BF16)
    a = _dot(xg, w1_ref[0, 0])
    hid = (a * _sigmoid(a)) * _dot(xg, w3_ref[0, 0])
    y = _dot(hid.astype(BF16), w2_ref[0, 0]).astype(BF16)
    for i in range(group):
        y_ref[i, 0] = y[i * cap:(i + 1) * cap]


def _expert_ffn(starts, h2, rank_t, w1, w3, w2, layer, cap):
    b, l, d = h2.shape
    f = w1.shape[3]
    nb = l // TOKEN_BLOCK
    window = min(cap, TOKEN_BLOCK + BF16_ROWS)
    small_window = min(cap, SMALL_WINDOW)
    group = max(1, min(b, FFN_ROWS // cap))
    assert b % group == 0
    grid_spec = pltpu.PrefetchScalarGridSpec(
        num_scalar_prefetch=1,
        grid=(b // group, N_EXPERTS),
        in_specs=[
            pl.BlockSpec((group, l, d), lambda i, j, s: (i, 0, 0)),
            pl.BlockSpec((group, nb, N_EXPERTS, TOKEN_BLOCK), lambda i, j, s: (i, 0, 0, 0)),
            pl.BlockSpec((1, 1, d, f), lambda i, j, s: (layer, j, 0, 0)),
            pl.BlockSpec((1, 1, d, f), lambda i, j, s: (layer, j, 0, 0)),
            pl.BlockSpec((1, 1, f, d), lambda i, j, s: (layer, j, 0, 0)),
        ],
        out_specs=pl.BlockSpec((group, 1, cap, d), lambda i, j, s: (i, j, 0, 0)),
        scratch_shapes=[pltpu.VMEM((group * cap, d), F32)],
    )
    return pl.pallas_call(
        functools.partial(_ffn_kernel, cap=cap, window=window, small_window=small_window),
        grid_spec=grid_spec,
        out_shape=jax.ShapeDtypeStruct((b, N_EXPERTS, cap, d), BF16),
        compiler_params=_params("arbitrary", "arbitrary"),
        name="expert_ffn",
    )(starts, h2, rank_t, w1, w3, w2)


def _combine_kernel(starts_ref, y_ref, rank_t_ref, gate_t_ref, x_ref, g2_ref, lng_ref, lnb_ref,
                    xo_ref, ml_ref, ystack_ref, *, cap, window, small_window, alpha):
    b = pl.program_id(0)
    jb = pl.program_id(1)
    nb = pl.num_programs(1)
    table = [(b * N_EXPERTS + e) * nb + jb for e in range(N_EXPERTS)]

    def stacked(win):
        slot = lax.broadcasted_iota(jnp.int32, (win, TOKEN_BLOCK), 0)
        hi, lo = [], []
        for e in range(N_EXPERTS):
            a = _window_start(starts_ref[table[e]], cap, win)
            ystack_ref[e * win:(e + 1) * win, :] = y_ref[0, e, pl.ds(a, win), :]
            hit = (slot + a).astype(F32) == rank_t_ref[0, 0, e:e + 1, :]
            g_hi, g_lo = _split(jnp.where(hit, gate_t_ref[0, 0, e:e + 1, :], 0.0))
            hi.append(g_hi)
            lo.append(g_lo)
        ystack = ystack_ref[:N_EXPERTS * win, :]
        contract_rows = (((0,), (0,)), ((), ()))
        ml_ref[...] = (
            lax.dot_general(jnp.concatenate(hi, axis=0), ystack, contract_rows, preferred_element_type=F32)
            + lax.dot_general(jnp.concatenate(lo, axis=0), ystack, contract_rows, preferred_element_type=F32))

    if small_window < window:
        fits = _block_fits(starts_ref, table[0], jb == nb - 1, cap, small_window)
        for e in range(1, N_EXPERTS):
            fits = fits & _block_fits(starts_ref, table[e], jb == nb - 1, cap, small_window)
        pl.when(fits)(lambda: stacked(small_window))
        pl.when(jnp.logical_not(fits))(lambda: stacked(window))
    else:
        stacked(window)
    xo_ref[0] = _layer_norm(alpha * x_ref[0] + g2_ref[0] * ml_ref[...]) * lng_ref[...] + lnb_ref[...]


def _combine(starts, y, rank_t, gate_t, x, g2, ln_g, ln_b, cap, alpha):
    b, l, d = x.shape
    nb = l // TOKEN_BLOCK
    window = min(cap, TOKEN_BLOCK + BF16_ROWS)
    small_window = min(cap, SMALL_WINDOW)
    grid_spec = pltpu.PrefetchScalarGridSpec(
        num_scalar_prefetch=1,
        grid=(b, nb),
        in_specs=[
            pl.BlockSpec((1, N_EXPERTS, cap, d), lambda i, j, s: (i, 0, 0, 0)),
            pl.BlockSpec((1, 1, N_EXPERTS, TOKEN_BLOCK), lambda i, j, s: (i, j, 0, 0)),
            pl.BlockSpec((1, 1, N_EXPERTS, TOKEN_BLOCK), lambda i, j, s: (i, j, 0, 0)),
            pl.BlockSpec((1, TOKEN_BLOCK, d), lambda i, j, s: (i, j, 0)),
            pl.BlockSpec((1, 1, d), lambda i, j, s: (i, 0, 0)),
            pl.BlockSpec((1, d), lambda i, j, s: (0, 0)),
            pl.BlockSpec((1, d), lambda i, j, s: (0, 0)),
        ],
        out_specs=pl.BlockSpec((1, TOKEN_BLOCK, d), lambda i, j, s: (i, j, 0)),
        scratch_shapes=[
            pltpu.VMEM((TOKEN_BLOCK, d), F32),
            pltpu.VMEM((N_EXPERTS * window, d), BF16),
        ],
    )
    return pl.pallas_call(
        functools.partial(_combine_kernel, cap=cap, window=window, small_window=small_window, alpha=alpha),
        grid_spec=grid_spec,
        out_shape=jax.ShapeDtypeStruct((b, l, d), F32),
        compiler_params=_params("arbitrary", "arbitrary"),
        name="moe_combine",
    )(starts, y, rank_t, gate_t, x, g2, ln_g, ln_b)


def _group_mean_matrix(n, group):
    idx = jnp.arange(n) // group
    return (idx[:, None] == idx[None, :]).astype(F32) / group


def _rope_tables(n):
    rows = n // GRID_W
    r = jnp.repeat(jnp.arange(rows, dtype=F32), GRID_W)
    col = jnp.tile(jnp.arange(GRID_W, dtype=F32), rows)
    inv = ROPE_THETA ** (-jnp.arange(ROPE_AXIS_FREQS, dtype=F32) / ROPE_AXIS_FREQS)
    ang = jnp.concatenate([r[:, None] * inv, col[:, None] * inv], axis=-1)
    cos, sin = jnp.cos(ang), jnp.sin(ang)
    cos_t = jnp.tile(cos, (1, LANES // (HEAD_DIM // 2)))
    sin_t = jnp.tile(jnp.concatenate([-sin, sin], axis=-1), (1, LANES // HEAD_DIM))
    return cos_t, sin_t


def _layer_weights(l, w_in, pool_w, pool_scale, sgu_g, sgu_w, sgu_b, q_g, k_g, w_out, ln1_g, ln1_b,
                   w_router, w1, w3, w2, ln2_g, ln2_b):
    d = w_in.shape[1]
    q_scale = HEAD_DIM ** -0.5 * math.log2(math.e)
    gain = jnp.concatenate([jnp.tile(q_g[l] * q_scale, N_HEADS), jnp.tile(k_g[l], N_KV_HEADS)])[None, :]
    pool_bd = jax.scipy.linalg.block_diag(*[pool_w[l, g] for g in range(len(POOL_WINDOWS))])
    wr = jnp.pad(w_router[l], ((0, 0), (0, LANES - N_EXPERTS)))
    wr_hi = wr.astype(BF16)
    return dict(
        w_in=w_in[l].astype(BF16),
        gain=gain,
        pool_w=pool_bd.astype(BF16),
        pool_scale=pool_scale[l][None, :],
        gavg256=_group_mean_matrix(SGU_WIDTH, SGU_WIDTH // SGU_HEADS).astype(BF16),
        sgu_g=sgu_g[l].reshape(1, SGU_WIDTH),
        sgu_w=jnp.swapaxes(sgu_w[l], 0, 1).reshape(SGU_CHUNK, SGU_HEADS * SGU_CHUNK).astype(BF16),
        sgu_b=jnp.repeat(sgu_b[l].T, SGU_WIDTH // SGU_HEADS, axis=1),
        w_out=w_out[l].astype(BF16),
        ln1_g=ln1_g[l][None, :], ln1_b=ln1_b[l][None, :],
        wr_hi=wr_hi, wr_lo=(wr - wr_hi.astype(F32)).astype(BF16),
        layer=l, w1=w1, w3=w3, w2=w2,
        ln2_g=ln2_g[l][None, :], ln2_b=ln2_b[l][None, :],
    )


def _moe(x, h2, logits, g2, lw, tri, alpha):
    b, l, d = x.shape
    cap = EC_CAPACITY_FACTOR * l // N_EXPERTS
    nb = l // TOKEN_BLOCK
    same_expert = jnp.eye(N_EXPERTS, dtype=F32)
    earlier = jnp.kron(jnp.tril(jnp.ones((nb, nb), F32), -1), same_expert).astype(BF16)
    rank_t, gate_t, starts = _route(logits, tri, earlier, cap)
    starts = jnp.swapaxes(starts[:, :, :, 0], 1, 2).reshape(-1)
    y = _expert_ffn(starts, h2, rank_t, lw["w1"], lw["w3"], lw["w2"], lw["layer"], cap)
    return _combine(starts, y, rank_t, gate_t, x, g2, lw["ln2_g"], lw["ln2_b"], cap, alpha)


def kernel(x, c, ctx, c_ctx, w_mod, b_mod, w_in, pool_w, pool_scale, sgu_g, sgu_w, sgu_b, q_g, k_g, w_out,
           ln1_g, ln1_b, w_router, w1, w3, w2, ln2_g, ln2_b):
    batch, seq, d = x.shape
    ctx_len = ctx.shape[1]
    depth = w_mod.shape[0]
    alpha = (2 * depth) ** 0.25
    assert batch + 1 <= COND_ROWS and d % LANES == 0
    assert seq % ROW_TILE == 0 and seq % GRID_W == 0 and seq % ATTN_Q_TILE == 0
    assert ctx_len % KV_TILE == 0 and ctx_len <= ROW_TILE
    for tokens in (seq, ctx_len):
        cap = EC_CAPACITY_FACTOR * tokens // N_EXPERTS
        assert cap % BF16_ROWS == 0 and tokens % TOKEN_BLOCK == 0

    cond = jnp.concatenate([c, c_ctx[None, :], jnp.zeros((COND_ROWS - batch - 1, d), F32)], axis=0)
    mod = _modulation(cond, w_mod, b_mod)

    cos, sin = _rope_tables(seq)
    cos_c = jnp.ones((ctx_len, LANES), F32)
    sin_c = jnp.zeros((ctx_len, LANES), F32)
    gavg128 = _group_mean_matrix(LANES, HEAD_DIM).astype(BF16)
    tri = jnp.triu(jnp.ones((TOKEN_BLOCK, TOKEN_BLOCK), F32)).astype(BF16)

    w1, w3, w2 = w1.astype(BF16), w3.astype(BF16), w2.astype(BF16)

    xc = ctx
    for l in range(depth):
        last = l == depth - 1
        lw = _layer_weights(l, w_in, pool_w, pool_scale, sgu_g, sgu_w, sgu_b, q_g, k_g, w_out, ln1_g, ln1_b,
                            w_router, w1, w3, w2, ln2_g, ln2_b)
        m = mod[l].reshape(COND_ROWS, 6, d)
        sh1, sc1, g1, sh2, sc2, g2 = [m[:batch, i][:, None, :] for i in range(6)]
        csh1, csc1, cg1, csh2, csc2, cg2 = [jnp.broadcast_to(m[batch, i][None, None, :], (batch, 1, d))
                                            for i in range(6)]

        zpu, qt, kl, vtl = _in_projection(x, sh1, sc1, lw["w_in"], lw["gain"], cos, sin, gavg128)
        zpu_c, qt_c, kc, vtc = _in_projection(xc, csh1, csc1, lw["w_in"], lw["gain"], cos_c, sin_c, gavg128)

        attn = _attention(qt, jnp.concatenate([kl, kc], axis=1), jnp.concatenate([vtl, vtc], axis=2))
        if not last:
            attn_c = _attention(qt_c, kc, vtc)
            xc, h2c, logits_c = _mixer_output(zpu_c, attn_c, xc, cg1, csh2, csc2, lw, alpha)
            xc = _moe(xc, h2c, logits_c, cg2, lw, tri, alpha)

        x, h2, logits = _mixer_output(zpu, attn, x, g1, sh2, sc2, lw, alpha)
        x = _moe(x, h2, logits, g2, lw, tri, alpha)
    return x
```

```python
import functools
import math

import jax
import jax.numpy as jnp
from jax import lax
from jax.experimental import pallas as pl
from jax.experimental.pallas import tpu as pltpu

F32 = jnp.float32
BF16 = jnp.bfloat16

GRID_W = 64
POOL_WIDTH = 256
POOL_WINDOWS = (2, 4, 8, 16)
POOL_GROUP = 64
POOL_HALO = 8
SGU_WIDTH = 256
SGU_HEADS = 4
SGU_CHUNK = 128
HEAD_DIM = 64
N_HEADS = 8
N_KV_HEADS = 2
KV_GROUP = N_HEADS // N_KV_HEADS
ATTN_WIDTH = N_HEADS * HEAD_DIM
KV_WIDTH = N_KV_HEADS * HEAD_DIM
PU_WIDTH = POOL_WIDTH + 2 * SGU_WIDTH
QK_WIDTH = ATTN_WIDTH + KV_WIDTH
ROPE_THETA = 10000.0
ROPE_AXIS_FREQS = HEAD_DIM // 4
N_EXPERTS = 16
EC_CAPACITY_FACTOR = 2
LN_EPS = 1e-6

LANES = 128
BF16_ROWS = 16
V_EXT_ROWS = HEAD_DIM + BF16_ROWS
VMEM_LIMIT_BYTES = 56 * 1024 * 1024
F32_ORDER_BITS = 31

ROW_TILE = 1024
KV_TILE = 256
ATTN_Q_TILE = 512
TOKEN_BLOCK = 128
GATHER_UNROLL = 32
COMBINE_BLOCKS = 2
FFN_ROWS = 512
SMALL_WINDOW = 48
COND_ROWS = 16
MOD_COLUMN_TILES = 4
SCORE_FLOOR = -1e30


def _params(*sem):
    return pltpu.CompilerParams(dimension_semantics=sem, vmem_limit_bytes=VMEM_LIMIT_BYTES)


def _split(x):
    hi = x.astype(BF16)
    lo = (x - hi.astype(F32)).astype(BF16)
    return hi, lo


def _dot(a, b):
    return jnp.dot(a, b, preferred_element_type=F32)


def _dot3(a, b_hi, b_lo):
    a_hi, a_lo = _split(a)
    return _dot(a_hi, b_hi) + _dot(a_hi, b_lo) + _dot(a_lo, b_hi)


def _dot_exact_rhs(a, b):
    a_hi, a_lo = _split(a)
    return _dot(a_hi, b) + _dot(a_lo, b)


def _layer_norm(x):
    mu = jnp.mean(x, axis=-1, keepdims=True)
    xc = x - mu
    var = jnp.mean(xc * xc, axis=-1, keepdims=True)
    return xc * lax.rsqrt(var + LN_EPS)


def _sigmoid(x):
    return 1.0 / (1.0 + jnp.exp(-x))


def _gelu_tanh(x):
    c = math.sqrt(2.0 / math.pi)
    return 0.5 * x * (1.0 + jnp.tanh(c * (x + 0.044715 * (x * x * x))))


def _mod_kernel(cond_ref, w_ref, b_ref, o_ref):
    a = cond_ref[...]
    a = a * _sigmoid(a)
    w_hi, w_lo = _split(w_ref[0])
    o_ref[0] = _dot3(a, w_hi, w_lo) + b_ref[0]


def _modulation(cond, w_mod, b_mod):
    depth, d, n = w_mod.shape
    tn = n // MOD_COLUMN_TILES
    return pl.pallas_call(
        _mod_kernel,
        grid=(depth, n // tn),
        in_specs=[
            pl.BlockSpec((COND_ROWS, d), lambda l, j: (0, 0)),
            pl.BlockSpec((1, d, tn), lambda l, j: (l, 0, j)),
            pl.BlockSpec((1, 1, tn), lambda l, j: (l, 0, j)),
        ],
        out_specs=pl.BlockSpec((1, COND_ROWS, tn), lambda l, j: (l, 0, j)),
        out_shape=jax.ShapeDtypeStruct((depth, COND_ROWS, n), F32),
        compiler_params=_params("arbitrary", "arbitrary"),
        name="modulation",
    )(cond, w_mod, b_mod.reshape(depth, 1, n))


def _in_kernel(x_ref, sh_ref, sc_ref, w_ref, gain_ref, cos_ref, sin_ref, gavg_ref,
               zpu_ref, qt_ref, k_ref, vt_ref):
    gavg = gavg_ref[...]
    lane = lax.broadcasted_iota(jnp.int32, (KV_TILE, LANES), 1)
    first_half = (lane & (HEAD_DIM - 1)) < (HEAD_DIM // 2)
    row = lax.broadcasted_iota(jnp.int32, (V_EXT_ROWS - HEAD_DIM, KV_TILE), 0)
    ones_row = (row == 0).astype(F32)

    for r in range(x_ref.shape[1] // KV_TILE):
        rows = slice(r * KV_TILE, (r + 1) * KV_TILE)
        h = _layer_norm(x_ref[0, rows, :]) * (1.0 + sc_ref[0]) + sh_ref[0]
        z = _dot(h.astype(BF16), w_ref[...])
        zpu_ref[0, rows, :] = z[:, :PU_WIDTH]

        cos = cos_ref[rows, :]
        sin = sin_ref[rows, :]
        pieces = []
        for c in range(QK_WIDTH // LANES):
            lo = PU_WIDTH + c * LANES
            y = z[:, lo:lo + LANES]
            ms = _dot_exact_rhs(y * y, gavg)
            y = y * lax.rsqrt(ms + LN_EPS) * gain_ref[:, c * LANES:(c + 1) * LANES]
            partner = jnp.where(first_half,
                                pltpu.roll(y, LANES - HEAD_DIM // 2, 1),
                                pltpu.roll(y, HEAD_DIM // 2, 1))
            pieces.append(y * cos + partner * sin)
        q = jnp.concatenate(pieces[:ATTN_WIDTH // LANES], axis=1)
        qt_ref[0, :, rows] = q.T.astype(BF16)
        k_ref[0, rows, :] = pieces[-1].astype(BF16)

        vt = z[:, PU_WIDTH + QK_WIDTH:].T
        for g in range(N_KV_HEADS):
            ext = jnp.concatenate([vt[g * HEAD_DIM:(g + 1) * HEAD_DIM], ones_row], axis=0)
            vt_ref[0, g, r] = ext.astype(BF16)


def _in_projection(x, sh, sc, w_in, gain, cos, sin, gavg):
    b, l, d = x.shape
    n = w_in.shape[1]
    tm = min(ROW_TILE, l)
    nt = l // tm
    sub = tm // KV_TILE
    vec = pl.BlockSpec((1, 1, d), lambda i, j: (i, 0, 0))
    return pl.pallas_call(
        _in_kernel,
        grid=(b, nt),
        in_specs=[
            pl.BlockSpec((1, tm, d), lambda i, j: (i, j, 0)),
            vec, vec,
            pl.BlockSpec((d, n), lambda i, j: (0, 0)),
            pl.BlockSpec((1, QK_WIDTH), lambda i, j: (0, 0)),
            pl.BlockSpec((tm, LANES), lambda i, j: (j, 0)),
            pl.BlockSpec((tm, LANES), lambda i, j: (j, 0)),
            pl.BlockSpec((LANES, LANES), lambda i, j: (0, 0)),
        ],
        out_specs=[
            pl.BlockSpec((1, tm, PU_WIDTH), lambda i, j: (i, j, 0)),
            pl.BlockSpec((1, ATTN_WIDTH, tm), lambda i, j: (i, 0, j)),
            pl.BlockSpec((1, tm, KV_WIDTH), lambda i, j: (i, j, 0)),
            pl.BlockSpec((1, N_KV_HEADS, sub, V_EXT_ROWS, KV_TILE), lambda i, j: (i, 0, j, 0, 0)),
        ],
        out_shape=[
            jax.ShapeDtypeStruct((b, l, PU_WIDTH), F32),
            jax.ShapeDtypeStruct((b, ATTN_WIDTH, l), BF16),
            jax.ShapeDtypeStruct((b, l, KV_WIDTH), BF16),
            jax.ShapeDtypeStruct((b, N_KV_HEADS, l // KV_TILE, V_EXT_ROWS, KV_TILE), BF16),
        ],
        compiler_params=_params("arbitrary", "arbitrary"),
        name="in_projection",
    )(x, sh, sc, w_in, gain, cos, sin, gavg)


def _attn_kernel(qt_ref, k_ref, vt_ref, o_ref, qpad_ref, m_ref, acc_ref, p_ref, scale_ref, *, n_kv):
    tq = qt_ref.shape[2]
    zeros = jnp.zeros((HEAD_DIM, tq), BF16)
    for h in range(N_HEADS):
        qh = qt_ref[0, h * HEAD_DIM:(h + 1) * HEAD_DIM, :]
        qpad_ref[h] = jnp.concatenate([qh, zeros] if h // KV_GROUP == 0 else [zeros, qh], axis=0)
    m_ref[...] = jnp.full(m_ref.shape, SCORE_FLOOR, F32)
    acc_ref[...] = jnp.zeros(acc_ref.shape, F32)

    def scores(c):
        start = pl.multiple_of(c * KV_TILE, KV_TILE)
        kc = k_ref[0, pl.ds(start, KV_TILE), :]
        return [_dot(kc, qpad_ref[h]) for h in range(N_HEADS)]

    def softmax(s, slot):
        for h in range(N_HEADS):
            m = m_ref[h]
            m_new = jnp.maximum(m, jnp.max(s[h], axis=0, keepdims=True))
            p_ref[slot, h] = jnp.exp2(s[h] - m_new).astype(BF16)
            scale_ref[slot, h] = jnp.exp2(m - m_new)
            m_ref[h] = m_new

    def values(c, slot):
        for h in range(N_HEADS):
            acc_ref[h] = acc_ref[h] * scale_ref[slot, h] + _dot(vt_ref[0, h // KV_GROUP, c], p_ref[slot, h])

    def step(c, slot_prev, slot_cur):
        s = scores(c)
        values(c - 1, slot_prev)
        softmax(s, slot_cur)

    softmax(scores(0), 0)
    pairs, leftover = divmod(n_kv - 1, 2)

    def body(i, carry):
        step(2 * i + 1, 0, 1)
        step(2 * i + 2, 1, 0)
        return carry

    lax.fori_loop(0, pairs, body, 0)
    if leftover:
        step(n_kv - 1, 0, 1)
    values(n_kv - 1, leftover)
    blocks = []
    for hp in range(N_HEADS // 2):
        pair = []
        for h in (2 * hp, 2 * hp + 1):
            acc = acc_ref[h]
            pair.append(acc[:HEAD_DIM] / acc[HEAD_DIM:HEAD_DIM + 1])
        blocks.append(jnp.concatenate(pair, axis=0).T)
    o_ref[0] = jnp.concatenate(blocks, axis=1).astype(BF16)


def _attention(qt, k, vt):
    b, _, lq = qt.shape
    lk = k.shape[1]
    n_kv = lk // KV_TILE
    tq = min(ATTN_Q_TILE, lq)
    return pl.pallas_call(
        functools.partial(_attn_kernel, n_kv=n_kv),
        grid=(b, lq // tq),
        in_specs=[
            pl.BlockSpec((1, ATTN_WIDTH, tq), lambda i, j: (i, 0, j)),
            pl.BlockSpec((1, lk, KV_WIDTH), lambda i, j: (i, 0, 0)),
            pl.BlockSpec((1, N_KV_HEADS, n_kv, V_EXT_ROWS, KV_TILE), lambda i, j: (i, 0, 0, 0, 0)),
        ],
        out_specs=pl.BlockSpec((1, tq, ATTN_WIDTH), lambda i, j: (i, j, 0)),
        out_shape=jax.ShapeDtypeStruct((b, lq, ATTN_WIDTH), BF16),
        scratch_shapes=[
            pltpu.VMEM((N_HEADS, 2 * HEAD_DIM, tq), BF16),
            pltpu.VMEM((N_HEADS, 1, tq), F32),
            pltpu.VMEM((N_HEADS, V_EXT_ROWS, tq), F32),
            pltpu.VMEM((2, N_HEADS, KV_TILE, tq), BF16),
            pltpu.VMEM((2, N_HEADS, 1, tq), F32),
        ],
        compiler_params=_params("arbitrary", "arbitrary"),
        name="attention",
    )(qt, k, vt)


def _mix_kernel(zp_ref, zn_ref, z_ref, attn_ref, x_ref, g1_ref, sh2_ref, sc2_ref,
                poolw_ref, pscale_ref, gavg_ref, sgug_ref, sguw_ref, sgub_ref,
                wout_ref, lng_ref, lnb_ref, wr_hi_ref, wr_lo_ref,
                xo_ref, h2_ref, logit_ref, *, seq_len, alpha):
    j = pl.program_id(1)
    nt = pl.num_programs(1)
    z = z_ref[0]
    tm = z.shape[0]
    lane = lax.broadcasted_iota(jnp.int32, (tm, POOL_WIDTH), 1)
    group = lax.shift_right_logical(lane, POOL_GROUP.bit_length() - 1)

    p = z[:, :POOL_WIDTH]
    prev = zp_ref[0] * (j > 0).astype(F32)
    nxt = zn_ref[0] * (j < nt - 1).astype(F32)
    ext = jnp.concatenate([prev, p, nxt], axis=0)
    n_ext = ext.shape[0]
    s2 = ext + pltpu.roll(ext, 1, 0)
    s4 = pltpu.roll(s2, 1, 0) + pltpu.roll(s2, n_ext - 1, 0)
    s8 = pltpu.roll(s4, 2, 0) + pltpu.roll(s4, n_ext - 2, 0)
    s16 = pltpu.roll(s8, 4, 0) + pltpu.roll(s8, n_ext - 4, 0)
    sums = [s[POOL_HALO:POOL_HALO + tm] for s in (s2, s4, s8, s16)]
    wsum = jnp.where(group == 0, sums[0], jnp.where(group == 1, sums[1], jnp.where(group == 2, sums[2], sums[3])))
    half = jnp.where(group == 0, 1, jnp.where(group == 1, 2, jnp.where(group == 2, 4, 8)))
    t = j * tm + lax.broadcasted_iota(jnp.int32, (tm, POOL_WIDTH), 0)
    cnt = jnp.minimum(t + half, seq_len) - jnp.maximum(t - half, 0)
    pin = (wsum / cnt.astype(F32) - p).astype(BF16)

    gavg = gavg_ref[...]
    head = lax.shift_right_logical(lax.broadcasted_iota(jnp.int32, (SGU_CHUNK, SGU_WIDTH), 1),
                                   (SGU_WIDTH // SGU_HEADS).bit_length() - 1)
    split = POOL_WIDTH + SGU_WIDTH

    for r in range(tm // KV_TILE):
        sub = slice(r * KV_TILE, (r + 1) * KV_TILE)
        pool = _dot(pin[sub], poolw_ref[...]) * pscale_ref[...]

        sgu_parts = []
        for c in range(KV_TILE // SGU_CHUNK):
            rows = slice(r * KV_TILE + c * SGU_CHUNK, r * KV_TILE + (c + 1) * SGU_CHUNK)
            u = _gelu_tanh(z[rows, POOL_WIDTH:POOL_WIDTH + SGU_WIDTH])
            v = _gelu_tanh(z[rows, POOL_WIDTH + SGU_WIDTH:])
            mu = _dot_exact_rhs(v, gavg)
            vc = v - mu
            var = _dot_exact_rhs(vc * vc, gavg)
            vn = vc * lax.rsqrt(var + LN_EPS) * sgug_ref[...]
            stacked = jnp.concatenate(
                [jnp.where(head == hd, vn, 0.0).astype(BF16) for hd in range(SGU_HEADS)], axis=0)
            sgu_parts.append(u * (_dot(sguw_ref[...], stacked) + sgub_ref[...]))
        sgu = jnp.concatenate(sgu_parts, axis=0)

        ps = jnp.concatenate([pool, sgu], axis=1).astype(BF16)
        y = _dot(ps, wout_ref[:split, :]) + _dot(attn_ref[0, sub, :], wout_ref[split:, :])
        xn = _layer_norm(alpha * x_ref[0, sub, :] + g1_ref[0] * y) * lng_ref[...] + lnb_ref[...]
        xo_ref[0, sub, :] = xn

        h2 = _layer_norm(xn) * (1.0 + sc2_ref[0]) + sh2_ref[0]
        h2_ref[0, sub, :] = h2.astype(BF16)
        logits_t = _dot3(h2, wr_hi_ref[...], wr_lo_ref[...]).T
        for c in range(KV_TILE // TOKEN_BLOCK):
            logit_ref[0, r * (KV_TILE // TOKEN_BLOCK) + c] = (
                logits_t[:N_EXPERTS, c * TOKEN_BLOCK:(c + 1) * TOKEN_BLOCK])


def _mixer_output(zpu, attn, x, g1, sh2, sc2, lw, alpha):
    b, l, d = x.shape
    tm = min(ROW_TILE, l)
    nt = l // tm
    hb = tm // POOL_HALO
    n_halo = l // POOL_HALO
    vec = pl.BlockSpec((1, 1, d), lambda i, j: (i, 0, 0))

    def full(a):
        nd = a.ndim
        return pl.BlockSpec(a.shape, lambda i, j: (0,) * nd)

    weights = [lw["pool_w"], lw["pool_scale"], lw["gavg256"], lw["sgu_g"], lw["sgu_w"], lw["sgu_b"],
               lw["w_out"], lw["ln1_g"], lw["ln1_b"], lw["wr_hi"], lw["wr_lo"]]
    return pl.pallas_call(
        functools.partial(_mix_kernel, seq_len=l, alpha=alpha),
        grid=(b, nt),
        in_specs=[
            pl.BlockSpec((1, POOL_HALO, POOL_WIDTH), lambda i, j: (i, jnp.maximum(j * hb - 1, 0), 0)),
            pl.BlockSpec((1, POOL_HALO, POOL_WIDTH), lambda i, j: (i, jnp.minimum((j + 1) * hb, n_halo - 1), 0)),
            pl.BlockSpec((1, tm, PU_WIDTH), lambda i, j: (i, j, 0)),
            pl.BlockSpec((1, tm, ATTN_WIDTH), lambda i, j: (i, j, 0)),
            pl.BlockSpec((1, tm, d), lambda i, j: (i, j, 0)),
            vec, vec, vec,
        ] + [full(w) for w in weights],
        out_specs=[
            pl.BlockSpec((1, tm, d), lambda i, j: (i, j, 0)),
            pl.BlockSpec((1, tm, d), lambda i, j: (i, j, 0)),
            pl.BlockSpec((1, tm // TOKEN_BLOCK, N_EXPERTS, TOKEN_BLOCK), lambda i, j: (i, j, 0, 0)),
        ],
        out_shape=[
            jax.ShapeDtypeStruct((b, l, d), F32),
            jax.ShapeDtypeStruct((b, l, d), BF16),
            jax.ShapeDtypeStruct((b, l // TOKEN_BLOCK, N_EXPERTS, TOKEN_BLOCK), F32),
        ],
        compiler_params=_params("arbitrary", "arbitrary"),
        name="mixer_output",
    )(zpu, zpu, zpu, attn, x, g1, sh2, sc2, *weights)


def _route_kernel(logit_ref, tri_ref, earlier_ref, rank_ref, gate_ref, start_ref, bits_ref, *, cap):
    nb = logit_ref.shape[1]
    x = logit_ref[0]
    e = jnp.exp(x - jnp.max(x, axis=1, keepdims=True))
    aff = e / jnp.sum(e, axis=1, keepdims=True)
    gate_ref[0] = aff
    bits_ref[...] = pltpu.bitcast(aff, jnp.int32)

    def count(mask):
        return jnp.sum(jnp.sum(mask.astype(F32), axis=0), axis=1, keepdims=True)

    def bisect(i, thr):
        cand = thr | lax.shift_left(jnp.int32(1), F32_ORDER_BITS - 1 - i)
        return jnp.where(count(bits_ref[...] >= cand) >= float(cap), cand, thr)

    thr = lax.fori_loop(0, F32_ORDER_BITS, bisect, jnp.zeros((N_EXPERTS, 1), jnp.int32))
    bits = bits_ref[...]
    need = float(cap) - count(bits > thr)
    tri = tri_ref[...]
    ones = jnp.ones((TOKEN_BLOCK, TOKEN_BLOCK), BF16)
    earlier = earlier_ref[...]

    def running_count(mask):
        flat = mask.reshape(nb * N_EXPERTS, TOKEN_BLOCK).astype(BF16)
        totals = _dot(flat, ones)
        before = _dot(earlier, totals.astype(BF16))
        return (_dot(flat, tri) + before).reshape(mask.shape), before.reshape(mask.shape)

    eq = (bits == thr).astype(F32)
    sel = jnp.where((bits > thr) | ((eq > 0.0) & (running_count(eq)[0] <= need)), 1.0, 0.0)
    chosen, before = running_count(sel)
    rank_ref[0] = jnp.where(sel > 0.0, chosen - 1.0, -1.0)
    start_ref[0] = before.astype(jnp.int32)


def _route(logits_t, tri, earlier, cap):
    b, nb = logits_t.shape[:2]
    blocks = pl.BlockSpec((1, nb, N_EXPERTS, TOKEN_BLOCK), lambda i: (i, 0, 0, 0))
    table = jax.ShapeDtypeStruct((b, nb, N_EXPERTS, TOKEN_BLOCK), F32)
    return pl.pallas_call(
        functools.partial(_route_kernel, cap=cap),
        grid=(b,),
        in_specs=[blocks, pl.BlockSpec((TOKEN_BLOCK, TOKEN_BLOCK), lambda i: (0, 0)),
                  pl.BlockSpec(earlier.shape, lambda i: (0, 0))],
        out_specs=[blocks, blocks, blocks],
        out_shape=[
            table,
            table,
            jax.ShapeDtypeStruct(table.shape, jnp.int32),
        ],
        scratch_shapes=[pltpu.VMEM((nb, N_EXPERTS, TOKEN_BLOCK), jnp.int32)],
        compiler_params=_params("arbitrary"),
        name="routing",
    )(logits_t, tri, earlier)


def _window_start(s0, cap, window):
    return pl.multiple_of(jnp.minimum(s0 & ~(BF16_ROWS - 1), cap - window), BF16_ROWS)


def _block_fits(starts_ref, idx, is_last, cap, window):
    s0 = starts_ref[idx]
    s1 = jnp.where(is_last, cap, starts_ref[jnp.where(is_last, idx, idx + 1)])
    return (s1 - _window_start(s0, cap, window)) <= window


def _ffn_kernel(starts_ref, h_ref, rank_t_ref, w1_ref, w3_ref, w2_ref, y_ref, xg_ref,
                *, cap, window, small_window):
    e = pl.program_id(1)
    group, nb = rank_t_ref.shape[0], rank_t_ref.shape[1]
    first = pl.program_id(0) * group
    bases = [((first + i) * N_EXPERTS + e) * nb for i in range(group)]
    xg_ref[...] = jnp.zeros(xg_ref.shape, F32)

    def gather_all(win):
        slot = lax.broadcasted_iota(jnp.int32, (win, TOKEN_BLOCK), 0)
        for i in range(group):
            def gather(jb, carry, i=i):
                a = _window_start(starts_ref[bases[i] + jb], cap, win)
                rank = rank_t_ref[i, jb, pl.ds(e, 1), :]
                onehot = ((slot + a).astype(F32) == rank).astype(BF16)
                rows = pl.ds(pl.multiple_of(jb * TOKEN_BLOCK, TOKEN_BLOCK), TOKEN_BLOCK)
                xg_ref[pl.ds(i * cap + a, win), :] += _dot(onehot, h_ref[i, rows, :])
                return carry

            lax.fori_loop(0, nb, gather, 0, unroll=min(nb, GATHER_UNROLL))

    if small_window < window:
        fits = True
        for i in range(group):
            fits = lax.fori_loop(
                0, nb,
                lambda jb, ok, i=i: ok & _block_fits(starts_ref, bases[i] + jb, jb == nb - 1, cap, small_window),
                fits, unroll=True)
        pl.when(fits)(lambda: gather_all(small_window))
        pl.when(jnp.logical_not(fits))(lambda: gather_all(window))
    else:
        gather_all(window)

    xg = xg_ref[...].astype(BF16)
    a = _dot(xg, w1_ref[0, 0])
    hid = (a * _sigmoid(a)) * _dot(xg, w3_ref[0, 0])
    y = _dot(hid.astype(BF16), w2_ref[0, 0]).astype(BF16)
    for i in range(group):
        y_ref[i, 0] = y[i * cap:(i + 1) * cap]


def _expert_ffn(starts, h2, rank_t, w1, w3, w2, layer, cap):
    b, l, d = h2.shape
    f = w1.shape[3]
    nb = l // TOKEN_BLOCK
    window = min(cap, TOKEN_BLOCK + BF16_ROWS)
    small_window = min(cap, SMALL_WINDOW)
    group = max(1, min(b, FFN_ROWS // cap))
    assert b % group == 0
    grid_spec = pltpu.PrefetchScalarGridSpec(
        num_scalar_prefetch=1,
        grid=(b // group, N_EXPERTS),
        in_specs=[
            pl.BlockSpec((group, l, d), lambda i, j, s: (i, 0, 0)),
            pl.BlockSpec((group, nb, N_EXPERTS, TOKEN_BLOCK), lambda i, j, s: (i, 0, 0, 0)),
            pl.BlockSpec((1, 1, d, f), lambda i, j, s: (layer, j, 0, 0)),
            pl.BlockSpec((1, 1, d, f), lambda i, j, s: (layer, j, 0, 0)),
            pl.BlockSpec((1, 1, f, d), lambda i, j, s: (layer, j, 0, 0)),
        ],
        out_specs=pl.BlockSpec((group, 1, cap, d), lambda i, j, s: (i, j, 0, 0)),
        scratch_shapes=[pltpu.VMEM((group * cap, d), F32)],
    )
    return pl.pallas_call(
        functools.partial(_ffn_kernel, cap=cap, window=window, small_window=small_window),
        grid_spec=grid_spec,
        out_shape=jax.ShapeDtypeStruct((b, N_EXPERTS, cap, d), BF16),
        compiler_params=_params("arbitrary", "arbitrary"),
        name="expert_ffn",
    )(starts, h2, rank_t, w1, w3, w2)


def _combine_kernel(starts_ref, y_ref, rank_t_ref, gate_t_ref, x_ref, g2_ref, lng_ref, lnb_ref,
                    xo_ref, ml_ref, ystack_ref, *, cap, window, small_window, alpha, n_blocks):
    b = pl.program_id(0)
    per_step = rank_t_ref.shape[1]
    first = pl.program_id(1) * per_step
    table = [[(b * N_EXPERTS + e) * n_blocks + first + r for e in range(N_EXPERTS)] for r in range(per_step)]

    def stacked(win):
        slot = lax.broadcasted_iota(jnp.int32, (win, TOKEN_BLOCK), 0)
        contract_rows = (((0,), (0,)), ((), ()))
        for r in range(per_step):
            hi, lo = [], []
            for e in range(N_EXPERTS):
                a = _window_start(starts_ref[table[r][e]], cap, win)
                ystack_ref[r, e * win:(e + 1) * win, :] = y_ref[0, e, pl.ds(a, win), :]
                hit = (slot + a).astype(F32) == rank_t_ref[0, r, e:e + 1, :]
                g_hi, g_lo = _split(jnp.where(hit, gate_t_ref[0, r, e:e + 1, :], 0.0))
                hi.append(g_hi)
                lo.append(g_lo)
            ystack = ystack_ref[r, :N_EXPERTS * win, :]
            ml_ref[r * TOKEN_BLOCK:(r + 1) * TOKEN_BLOCK, :] = (
                lax.dot_general(jnp.concatenate(hi, axis=0), ystack, contract_rows, preferred_element_type=F32)
                + lax.dot_general(jnp.concatenate(lo, axis=0), ystack, contract_rows, preferred_element_type=F32))

    if small_window < window:
        fits = True
        for r in range(per_step):
            for e in range(N_EXPERTS):
                fits = fits & _block_fits(starts_ref, table[r][e], first + r == n_blocks - 1, cap, small_window)
        pl.when(fits)(lambda: stacked(small_window))
        pl.when(jnp.logical_not(fits))(lambda: stacked(window))
    else:
        stacked(window)
    xo_ref[0] = _layer_norm(alpha * x_ref[0] + g2_ref[0] * ml_ref[...]) * lng_ref[...] + lnb_ref[...]


def _combine(starts, y, rank_t, gate_t, x, g2, ln_g, ln_b, cap, alpha):
    b, l, d = x.shape
    nb = l // TOKEN_BLOCK
    window = min(cap, TOKEN_BLOCK + BF16_ROWS)
    small_window = min(cap, SMALL_WINDOW)
    per_step = min(COMBINE_BLOCKS, nb)
    rows = per_step * TOKEN_BLOCK
    assert nb % per_step == 0
    grid_spec = pltpu.PrefetchScalarGridSpec(
        num_scalar_prefetch=1,
        grid=(b, nb // per_step),
        in_specs=[
            pl.BlockSpec((1, N_EXPERTS, cap, d), lambda i, j, s: (i, 0, 0, 0)),
            pl.BlockSpec((1, per_step, N_EXPERTS, TOKEN_BLOCK), lambda i, j, s: (i, j, 0, 0)),
            pl.BlockSpec((1, per_step, N_EXPERTS, TOKEN_BLOCK), lambda i, j, s: (i, j, 0, 0)),
            pl.BlockSpec((1, rows, d), lambda i, j, s: (i, j, 0)),
            pl.BlockSpec((1, 1, d), lambda i, j, s: (i, 0, 0)),
            pl.BlockSpec((1, d), lambda i, j, s: (0, 0)),
            pl.BlockSpec((1, d), lambda i, j, s: (0, 0)),
        ],
        out_specs=pl.BlockSpec((1, rows, d), lambda i, j, s: (i, j, 0)),
        scratch_shapes=[
            pltpu.VMEM((rows, d), F32),
            pltpu.VMEM((per_step, N_EXPERTS * window, d), BF16),
        ],
    )
    return pl.pallas_call(
        functools.partial(_combine_kernel, cap=cap, window=window, small_window=small_window, alpha=alpha,
                          n_blocks=nb),
        grid_spec=grid_spec,
        out_shape=jax.ShapeDtypeStruct((b, l, d), F32),
        compiler_params=_params("arbitrary", "arbitrary"),
        name="moe_combine",
    )(starts, y, rank_t, gate_t, x, g2, ln_g, ln_b)


def _group_mean_matrix(n, group):
    idx = jnp.arange(n) // group
    return (idx[:, None] == idx[None, :]).astype(F32) / group


def _rope_tables(n):
    rows = n // GRID_W
    r = jnp.repeat(jnp.arange(rows, dtype=F32), GRID_W)
    col = jnp.tile(jnp.arange(GRID_W, dtype=F32), rows)
    inv = ROPE_THETA ** (-jnp.arange(ROPE_AXIS_FREQS, dtype=F32) / ROPE_AXIS_FREQS)
    ang = jnp.concatenate([r[:, None] * inv, col[:, None] * inv], axis=-1)
    cos, sin = jnp.cos(ang), jnp.sin(ang)
    cos_t = jnp.tile(cos, (1, LANES // (HEAD_DIM // 2)))
    sin_t = jnp.tile(jnp.concatenate([-sin, sin], axis=-1), (1, LANES // HEAD_DIM))
    return cos_t, sin_t


def _layer_weights(l, w_in, pool_w, pool_scale, sgu_g, sgu_w, sgu_b, q_g, k_g, w_out, ln1_g, ln1_b,
                   w_router, w1, w3, w2, ln2_g, ln2_b):
    d = w_in.shape[1]
    q_scale = HEAD_DIM ** -0.5 * math.log2(math.e)
    gain = jnp.concatenate([jnp.tile(q_g[l] * q_scale, N_HEADS), jnp.tile(k_g[l], N_KV_HEADS)])[None, :]
    pool_bd = jax.scipy.linalg.block_diag(*[pool_w[l, g] for g in range(len(POOL_WINDOWS))])
    wr = jnp.pad(w_router[l], ((0, 0), (0, LANES - N_EXPERTS)))
    wr_hi = wr.astype(BF16)
    return dict(
        w_in=w_in[l].astype(BF16),
        gain=gain,
        pool_w=pool_bd.astype(BF16),
        pool_scale=pool_scale[l][None, :],
        gavg256=_group_mean_matrix(SGU_WIDTH, SGU_WIDTH // SGU_HEADS).astype(BF16),
        sgu_g=sgu_g[l].reshape(1, SGU_WIDTH),
        sgu_w=jnp.swapaxes(sgu_w[l], 0, 1).reshape(SGU_CHUNK, SGU_HEADS * SGU_CHUNK).astype(BF16),
        sgu_b=jnp.repeat(sgu_b[l].T, SGU_WIDTH // SGU_HEADS, axis=1),
        w_out=w_out[l].astype(BF16),
        ln1_g=ln1_g[l][None, :], ln1_b=ln1_b[l][None, :],
        wr_hi=wr_hi, wr_lo=(wr - wr_hi.astype(F32)).astype(BF16),
        layer=l, w1=w1, w3=w3, w2=w2,
        ln2_g=ln2_g[l][None, :], ln2_b=ln2_b[l][None, :],
    )


def _moe(x, h2, logits, g2, lw, tri, alpha):
    b, l, d = x.shape
    cap = EC_CAPACITY_FACTOR * l // N_EXPERTS
    nb = l // TOKEN_BLOCK
    same_expert = jnp.eye(N_EXPERTS, dtype=F32)
    earlier = jnp.kron(jnp.tril(jnp.ones((nb, nb), F32), -1), same_expert).astype(BF16)
    rank_t, gate_t, starts = _route(logits, tri, earlier, cap)
    starts = jnp.swapaxes(starts[:, :, :, 0], 1, 2).reshape(-1)
    y = _expert_ffn(starts, h2, rank_t, lw["w1"], lw["w3"], lw["w2"], lw["layer"], cap)
    return _combine(starts, y, rank_t, gate_t, x, g2, lw["ln2_g"], lw["ln2_b"], cap, alpha)


def kernel(x, c, ctx, c_ctx, w_mod, b_mod, w_in, pool_w, pool_scale, sgu_g, sgu_w, sgu_b, q_g, k_g, w_out,
           ln1_g, ln1_b, w_router, w1, w3, w2, ln2_g, ln2_b):
    batch, seq, d = x.shape
    ctx_len = ctx.shape[1]
    depth = w_mod.shape[0]
    alpha = (2 * depth) ** 0.25
    assert batch + 1 <= COND_ROWS and d % LANES == 0
    assert seq % ROW_TILE == 0 and seq % GRID_W == 0 and seq % ATTN_Q_TILE == 0
    assert ctx_len % KV_TILE == 0 and ctx_len <= ROW_TILE
    for tokens in (seq, ctx_len):
        cap = EC_CAPACITY_FACTOR * tokens // N_EXPERTS
        assert cap % BF16_ROWS == 0 and tokens % TOKEN_BLOCK == 0

    cond = jnp.concatenate([c, c_ctx[None, :], jnp.zeros((COND_ROWS - batch - 1, d), F32)], axis=0)
    mod = _modulation(cond, w_mod, b_mod)

    cos, sin = _rope_tables(seq)
    cos_c = jnp.ones((ctx_len, LANES), F32)
    sin_c = jnp.zeros((ctx_len, LANES), F32)
    gavg128 = _group_mean_matrix(LANES, HEAD_DIM).astype(BF16)
    tri = jnp.triu(jnp.ones((TOKEN_BLOCK, TOKEN_BLOCK), F32)).astype(BF16)

    w1, w3, w2 = w1.astype(BF16), w3.astype(BF16), w2.astype(BF16)

    xc = ctx
    for l in range(depth):
        last = l == depth - 1
        lw = _layer_weights(l, w_in, pool_w, pool_scale, sgu_g, sgu_w, sgu_b, q_g, k_g, w_out, ln1_g, ln1_b,
                            w_router, w1, w3, w2, ln2_g, ln2_b)
        m = mod[l].reshape(COND_ROWS, 6, d)
        sh1, sc1, g1, sh2, sc2, g2 = [m[:batch, i][:, None, :] for i in range(6)]
        csh1, csc1, cg1, csh2, csc2, cg2 = [jnp.broadcast_to(m[batch, i][None, None, :], (batch, 1, d))
                                            for i in range(6)]

        zpu, qt, kl, vtl = _in_projection(x, sh1, sc1, lw["w_in"], lw["gain"], cos, sin, gavg128)
        zpu_c, qt_c, kc, vtc = _in_projection(xc, csh1, csc1, lw["w_in"], lw["gain"], cos_c, sin_c, gavg128)

        attn = _attention(qt, jnp.concatenate([kl, kc], axis=1), jnp.concatenate([vtl, vtc], axis=2))
        if not last:
            attn_c = _attention(qt_c, kc, vtc)
            xc, h2c, logits_c = _mixer_output(zpu_c, attn_c, xc, cg1, csh2, csc2, lw, alpha)
            xc = _moe(xc, h2c, logits_c, cg2, lw, tri, alpha)

        x, h2, logits = _mixer_output(zpu, attn, x, g1, sh2, sc2, lw, alpha)
        x = _moe(x, h2, logits, g2, lw, tri, alpha)
    return x
```

```python
import functools
import math

import jax
import jax.numpy as jnp
from jax import lax
from jax.experimental import pallas as pl
from jax.experimental.pallas import tpu as pltpu

F32 = jnp.float32
BF16 = jnp.bfloat16

GRID_W = 64
POOL_WIDTH = 256
POOL_WINDOWS = (2, 4, 8, 16)
POOL_GROUP = 64
POOL_HALO = 8
SGU_WIDTH = 256
SGU_HEADS = 4
SGU_CHUNK = 128
HEAD_DIM = 64
N_HEADS = 8
N_KV_HEADS = 2
KV_GROUP = N_HEADS // N_KV_HEADS
ATTN_WIDTH = N_HEADS * HEAD_DIM
KV_WIDTH = N_KV_HEADS * HEAD_DIM
PU_WIDTH = POOL_WIDTH + 2 * SGU_WIDTH
QK_WIDTH = ATTN_WIDTH + KV_WIDTH
ROPE_THETA = 10000.0
ROPE_AXIS_FREQS = HEAD_DIM // 4
N_EXPERTS = 16
EC_CAPACITY_FACTOR = 2
LN_EPS = 1e-6

LANES = 128
BF16_ROWS = 16
V_EXT_ROWS = HEAD_DIM + BF16_ROWS
VMEM_LIMIT_BYTES = 56 * 1024 * 1024
F32_ORDER_BITS = 31

ROW_TILE = 1024
KV_TILE = 256
ATTN_Q_TILE = 512
TOKEN_BLOCK = 128
GATHER_UNROLL = 32
COMBINE_BLOCKS = 2
FFN_ROWS = 512
SMALL_WINDOW = 48
COND_ROWS = 16
MOD_COLUMN_TILES = 4
SCORE_FLOOR = -1e30


def _params(*sem):
    return pltpu.CompilerParams(dimension_semantics=sem, vmem_limit_bytes=VMEM_LIMIT_BYTES)


def _split(x):
    hi = x.astype(BF16)
    lo = (x - hi.astype(F32)).astype(BF16)
    return hi, lo


def _dot(a, b):
    return jnp.dot(a, b, preferred_element_type=F32)


def _dot3(a, b_hi, b_lo):
    a_hi, a_lo = _split(a)
    return _dot(a_hi, b_hi) + _dot(a_hi, b_lo) + _dot(a_lo, b_hi)


def _dot_exact_rhs(a, b):
    a_hi, a_lo = _split(a)
    return _dot(a_hi, b) + _dot(a_lo, b)


def _layer_norm(x):
    mu = jnp.mean(x, axis=-1, keepdims=True)
    xc = x - mu
    var = jnp.mean(xc * xc, axis=-1, keepdims=True)
    return xc * lax.rsqrt(var + LN_EPS)


def _sigmoid(x):
    return 1.0 / (1.0 + jnp.exp(-x))


def _gelu_tanh(x):
    c = math.sqrt(2.0 / math.pi)
    return 0.5 * x * (1.0 + jnp.tanh(c * (x + 0.044715 * (x * x * x))))


def _mod_kernel(cond_ref, w_ref, b_ref, o_ref):
    a = cond_ref[...]
    a = a * _sigmoid(a)
    w_hi, w_lo = _split(w_ref[0])
    o_ref[0] = _dot3(a, w_hi, w_lo) + b_ref[0]


def _modulation(cond, w_mod, b_mod):
    depth, d, n = w_mod.shape
    tn = n // MOD_COLUMN_TILES
    return pl.pallas_call(
        _mod_kernel,
        grid=(depth, n // tn),
        in_specs=[
            pl.BlockSpec((COND_ROWS, d), lambda l, j: (0, 0)),
            pl.BlockSpec((1, d, tn), lambda l, j: (l, 0, j)),
            pl.BlockSpec((1, 1, tn), lambda l, j: (l, 0, j)),
        ],
        out_specs=pl.BlockSpec((1, COND_ROWS, tn), lambda l, j: (l, 0, j)),
        out_shape=jax.ShapeDtypeStruct((depth, COND_ROWS, n), F32),
        compiler_params=_params("arbitrary", "arbitrary"),
        name="modulation",
    )(cond, w_mod, b_mod.reshape(depth, 1, n))


def _in_kernel(x_ref, sh_ref, sc_ref, w_ref, gain_ref, cos_ref, sin_ref, gavg_ref,
               zpu_ref, qt_ref, k_ref, vt_ref):
    gavg = gavg_ref[...]
    lane = lax.broadcasted_iota(jnp.int32, (KV_TILE, LANES), 1)
    first_half = (lane & (HEAD_DIM - 1)) < (HEAD_DIM // 2)
    row = lax.broadcasted_iota(jnp.int32, (V_EXT_ROWS - HEAD_DIM, KV_TILE), 0)
    ones_row = (row == 0).astype(F32)

    for r in range(x_ref.shape[1] // KV_TILE):
        rows = slice(r * KV_TILE, (r + 1) * KV_TILE)
        h = _layer_norm(x_ref[0, rows, :]) * (1.0 + sc_ref[0]) + sh_ref[0]
        z = _dot(h.astype(BF16), w_ref[...])
        zpu_ref[0, rows, :] = z[:, :PU_WIDTH]

        cos = cos_ref[rows, :]
        sin = sin_ref[rows, :]
        pieces = []
        for c in range(QK_WIDTH // LANES):
            lo = PU_WIDTH + c * LANES
            y = z[:, lo:lo + LANES]
            ms = _dot_exact_rhs(y * y, gavg)
            y = y * lax.rsqrt(ms + LN_EPS) * gain_ref[:, c * LANES:(c + 1) * LANES]
            partner = jnp.where(first_half,
                                pltpu.roll(y, LANES - HEAD_DIM // 2, 1),
                                pltpu.roll(y, HEAD_DIM // 2, 1))
            pieces.append(y * cos + partner * sin)
        q = jnp.concatenate(pieces[:ATTN_WIDTH // LANES], axis=1)
        qt_ref[0, :, rows] = q.T.astype(BF16)
        k_ref[0, rows, :] = pieces[-1].astype(BF16)

        vt = z[:, PU_WIDTH + QK_WIDTH:].T
        for g in range(N_KV_HEADS):
            ext = jnp.concatenate([vt[g * HEAD_DIM:(g + 1) * HEAD_DIM], ones_row], axis=0)
            vt_ref[0, g, r] = ext.astype(BF16)


def _in_projection(x, sh, sc, w_in, gain, cos, sin, gavg):
    b, l, d = x.shape
    n = w_in.shape[1]
    tm = min(ROW_TILE, l)
    nt = l // tm
    sub = tm // KV_TILE
    vec = pl.BlockSpec((1, 1, d), lambda i, j: (i, 0, 0))
    return pl.pallas_call(
        _in_kernel,
        grid=(b, nt),
        in_specs=[
            pl.BlockSpec((1, tm, d), lambda i, j: (i, j, 0)),
            vec, vec,
            pl.BlockSpec((d, n), lambda i, j: (0, 0)),
            pl.BlockSpec((1, QK_WIDTH), lambda i, j: (0, 0)),
            pl.BlockSpec((tm, LANES), lambda i, j: (j, 0)),
            pl.BlockSpec((tm, LANES), lambda i, j: (j, 0)),
            pl.BlockSpec((LANES, LANES), lambda i, j: (0, 0)),
        ],
        out_specs=[
            pl.BlockSpec((1, tm, PU_WIDTH), lambda i, j: (i, j, 0)),
            pl.BlockSpec((1, ATTN_WIDTH, tm), lambda i, j: (i, 0, j)),
            pl.BlockSpec((1, tm, KV_WIDTH), lambda i, j: (i, j, 0)),
            pl.BlockSpec((1, N_KV_HEADS, sub, V_EXT_ROWS, KV_TILE), lambda i, j: (i, 0, j, 0, 0)),
        ],
        out_shape=[
            jax.ShapeDtypeStruct((b, l, PU_WIDTH), F32),
            jax.ShapeDtypeStruct((b, ATTN_WIDTH, l), BF16),
            jax.ShapeDtypeStruct((b, l, KV_WIDTH), BF16),
            jax.ShapeDtypeStruct((b, N_KV_HEADS, l // KV_TILE, V_EXT_ROWS, KV_TILE), BF16),
        ],
        compiler_params=_params("arbitrary", "arbitrary"),
        name="in_projection",
    )(x, sh, sc, w_in, gain, cos, sin, gavg)


def _attn_kernel(qt_ref, k_ref, vt_ref, o_ref, qpad_ref, m_ref, acc_ref, p_ref, scale_ref, *, n_kv):
    tq = qt_ref.shape[2]
    zeros = jnp.zeros((HEAD_DIM, tq), BF16)
    for h in range(N_HEADS):
        qh = qt_ref[0, h * HEAD_DIM:(h + 1) * HEAD_DIM, :]
        qpad_ref[h] = jnp.concatenate([qh, zeros] if h // KV_GROUP == 0 else [zeros, qh], axis=0)
    m_ref[...] = jnp.full(m_ref.shape, SCORE_FLOOR, F32)
    acc_ref[...] = jnp.zeros(acc_ref.shape, F32)

    def scores(c):
        start = pl.multiple_of(c * KV_TILE, KV_TILE)
        kc = k_ref[0, pl.ds(start, KV_TILE), :]
        return [_dot(kc, qpad_ref[h]) for h in range(N_HEADS)]

    def softmax(s, slot):
        for h in range(N_HEADS):
            m = m_ref[h]
            m_new = jnp.maximum(m, jnp.max(s[h], axis=0, keepdims=True))
            p_ref[slot, h] = jnp.exp2(s[h] - m_new).astype(BF16)
            scale_ref[slot, h] = jnp.exp2(m - m_new)
            m_ref[h] = m_new

    def values(c, slot):
        for h in range(N_HEADS):
            acc_ref[h] = acc_ref[h] * scale_ref[slot, h] + _dot(vt_ref[0, h // KV_GROUP, c], p_ref[slot, h])

    def step(c, slot_prev, slot_cur):
        s = scores(c)
        values(c - 1, slot_prev)
        softmax(s, slot_cur)

    softmax(scores(0), 0)
    pairs, leftover = divmod(n_kv - 1, 2)

    def body(i, carry):
        step(2 * i + 1, 0, 1)
        step(2 * i + 2, 1, 0)
        return carry

    lax.fori_loop(0, pairs, body, 0)
    if leftover:
        step(n_kv - 1, 0, 1)
    values(n_kv - 1, leftover)
    blocks = []
    for hp in range(N_HEADS // 2):
        pair = []
        for h in (2 * hp, 2 * hp + 1):
            acc = acc_ref[h]
            pair.append(acc[:HEAD_DIM] / acc[HEAD_DIM:HEAD_DIM + 1])
        blocks.append(jnp.concatenate(pair, axis=0).T)
    o_ref[0] = jnp.concatenate(blocks, axis=1).astype(BF16)


def _attention(qt, k, vt):
    b, _, lq = qt.shape
    lk = k.shape[1]
    n_kv = lk // KV_TILE
    tq = min(ATTN_Q_TILE, lq)
    return pl.pallas_call(
        functools.partial(_attn_kernel, n_kv=n_kv),
        grid=(b, lq // tq),
        in_specs=[
            pl.BlockSpec((1, ATTN_WIDTH, tq), lambda i, j: (i, 0, j)),
            pl.BlockSpec((1, lk, KV_WIDTH), lambda i, j: (i, 0, 0)),
            pl.BlockSpec((1, N_KV_HEADS, n_kv, V_EXT_ROWS, KV_TILE), lambda i, j: (i, 0, 0, 0, 0)),
        ],
        out_specs=pl.BlockSpec((1, tq, ATTN_WIDTH), lambda i, j: (i, j, 0)),
        out_shape=jax.ShapeDtypeStruct((b, lq, ATTN_WIDTH), BF16),
        scratch_shapes=[
            pltpu.VMEM((N_HEADS, 2 * HEAD_DIM, tq), BF16),
            pltpu.VMEM((N_HEADS, 1, tq), F32),
            pltpu.VMEM((N_HEADS, V_EXT_ROWS, tq), F32),
            pltpu.VMEM((2, N_HEADS, KV_TILE, tq), BF16),
            pltpu.VMEM((2, N_HEADS, 1, tq), F32),
        ],
        compiler_params=_params("arbitrary", "arbitrary"),
        name="attention",
    )(qt, k, vt)


def _mix_kernel(zp_ref, zn_ref, z_ref, attn_ref, x_ref, g1_ref, sh2_ref, sc2_ref,
                poolw_ref, pscale_ref, gavg_ref, sgug_ref, sguw_ref, sgub_ref,
                wout_ref, lng_ref, lnb_ref, wr_hi_ref, wr_lo_ref,
                xo_ref, h2_ref, logit_ref, *, seq_len, alpha):
    j = pl.program_id(1)
    nt = pl.num_programs(1)
    z = z_ref[0]
    tm = z.shape[0]
    lane = lax.broadcasted_iota(jnp.int32, (tm, POOL_WIDTH), 1)
    group = lax.shift_right_logical(lane, POOL_GROUP.bit_length() - 1)

    p = z[:, :POOL_WIDTH]
    prev = zp_ref[0] * (j > 0).astype(F32)
    nxt = zn_ref[0] * (j < nt - 1).astype(F32)
    ext = jnp.concatenate([prev, p, nxt], axis=0)
    n_ext = ext.shape[0]
    s2 = ext + pltpu.roll(ext, 1, 0)
    s4 = pltpu.roll(s2, 1, 0) + pltpu.roll(s2, n_ext - 1, 0)
    s8 = pltpu.roll(s4, 2, 0) + pltpu.roll(s4, n_ext - 2, 0)
    s16 = pltpu.roll(s8, 4, 0) + pltpu.roll(s8, n_ext - 4, 0)
    sums = [s[POOL_HALO:POOL_HALO + tm] for s in (s2, s4, s8, s16)]
    wsum = jnp.where(group == 0, sums[0], jnp.where(group == 1, sums[1], jnp.where(group == 2, sums[2], sums[3])))
    half = jnp.where(group == 0, 1, jnp.where(group == 1, 2, jnp.where(group == 2, 4, 8)))
    t = j * tm + lax.broadcasted_iota(jnp.int32, (tm, POOL_WIDTH), 0)
    cnt = jnp.minimum(t + half, seq_len) - jnp.maximum(t - half, 0)
    pin = (wsum / cnt.astype(F32) - p).astype(BF16)

    gavg = gavg_ref[...]
    head = lax.shift_right_logical(lax.broadcasted_iota(jnp.int32, (SGU_CHUNK, SGU_WIDTH), 1),
                                   (SGU_WIDTH // SGU_HEADS).bit_length() - 1)
    split = POOL_WIDTH + SGU_WIDTH

    for r in range(tm // KV_TILE):
        sub = slice(r * KV_TILE, (r + 1) * KV_TILE)
        pool = _dot(pin[sub], poolw_ref[...]) * pscale_ref[...]

        sgu_parts = []
        for c in range(KV_TILE // SGU_CHUNK):
            rows = slice(r * KV_TILE + c * SGU_CHUNK, r * KV_TILE + (c + 1) * SGU_CHUNK)
            u = _gelu_tanh(z[rows, POOL_WIDTH:POOL_WIDTH + SGU_WIDTH])
            v = _gelu_tanh(z[rows, POOL_WIDTH + SGU_WIDTH:])
            mu = _dot_exact_rhs(v, gavg)
            vc = v - mu
            var = _dot_exact_rhs(vc * vc, gavg)
            vn = vc * lax.rsqrt(var + LN_EPS) * sgug_ref[...]
            stacked = jnp.concatenate(
                [jnp.where(head == hd, vn, 0.0).astype(BF16) for hd in range(SGU_HEADS)], axis=0)
            sgu_parts.append(u * (_dot(sguw_ref[...], stacked) + sgub_ref[...]))
        sgu = jnp.concatenate(sgu_parts, axis=0)

        ps = jnp.concatenate([pool, sgu], axis=1).astype(BF16)
        y = _dot(ps, wout_ref[:split, :]) + _dot(attn_ref[0, sub, :], wout_ref[split:, :])
        xn = _layer_norm(alpha * x_ref[0, sub, :] + g1_ref[0] * y) * lng_ref[...] + lnb_ref[...]
        xo_ref[0, sub, :] = xn

        h2 = _layer_norm(xn) * (1.0 + sc2_ref[0]) + sh2_ref[0]
        h2_ref[0, sub, :] = h2.astype(BF16)
        logits_t = _dot3(h2, wr_hi_ref[...], wr_lo_ref[...]).T
        for c in range(KV_TILE // TOKEN_BLOCK):
            logit_ref[0, r * (KV_TILE // TOKEN_BLOCK) + c] = (
                logits_t[:N_EXPERTS, c * TOKEN_BLOCK:(c + 1) * TOKEN_BLOCK])


def _mixer_output(zpu, attn, x, g1, sh2, sc2, lw, alpha):
    b, l, d = x.shape
    tm = min(ROW_TILE, l)
    nt = l // tm
    hb = tm // POOL_HALO
    n_halo = l // POOL_HALO
    vec = pl.BlockSpec((1, 1, d), lambda i, j: (i, 0, 0))

    def full(a):
        nd = a.ndim
        return pl.BlockSpec(a.shape, lambda i, j: (0,) * nd)

    weights = [lw["pool_w"], lw["pool_scale"], lw["gavg256"], lw["sgu_g"], lw["sgu_w"], lw["sgu_b"],
               lw["w_out"], lw["ln1_g"], lw["ln1_b"], lw["wr_hi"], lw["wr_lo"]]
    return pl.pallas_call(
        functools.partial(_mix_kernel, seq_len=l, alpha=alpha),
        grid=(b, nt),
        in_specs=[
            pl.BlockSpec((1, POOL_HALO, POOL_WIDTH), lambda i, j: (i, jnp.maximum(j * hb - 1, 0), 0)),
            pl.BlockSpec((1, POOL_HALO, POOL_WIDTH), lambda i, j: (i, jnp.minimum((j + 1) * hb, n_halo - 1), 0)),
            pl.BlockSpec((1, tm, PU_WIDTH), lambda i, j: (i, j, 0)),
            pl.BlockSpec((1, tm, ATTN_WIDTH), lambda i, j: (i, j, 0)),
            pl.BlockSpec((1, tm, d), lambda i, j: (i, j, 0)),
            vec, vec, vec,
        ] + [full(w) for w in weights],
        out_specs=[
            pl.BlockSpec((1, tm, d), lambda i, j: (i, j, 0)),
            pl.BlockSpec((1, tm, d), lambda i, j: (i, j, 0)),
            pl.BlockSpec((1, tm // TOKEN_BLOCK, N_EXPERTS, TOKEN_BLOCK), lambda i, j: (i, j, 0, 0)),
        ],
        out_shape=[
            jax.ShapeDtypeStruct((b, l, d), F32),
            jax.ShapeDtypeStruct((b, l, d), BF16),
            jax.ShapeDtypeStruct((b, l // TOKEN_BLOCK, N_EXPERTS, TOKEN_BLOCK), F32),
        ],
        compiler_params=_params("arbitrary", "arbitrary"),
        name="mixer_output",
    )(zpu, zpu, zpu, attn, x, g1, sh2, sc2, *weights)


def _route_kernel(logit_ref, tri_ref, earlier_ref, rank_ref, gate_ref, start_ref, bits_ref, *, cap):
    nb = logit_ref.shape[1]
    x = logit_ref[0]
    e = jnp.exp(x - jnp.max(x, axis=1, keepdims=True))
    aff = e / jnp.sum(e, axis=1, keepdims=True)
    gate_ref[0] = aff
    bits_ref[...] = pltpu.bitcast(aff, jnp.int32)

    def count(mask):
        return jnp.sum(jnp.sum(mask.astype(F32), axis=0), axis=1, keepdims=True)

    def bisect(i, thr):
        cand = thr | lax.shift_left(jnp.int32(1), F32_ORDER_BITS - 1 - i)
        return jnp.where(count(bits_ref[...] >= cand) >= float(cap), cand, thr)

    thr = lax.fori_loop(0, F32_ORDER_BITS, bisect, jnp.zeros((N_EXPERTS, 1), jnp.int32))
    bits = bits_ref[...]
    need = float(cap) - count(bits > thr)
    tri = tri_ref[...]
    ones = jnp.ones((TOKEN_BLOCK, TOKEN_BLOCK), BF16)
    earlier = earlier_ref[...]

    def running_count(mask):
        flat = mask.reshape(nb * N_EXPERTS, TOKEN_BLOCK).astype(BF16)
        totals = _dot(flat, ones)
        before = _dot(earlier, totals.astype(BF16))
        return (_dot(flat, tri) + before).reshape(mask.shape), before.reshape(mask.shape)

    eq = (bits == thr).astype(F32)
    sel = jnp.where((bits > thr) | ((eq > 0.0) & (running_count(eq)[0] <= need)), 1.0, 0.0)
    chosen, before = running_count(sel)
    rank_ref[0] = jnp.where(sel > 0.0, chosen - 1.0, -1.0)
    start_ref[0] = before.astype(jnp.int32)


def _route(logits_t, tri, earlier, cap):
    b, nb = logits_t.shape[:2]
    blocks = pl.BlockSpec((1, nb, N_EXPERTS, TOKEN_BLOCK), lambda i: (i, 0, 0, 0))
    table = jax.ShapeDtypeStruct((b, nb, N_EXPERTS, TOKEN_BLOCK), F32)
    return pl.pallas_call(
        functools.partial(_route_kernel, cap=cap),
        grid=(b,),
        in_specs=[blocks, pl.BlockSpec((TOKEN_BLOCK, TOKEN_BLOCK), lambda i: (0, 0)),
                  pl.BlockSpec(earlier.shape, lambda i: (0, 0))],
        out_specs=[blocks, blocks, blocks],
        out_shape=[
            table,
            table,
            jax.ShapeDtypeStruct(table.shape, jnp.int32),
        ],
        scratch_shapes=[pltpu.VMEM((nb, N_EXPERTS, TOKEN_BLOCK), jnp.int32)],
        compiler_params=_params("arbitrary"),
        name="routing",
    )(logits_t, tri, earlier)


def _window_start(s0, cap, window):
    return pl.multiple_of(jnp.minimum(s0 & ~(BF16_ROWS - 1), cap - window), BF16_ROWS)


def _block_fits(starts_ref, idx, is_last, cap, window):
    s0 = starts_ref[idx]
    s1 = jnp.where(is_last, cap, starts_ref[jnp.where(is_last, idx, idx + 1)])
    return (s1 - _window_start(s0, cap, window)) <= window


def _ffn_kernel(starts_ref, h_ref, rank_t_ref, w1_ref, w3_ref, w2_ref, y_ref, xg_ref,
                *, cap, window, small_window):
    e = pl.program_id(1)
    group, nb = rank_t_ref.shape[0], rank_t_ref.shape[1]
    first = pl.program_id(0) * group
    bases = [((first + i) * N_EXPERTS + e) * nb for i in range(group)]
    xg_ref[...] = jnp.zeros(xg_ref.shape, F32)

    def gather_all(win):
        slot = lax.broadcasted_iota(jnp.int32, (win, TOKEN_BLOCK), 0)
        for i in range(group):
            def gather(jb, carry, i=i):
                a = _window_start(starts_ref[bases[i] + jb], cap, win)
                rank = rank_t_ref[i, jb, pl.ds(e, 1), :]
                onehot = ((slot + a).astype(F32) == rank).astype(BF16)
                rows = pl.ds(pl.multiple_of(jb * TOKEN_BLOCK, TOKEN_BLOCK), TOKEN_BLOCK)
                xg_ref[pl.ds(i * cap + a, win), :] += _dot(onehot, h_ref[i, rows, :])
                return carry

            lax.fori_loop(0, nb, gather, 0, unroll=min(nb, GATHER_UNROLL))

    if small_window < window:
        fits = True
        for i in range(group):
            fits = lax.fori_loop(
                0, nb,
                lambda jb, ok, i=i: ok & _block_fits(starts_ref, bases[i] + jb, jb == nb - 1, cap, small_window),
                fits, unroll=True)
        pl.when(fits)(lambda: gather_all(small_window))
        pl.when(jnp.logical_not(fits))(lambda: gather_all(window))
    else:
        gather_all(window)

    xg = xg_ref[...].astype(BF16)
    a = _dot(xg, w1_ref[0, 0])
    hid = (a * _sigmoid(a)) * _dot(xg, w3_ref[0, 0])
    y = _dot(hid.astype(BF16), w2_ref[0, 0]).astype(BF16)
    for i in range(group):
        y_ref[i, 0] = y[i * cap:(i + 1) * cap]


def _expert_ffn(starts, h2, rank_t, w1, w3, w2, layer, cap):
    b, l, d = h2.shape
    f = w1.shape[3]
    nb = l // TOKEN_BLOCK
    window = min(cap, TOKEN_BLOCK + BF16_ROWS)
    small_window = min(cap, SMALL_WINDOW)
    group = max(1, min(b, FFN_ROWS // cap))
    assert b % group == 0
    grid_spec = pltpu.PrefetchScalarGridSpec(
        num_scalar_prefetch=1,
        grid=(b // group, N_EXPERTS),
        in_specs=[
            pl.BlockSpec((group, l, d), lambda i, j, s: (i, 0, 0)),
            pl.BlockSpec((group, nb, N_EXPERTS, TOKEN_BLOCK), lambda i, j, s: (i, 0, 0, 0)),
            pl.BlockSpec((1, 1, d, f), lambda i, j, s: (layer, j, 0, 0)),
            pl.BlockSpec((1, 1, d, f), lambda i, j, s: (layer, j, 0, 0)),
            pl.BlockSpec((1, 1, f, d), lambda i, j, s: (layer, j, 0, 0)),
        ],
        out_specs=pl.BlockSpec((group, 1, cap, d), lambda i, j, s: (i, j, 0, 0)),
        scratch_shapes=[pltpu.VMEM((group * cap, d), F32)],
    )
    return pl.pallas_call(
        functools.partial(_ffn_kernel, cap=cap, window=window, small_window=small_window),
        grid_spec=grid_spec,
        out_shape=jax.ShapeDtypeStruct((b, N_EXPERTS, cap, d), BF16),
        compiler_params=_params("arbitrary", "arbitrary"),
        name="expert_ffn",
    )(starts, h2, rank_t, w1, w3, w2)


def _combine_kernel(starts_ref, y_ref, rank_t_ref, gate_t_ref, x_ref, g2_ref, lng_ref, lnb_ref,
                    xo_ref, ml_ref, ystack_ref, *, cap, window, small_window, alpha, n_blocks):
    b = pl.program_id(0)
    per_step = rank_t_ref.shape[1]
    first = pl.program_id(1) * per_step
    table = [[(b * N_EXPERTS + e) * n_blocks + first + r for e in range(N_EXPERTS)] for r in range(per_step)]

    def stacked(win):
        slot = lax.broadcasted_iota(jnp.int32, (win, TOKEN_BLOCK), 0)
        contract_rows = (((0,), (0,)), ((), ()))
        for r in range(per_step):
            gated = []
            for e in range(N_EXPERTS):
                a = _window_start(starts_ref[table[r][e]], cap, win)
                ystack_ref[r, e * win:(e + 1) * win, :] = y_ref[0, e, pl.ds(a, win), :]
                hit = (slot + a).astype(F32) == rank_t_ref[0, r, e:e + 1, :]
                gated.append(jnp.where(hit, gate_t_ref[0, r, e:e + 1, :], 0.0).astype(BF16))
            ystack = ystack_ref[r, :N_EXPERTS * win, :]
            ml_ref[r * TOKEN_BLOCK:(r + 1) * TOKEN_BLOCK, :] = lax.dot_general(
                jnp.concatenate(gated, axis=0), ystack, contract_rows, preferred_element_type=F32)

    if small_window < window:
        fits = True
        for r in range(per_step):
            for e in range(N_EXPERTS):
                fits = fits & _block_fits(starts_ref, table[r][e], first + r == n_blocks - 1, cap, small_window)
        pl.when(fits)(lambda: stacked(small_window))
        pl.when(jnp.logical_not(fits))(lambda: stacked(window))
    else:
        stacked(window)
    xo_ref[0] = _layer_norm(alpha * x_ref[0] + g2_ref[0] * ml_ref[...]) * lng_ref[...] + lnb_ref[...]


def _combine(starts, y, rank_t, gate_t, x, g2, ln_g, ln_b, cap, alpha):
    b, l, d = x.shape
    nb = l // TOKEN_BLOCK
    window = min(cap, TOKEN_BLOCK + BF16_ROWS)
    small_window = min(cap, SMALL_WINDOW)
    per_step = min(COMBINE_BLOCKS, nb)
    rows = per_step * TOKEN_BLOCK
    assert nb % per_step == 0
    grid_spec = pltpu.PrefetchScalarGridSpec(
        num_scalar_prefetch=1,
        grid=(b, nb // per_step),
        in_specs=[
            pl.BlockSpec((1, N_EXPERTS, cap, d), lambda i, j, s: (i, 0, 0, 0)),
            pl.BlockSpec((1, per_step, N_EXPERTS, TOKEN_BLOCK), lambda i, j, s: (i, j, 0, 0)),
            pl.BlockSpec((1, per_step, N_EXPERTS, TOKEN_BLOCK), lambda i, j, s: (i, j, 0, 0)),
            pl.BlockSpec((1, rows, d), lambda i, j, s: (i, j, 0)),
            pl.BlockSpec((1, 1, d), lambda i, j, s: (i, 0, 0)),
            pl.BlockSpec((1, d), lambda i, j, s: (0, 0)),
            pl.BlockSpec((1, d), lambda i, j, s: (0, 0)),
        ],
        out_specs=pl.BlockSpec((1, rows, d), lambda i, j, s: (i, j, 0)),
        scratch_shapes=[
            pltpu.VMEM((rows, d), F32),
            pltpu.VMEM((per_step, N_EXPERTS * window, d), BF16),
        ],
    )
    return pl.pallas_call(
        functools.partial(_combine_kernel, cap=cap, window=window, small_window=small_window, alpha=alpha,
                          n_blocks=nb),
        grid_spec=grid_spec,
        out_shape=jax.ShapeDtypeStruct((b, l, d), F32),
        compiler_params=_params("arbitrary", "arbitrary"),
        name="moe_combine",
    )(starts, y, rank_t, gate_t, x, g2, ln_g, ln_b)


def _group_mean_matrix(n, group):
    idx = jnp.arange(n) // group
    return (idx[:, None] == idx[None, :]).astype(F32) / group


def _rope_tables(n):
    rows = n // GRID_W
    r = jnp.repeat(jnp.arange(rows, dtype=F32), GRID_W)
    col = jnp.tile(jnp.arange(GRID_W, dtype=F32), rows)
    inv = ROPE_THETA ** (-jnp.arange(ROPE_AXIS_FREQS, dtype=F32) / ROPE_AXIS_FREQS)
    ang = jnp.concatenate([r[:, None] * inv, col[:, None] * inv], axis=-1)
    cos, sin = jnp.cos(ang), jnp.sin(ang)
    cos_t = jnp.tile(cos, (1, LANES // (HEAD_DIM // 2)))
    sin_t = jnp.tile(jnp.concatenate([-sin, sin], axis=-1), (1, LANES // HEAD_DIM))
    return cos_t, sin_t


def _layer_weights(l, w_in, pool_w, pool_scale, sgu_g, sgu_w, sgu_b, q_g, k_g, w_out, ln1_g, ln1_b,
                   w_router, w1, w3, w2, ln2_g, ln2_b):
    d = w_in.shape[1]
    q_scale = HEAD_DIM ** -0.5 * math.log2(math.e)
    gain = jnp.concatenate([jnp.tile(q_g[l] * q_scale, N_HEADS), jnp.tile(k_g[l], N_KV_HEADS)])[None, :]
    pool_bd = jax.scipy.linalg.block_diag(*[pool_w[l, g] for g in range(len(POOL_WINDOWS))])
    wr = jnp.pad(w_router[l], ((0, 0), (0, LANES - N_EXPERTS)))
    wr_hi = wr.astype(BF16)
    return dict(
        w_in=w_in[l].astype(BF16),
        gain=gain,
        pool_w=pool_bd.astype(BF16),
        pool_scale=pool_scale[l][None, :],
        gavg256=_group_mean_matrix(SGU_WIDTH, SGU_WIDTH // SGU_HEADS).astype(BF16),
        sgu_g=sgu_g[l].reshape(1, SGU_WIDTH),
        sgu_w=jnp.swapaxes(sgu_w[l], 0, 1).reshape(SGU_CHUNK, SGU_HEADS * SGU_CHUNK).astype(BF16),
        sgu_b=jnp.repeat(sgu_b[l].T, SGU_WIDTH // SGU_HEADS, axis=1),
        w_out=w_out[l].astype(BF16),
        ln1_g=ln1_g[l][None, :], ln1_b=ln1_b[l][None, :],
        wr_hi=wr_hi, wr_lo=(wr - wr_hi.astype(F32)).astype(BF16),
        layer=l, w1=w1, w3=w3, w2=w2,
        ln2_g=ln2_g[l][None, :], ln2_b=ln2_b[l][None, :],
    )


def _moe(x, h2, logits, g2, lw, tri, alpha):
    b, l, d = x.shape
    cap = EC_CAPACITY_FACTOR * l // N_EXPERTS
    nb = l // TOKEN_BLOCK
    same_expert = jnp.eye(N_EXPERTS, dtype=F32)
    earlier = jnp.kron(jnp.tril(jnp.ones((nb, nb), F32), -1), same_expert).astype(BF16)
    rank_t, gate_t, starts = _route(logits, tri, earlier, cap)
    starts = jnp.swapaxes(starts[:, :, :, 0], 1, 2).reshape(-1)
    y = _expert_ffn(starts, h2, rank_t, lw["w1"], lw["w3"], lw["w2"], lw["layer"], cap)
    return _combine(starts, y, rank_t, gate_t, x, g2, lw["ln2_g"], lw["ln2_b"], cap, alpha)


def kernel(x, c, ctx, c_ctx, w_mod, b_mod, w_in, pool_w, pool_scale, sgu_g, sgu_w, sgu_b, q_g, k_g, w_out,
           ln1_g, ln1_b, w_router, w1, w3, w2, ln2_g, ln2_b):
    batch, seq, d = x.shape
    ctx_len = ctx.shape[1]
    depth = w_mod.shape[0]
    alpha = (2 * depth) ** 0.25
    assert batch + 1 <= COND_ROWS and d % LANES == 0
    assert seq % ROW_TILE == 0 and seq % GRID_W == 0 and seq % ATTN_Q_TILE == 0
    assert ctx_len % KV_TILE == 0 and ctx_len <= ROW_TILE
    for tokens in (seq, ctx_len):
        cap = EC_CAPACITY_FACTOR * tokens // N_EXPERTS
        assert cap % BF16_ROWS == 0 and tokens % TOKEN_BLOCK == 0

    cond = jnp.concatenate([c, c_ctx[None, :], jnp.zeros((COND_ROWS - batch - 1, d), F32)], axis=0)
    mod = _modulation(cond, w_mod, b_mod)

    cos, sin = _rope_tables(seq)
    cos_c = jnp.ones((ctx_len, LANES), F32)
    sin_c = jnp.zeros((ctx_len, LANES), F32)
    gavg128 = _group_mean_matrix(LANES, HEAD_DIM).astype(BF16)
    tri = jnp.triu(jnp.ones((TOKEN_BLOCK, TOKEN_BLOCK), F32)).astype(BF16)

    w1, w3, w2 = w1.astype(BF16), w3.astype(BF16), w2.astype(BF16)

    xc = ctx
    for l in range(depth):
        last = l == depth - 1
        lw = _layer_weights(l, w_in, pool_w, pool_scale, sgu_g, sgu_w, sgu_b, q_g, k_g, w_out, ln1_g, ln1_b,
                            w_router, w1, w3, w2, ln2_g, ln2_b)
        m = mod[l].reshape(COND_ROWS, 6, d)
        sh1, sc1, g1, sh2, sc2, g2 = [m[:batch, i][:, None, :] for i in range(6)]
        csh1, csc1, cg1, csh2, csc2, cg2 = [jnp.broadcast_to(m[batch, i][None, None, :], (batch, 1, d))
                                            for i in range(6)]

        zpu, qt, kl, vtl = _in_projection(x, sh1, sc1, lw["w_in"], lw["gain"], cos, sin, gavg128)
        zpu_c, qt_c, kc, vtc = _in_projection(xc, csh1, csc1, lw["w_in"], lw["gain"], cos_c, sin_c, gavg128)

        attn = _attention(qt, jnp.concatenate([kl, kc], axis=1), jnp.concatenate([vtl, vtc], axis=2))
        if not last:
            attn_c = _attention(qt_c, kc, vtc)
            xc, h2c, logits_c = _mixer_output(zpu_c, attn_c, xc, cg1, csh2, csc2, lw, alpha)
            xc = _moe(xc, h2c, logits_c, cg2, lw, tri, alpha)

        x, h2, logits = _mixer_output(zpu, attn, x, g1, sh2, sc2, lw, alpha)
        x = _moe(x, h2, logits, g2, lw, tri, alpha)
    return x
```
